```python
import math
import jax, jax.numpy as jnp
from jax import lax
import numpy as np

D_MODEL = 2048
BATCH = 2
SEQ = 8192
DEPTH = 4

D_MIX = D_MODEL
D_SSM = D_MIX // 2
SSM_GROUP = 16
N_SSM_GROUPS = D_SSM // SSM_GROUP
SSM_STATE = 64
DT_MIN = 1e-3
DT_MAX = 1e-1

N_HEADS = 16
N_KV = 4
GQA_R = N_HEADS // N_KV
HEAD_DIM = (D_MIX - D_SSM) // N_HEADS
CMP_BLOCK = 32
CMP_STRIDE = 16
CMP_HIDDEN = 128
SEL_BLOCK = 64
SEL_TOP = 16
WINDOW = 512
Q_BLOCK = 128
SEL_RATIO = SEL_BLOCK // CMP_STRIDE
CMP_RATIO = CMP_BLOCK // CMP_STRIDE
FORCED_SCORE = 1e4

N_BUCKETS = 32
RPB_MAX_EXACT = 16
RPB_MAX_DIST = 1024

D_FF = ((8 * D_MODEL // 3 + 255) // 256) * 256
NORM_EPS = 1e-6

IN_SIZES = [D_SSM, N_HEADS * HEAD_DIM] + [N_KV * HEAD_DIM] * 6 + [3 * N_HEADS]
IN_COLS = int(sum(IN_SIZES))
IN_SPLITS = [int(v) for v in np.cumsum(IN_SIZES)[:-1]]

kernel_name = "hymba_s5_nsa_hybrid_trunk"


def rms_norm(x, g):
    xf = x.astype(jnp.float32)
    y = xf * lax.rsqrt(jnp.mean(xf * xf, axis=-1, keepdims=True) + NORM_EPS)
    return (y * g.astype(jnp.float32)).astype(x.dtype)


def masked_softmax(logits, mask):
    logits = jnp.where(mask, logits.astype(jnp.float32), -jnp.inf)
    m = jnp.max(logits, axis=-1, keepdims=True)
    m = jnp.where(jnp.isfinite(m), m, 0.0)
    e = jnp.exp(logits - m)
    return e / jnp.maximum(jnp.sum(e, axis=-1, keepdims=True), 1e-30)


def rel_bucket(dist):
    n = jnp.maximum(dist, 0)
    nf = jnp.maximum(n, 1).astype(jnp.float32)
    large = RPB_MAX_EXACT + (jnp.log(nf / RPB_MAX_EXACT) / math.log(RPB_MAX_DIST / RPB_MAX_EXACT)
                             * (N_BUCKETS - RPB_MAX_EXACT)).astype(jnp.int32)
    large = jnp.minimum(large, N_BUCKETS - 1)
    return jnp.where(n < RPB_MAX_EXACT, n, large)


def _complex_linear_combine(e1, e2):
    a1r, a1i, b1r, b1i = e1
    a2r, a2i, b2r, b2i = e2
    return (a2r * a1r - a2i * a1i,
            a2r * a1i + a2i * a1r,
            a2r * b1r - a2i * b1i + b2r,
            a2r * b1i + a2i * b1r + b2i)


def s5_glu(u, a_re, a_im, log_dt, b_re, b_im, c_re, c_im, d, w_glu, b_glu):
    B_, L_ = u.shape[:2]
    ug = u.astype(jnp.float32).reshape(B_, L_, N_SSM_GROUPS, SSM_GROUP)
    lam_re = jnp.minimum(a_re.astype(jnp.float32), -1e-4)
    lam_im = a_im.astype(jnp.float32)
    dt = jnp.exp(log_dt.astype(jnp.float32))[:, None]
    mag = jnp.exp(lam_re * dt)
    ab_re = mag * jnp.cos(lam_im * dt)
    ab_im = mag * jnp.sin(lam_im * dt)
    nr, ni = ab_re - 1.0, ab_im
    den = lam_re * lam_re + lam_im * lam_im
    f_re = (nr * lam_re + ni * lam_im) / den
    f_im = (ni * lam_re - nr * lam_im) / den
    br, bi = b_re.astype(jnp.float32), b_im.astype(jnp.float32)
    bb_re = f_re[..., None] * br - f_im[..., None] * bi
    bb_im = f_re[..., None] * bi + f_im[..., None] * br
    bu_re = jnp.einsum('blgh,gph->blgp', ug, bb_re)
    bu_im = jnp.einsum('blgh,gph->blgp', ug, bb_im)
    a_b_re = jnp.broadcast_to(ab_re, bu_re.shape)
    a_b_im = jnp.broadcast_to(ab_im, bu_im.shape)
    _, _, x_re, x_im = lax.associative_scan(_complex_linear_combine,
                                            (a_b_re, a_b_im, bu_re, bu_im), axis=1)
    y = (jnp.einsum('blgp,ghp->blgh', x_re, c_re.astype(jnp.float32))
         - jnp.einsum('blgp,ghp->blgh', x_im, c_im.astype(jnp.float32))
         + d.astype(jnp.float32).reshape(N_SSM_GROUPS, SSM_GROUP) * ug)
    y = jax.nn.gelu(y.reshape(B_, L_, D_SSM))
    ab = y @ w_glu.astype(jnp.float32) + b_glu.astype(jnp.float32)
    a, b = jnp.split(ab, 2, axis=-1)
    return (a * jax.nn.sigmoid(b)).astype(u.dtype)


def compress(k, pos, w1, w2):
    B_, L_ = k.shape[:2]
    c = k.reshape(B_, L_ // CMP_STRIDE, CMP_STRIDE, N_KV, HEAD_DIM)
    blocks = jnp.concatenate([c[:, :-1], c[:, 1:]], axis=2)
    blocks = blocks + pos[None, None, :, None, :]
    n_cmp = blocks.shape[1]
    flat = blocks.transpose(0, 1, 3, 2, 4).reshape(B_, n_cmp, N_KV, CMP_BLOCK * HEAD_DIM)
    return jax.nn.gelu(flat @ w1) @ w2


def nsa_mixer(q, kc, vc, ks, vs, kw, vw, gates,
              pos_k, w1_k, w2_k, pos_v, w1_v, w2_v, rpb_table):
    B_, L_ = q.shape[:2]
    n_qb = L_ // Q_BLOCK
    n_sel = L_ // SEL_BLOCK
    top = min(SEL_TOP, n_sel)
    kv_shape = (B_, L_, N_KV, HEAD_DIM)
    q = q.reshape(B_, L_, N_KV, GQA_R, HEAD_DIM) * (HEAD_DIM ** -0.5)
    gates = jax.nn.sigmoid(gates.astype(jnp.float32)).reshape(B_, L_, N_KV, GQA_R, 3)

    k_cmp = compress(kc.reshape(kv_shape), pos_k, w1_k, w2_k)
    v_cmp = compress(vc.reshape(kv_shape), pos_v, w1_v, w2_v)
    n_cmp = k_cmp.shape[1]
    cmp_end = jnp.arange(n_cmp) * CMP_STRIDE + (CMP_BLOCK - 1)

    k_sel_blk = ks.reshape(B_, n_sel, SEL_BLOCK, N_KV, HEAD_DIM).transpose(0, 3, 1, 2, 4)
    v_sel_blk = vs.reshape(B_, n_sel, SEL_BLOCK, N_KV, HEAD_DIM).transpose(0, 3, 1, 2, 4)
    pad_w = ((0, 0), (WINDOW, 0), (0, 0), (0, 0))
    k_win = jnp.pad(kw.reshape(kv_shape), pad_w)
    v_win = jnp.pad(vw.reshape(kv_shape), pad_w)

    table_gr = rpb_table.astype(jnp.float32).reshape(N_BUCKETS, N_KV, GQA_R)
    table_g = table_gr.transpose(1, 0, 2)

    qi = jnp.arange(Q_BLOCK)
    kj = jnp.arange(Q_BLOCK + WINDOW)
    win_dist = qi[:, None] + WINDOW - kj[None, :]
    win_rel_mask = (win_dist >= 0) & (win_dist < WINDOW)
    win_bias = table_gr[rel_bucket(win_dist)].transpose(2, 3, 0, 1)

    bidx = jnp.arange(B_)[:, None, None, None]
    gidx = jnp.arange(N_KV)[None, :, None, None]
    blk_ids = jnp.arange(n_sel)
    s_max = SEL_RATIO + CMP_RATIO - 2
    coefs = {}
    for m in range(SEL_RATIO):
        for n in range(CMP_RATIO):
            coefs[m + n] = coefs.get(m + n, 0) + 1
    span = SEL_RATIO * (n_sel - 1) + 1

    def block(j):
        t = j * Q_BLOCK + qi
        qb = lax.dynamic_slice_in_dim(q, j * Q_BLOCK, Q_BLOCK, axis=1)
        gb = lax.dynamic_slice_in_dim(gates, j * Q_BLOCK, Q_BLOCK, axis=1)

        cdist = t[:, None] - cmp_end[None, :]
        c_bias = table_gr[rel_bucket(cdist)].transpose(2, 3, 0, 1)
        c_logits = jnp.einsum('bqgrd,bngd->bgrqn', qb, k_cmp).astype(jnp.float32) + c_bias
        p_cmp = masked_softmax(c_logits, cdist >= 0)
        o_cmp = jnp.einsum('bgrqn,bngd->bqgrd', p_cmp.astype(v_cmp.dtype), v_cmp)

        imp = jnp.pad(p_cmp.sum(axis=2), ((0, 0), (0, 0), (0, 0), (s_max, SEL_RATIO)))
        p_slc = jnp.zeros(imp.shape[:-1] + (n_sel,), jnp.float32)
        for s, cnt in coefs.items():
            p_slc = p_slc + cnt * imp[..., s_max - s: s_max - s + span: SEL_RATIO]
        cur = t // SEL_BLOCK
        valid = blk_ids[None, :] * SEL_BLOCK <= t[:, None]
        forced = ((blk_ids[None, :] == 0) | (blk_ids[None, :] == cur[:, None])
                  | (blk_ids[None, :] == cur[:, None] - 1))
        score = jnp.where(valid, jnp.where(forced, FORCED_SCORE, p_slc), -1.0)
        top_score, idx = lax.top_k(score, top)
        ksel = k_sel_blk[bidx, gidx, idx].reshape(B_, N_KV, Q_BLOCK, top * SEL_BLOCK, HEAD_DIM)
        vsel = v_sel_blk[bidx, gidx, idx].reshape(B_, N_KV, Q_BLOCK, top * SEL_BLOCK, HEAD_DIM)
        spos = (idx[..., None] * SEL_BLOCK + jnp.arange(SEL_BLOCK)).reshape(B_, N_KV, Q_BLOCK, -1)
        sdist = t[None, None, :, None] - spos
        s_valid = jnp.repeat(top_score >= 0.0, SEL_BLOCK, axis=-1) & (sdist >= 0)
        s_bias = table_g[gidx, rel_bucket(sdist)].transpose(0, 1, 4, 2, 3)
        s_logits = jnp.einsum('bqgrd,bgqkd->bgrqk', qb, ksel).astype(jnp.float32) + s_bias
        p_sel = masked_softmax(s_logits, s_valid[:, :, None])
        o_sel = jnp.einsum('bgrqk,bgqkd->bqgrd', p_sel.astype(vsel.dtype), vsel)

        kwb = lax.dynamic_slice_in_dim(k_win, j * Q_BLOCK, Q_BLOCK + WINDOW, axis=1)
        vwb = lax.dynamic_slice_in_dim(v_win, j * Q_BLOCK, Q_BLOCK + WINDOW, axis=1)
        w_mask = win_rel_mask & (kj[None, :] >= WINDOW - j * Q_BLOCK)
        w_logits = jnp.einsum('bqgrd,bkgd->bgrqk', qb, kwb).astype(jnp.float32) + win_bias
        p_win = masked_softmax(w_logits, w_mask)
        o_win = jnp.einsum('bgrqk,bkgd->bqgrd', p_win.astype(vwb.dtype), vwb)

        out = gb[..., 0:1] * o_cmp + gb[..., 1:2] * o_sel + gb[..., 2:3] * o_win
        return out.reshape(B_, Q_BLOCK, N_HEADS * HEAD_DIM).astype(q.dtype)

    out = lax.map(block, jnp.arange(n_qb))
    return out.transpose(1, 0, 2, 3).reshape(B_, L_, N_HEADS * HEAD_DIM)


def swiglu(h, w_gate, w_up, w_down):
    return (jax.nn.silu(h @ w_gate) * (h @ w_up)) @ w_down


def setup_inputs(seed: int = 0) -> dict:
    key = jax.random.key(seed)
    ks = jax.random.split(key, 32)
    f32 = jnp.float32

    def nrm(k, shape, scale):
        return jax.random.normal(k, shape, f32) * scale

    G, P, H16 = N_SSM_GROUPS, SSM_STATE, SSM_GROUP
    n_idx = jnp.arange(P, dtype=f32)
    return {
        "x": nrm(ks[0], (BATCH, SEQ, D_MODEL), 1.0),
        "rpb_table": nrm(ks[1], (N_BUCKETS, N_HEADS), 0.5),
        "attn_norm": 1.0 + nrm(ks[2], (DEPTH, D_MODEL), 0.01),
        "ffn_norm": 1.0 + nrm(ks[3], (DEPTH, D_MODEL), 0.01),
        "final_norm": 1.0 + nrm(ks[4], (D_MODEL,), 0.01),
        "w_in": nrm(ks[5], (DEPTH, D_MODEL, IN_COLS), D_MODEL ** -0.5),
        "ssm_a_re": -0.5 + nrm(ks[6], (DEPTH, G, P), 0.01),
        "ssm_a_im": math.pi * n_idx[None, None, :] + nrm(ks[7], (DEPTH, G, P), 0.01),
        "ssm_log_dt": jax.random.uniform(ks[8], (DEPTH, G), f32, math.log(DT_MIN), math.log(DT_MAX)),
        "ssm_b_re": nrm(ks[9], (DEPTH, G, P, H16), (2 * H16) ** -0.5),
        "ssm_b_im": nrm(ks[10], (DEPTH, G, P, H16), (2 * H16) ** -0.5),
        "ssm_c_re": nrm(ks[11], (DEPTH, G, H16, P), 0.5),
        "ssm_c_im": nrm(ks[12], (DEPTH, G, H16, P), 0.5),
        "ssm_d": nrm(ks[13], (DEPTH, D_SSM), 1.0),
        "w_glu": nrm(ks[14], (DEPTH, D_SSM, 2 * D_SSM), D_SSM ** -0.5),
        "b_glu": nrm(ks[15], (DEPTH, 2 * D_SSM), 0.01),
        "cmp_pos_k": nrm(ks[16], (DEPTH, CMP_BLOCK, HEAD_DIM), 0.02),
        "cmp_w1_k": nrm(ks[17], (DEPTH, CMP_BLOCK * HEAD_DIM, CMP_HIDDEN), (CMP_BLOCK * HEAD_DIM) ** -0.5),
        "cmp_w2_k": nrm(ks[18], (DEPTH, CMP_HIDDEN, HEAD_DIM), CMP_HIDDEN ** -0.5),
        "cmp_pos_v": nrm(ks[19], (DEPTH, CMP_BLOCK, HEAD_DIM), 0.02),
        "cmp_w1_v": nrm(ks[20], (DEPTH, CMP_BLOCK * HEAD_DIM, CMP_HIDDEN), (CMP_BLOCK * HEAD_DIM) ** -0.5),
        "cmp_w2_v": nrm(ks[21], (DEPTH, CMP_HIDDEN, HEAD_DIM), CMP_HIDDEN ** -0.5),
        "w_out": nrm(ks[22], (DEPTH, D_MIX, D_MODEL), D_MIX ** -0.5),
        "w_ffn_gate": nrm(ks[23], (DEPTH, D_MODEL, D_FF), D_MODEL ** -0.5),
        "w_ffn_up": nrm(ks[24], (DEPTH, D_MODEL, D_FF), D_MODEL ** -0.5),
        "w_ffn_down": nrm(ks[25], (DEPTH, D_FF, D_MODEL), D_FF ** -0.5),
    }


def reference(x, rpb_table, attn_norm, ffn_norm, final_norm, w_in,
              ssm_a_re, ssm_a_im, ssm_log_dt, ssm_b_re, ssm_b_im, ssm_c_re, ssm_c_im, ssm_d,
              w_glu, b_glu, cmp_pos_k, cmp_w1_k, cmp_w2_k, cmp_pos_v, cmp_w1_v, cmp_w2_v,
              w_out, w_ffn_gate, w_ffn_up, w_ffn_down):
    for l in range(DEPTH):
        h = rms_norm(x, attn_norm[l])
        proj = h @ w_in[l]
        u, q, kc, vc, ks_, vs_, kw, vw, gates = jnp.split(proj, IN_SPLITS, axis=-1)
        y_ssm = s5_glu(u, ssm_a_re[l], ssm_a_im[l], ssm_log_dt[l], ssm_b_re[l], ssm_b_im[l],
                       ssm_c_re[l], ssm_c_im[l], ssm_d[l], w_glu[l], b_glu[l])
        y_nsa = nsa_mixer(q, kc, vc, ks_, vs_, kw, vw, gates,
                          cmp_pos_k[l], cmp_w1_k[l], cmp_w2_k[l],
                          cmp_pos_v[l], cmp_w1_v[l], cmp_w2_v[l], rpb_table)
        x = x + jnp.concatenate([y_ssm, y_nsa], axis=-1) @ w_out[l]
        h = rms_norm(x, ffn_norm[l])
        x = x + swiglu(h, w_ffn_gate[l], w_ffn_up[l], w_ffn_down[l])
    return rms_norm(x, final_norm)
```

```python
import functools
import math

import jax
import jax.numpy as jnp
import numpy as np
from jax import lax
from jax.experimental import pallas as pl
from jax.experimental.pallas import tpu as pltpu

F32 = jnp.float32
BF16 = jnp.bfloat16

D_MODEL = 2048
DEPTH = 4
D_SSM = 1024
SSM_GROUP = 16
N_SSM_GROUPS = 64
SSM_STATE = 64
N_STATE = N_SSM_GROUPS * SSM_STATE
SSM_PACK = 16
N_PACKS = N_SSM_GROUPS // SSM_PACK
N_HEADS = 16
N_KV = 4
GQA_R = 4
HEAD_DIM = 64
CMP_BLOCK = 32
CMP_STRIDE = 16
CMP_HIDDEN = 128
SEL_BLOCK = 64
SEL_TOP = 16
WINDOW = 512
Q_TILE = 128
ROWS = GQA_R * Q_TILE
FORCED_SCORE = 1e4
N_BUCKETS = 32
RPB_MAX_EXACT = 16
RPB_MAX_DIST = 1024
D_FF = 5632
NORM_EPS = 1e-6
IN_COLS = 3632
IN_COLS_PAD = 3840
COL_U, COL_Q, COL_KC, COL_VC, COL_KS, COL_VS, COL_KW, COL_VW, COL_GATE = (
    0, 1024, 2048, 2304, 2560, 2816, 3072, 3328, 3584)
NEG = -1e30
ONEHOT_BIG = 1e30
FAR_DIST = 2048
SEL_KEY_TILE = 256
SEL_NEAR_TILES = 9
CMP_LANE = 128
VMEM_LIMIT = 56 * 1024 * 1024


def _cparams(sem):
    return pltpu.CompilerParams(dimension_semantics=sem, vmem_limit_bytes=VMEM_LIMIT)


def _rms(x, g):
    ms = jnp.mean(x * x, axis=-1, keepdims=True)
    return x * lax.rsqrt(ms + NORM_EPS) * g


def _norm_matmul_kernel(x_ref, g_ref, w_ref, o_ref, h_ref):
    @pl.when(pl.program_id(1) == 0)
    def _():
        h_ref[...] = _rms(x_ref[...], g_ref[...]).astype(BF16)

    o_ref[...] = jnp.dot(h_ref[...], w_ref[...], preferred_element_type=F32)


def norm_matmul(x, g, w, *, tm=512, tn=768):
    m, k = x.shape
    n = w.shape[1]
    return pl.pallas_call(
        _norm_matmul_kernel,
        grid=(m // tm, n // tn),
        in_specs=[pl.BlockSpec((tm, k), lambda i, j: (i, 0)),
                  pl.BlockSpec((1, k), lambda i, j: (0, 0)),
                  pl.BlockSpec((k, tn), lambda i, j: (0, j))],
        out_specs=pl.BlockSpec((tm, tn), lambda i, j: (i, j)),
        out_shape=jax.ShapeDtypeStruct((m, n), F32),
        scratch_shapes=[pltpu.VMEM((tm, k), BF16)],
        compiler_params=_cparams(("parallel", "arbitrary")),
        name="norm_matmul",
    )(x, g, w)


def _out_proj_kernel(x_ref, ys_ref, oc_ref, os_ref, ow_ref, w1_ref, w2_ref, o_ref):
    y_nsa = (oc_ref[...] + os_ref[...] + ow_ref[...]).astype(BF16)
    o_ref[...] = (x_ref[...]
                  + jnp.dot(ys_ref[...], w1_ref[...], preferred_element_type=F32)
                  + jnp.dot(y_nsa, w2_ref[...], preferred_element_type=F32))


def out_proj(x, y_ssm, o_cmp, o_sel, o_win, w_ssm, w_nsa, *, tm=512):
    m, d = x.shape
    k = y_ssm.shape[1]
    row = lambda i: (i, 0)
    full = lambda i: (0, 0)
    return pl.pallas_call(
        _out_proj_kernel,
        grid=(m // tm,),
        in_specs=[pl.BlockSpec((tm, d), row), pl.BlockSpec((tm, k), row), pl.BlockSpec((tm, k), row),
                  pl.BlockSpec((tm, k), row), pl.BlockSpec((tm, k), row),
                  pl.BlockSpec((k, d), full), pl.BlockSpec((k, d), full)],
        out_specs=pl.BlockSpec((tm, d), row),
        out_shape=jax.ShapeDtypeStruct((m, d), F32),
        compiler_params=_cparams(("parallel",)),
        name="out_proj",
    )(x, y_ssm, o_cmp, o_sel, o_win, w_ssm, w_nsa)


def _ffn_kernel(x_ref, g_ref, wg_ref, wu_ref, wd_ref, o_ref, h_ref):
    f = pl.program_id(1)

    @pl.when(f == 0)
    def _():
        x = x_ref[...]
        h_ref[...] = _rms(x, g_ref[...]).astype(BF16)
        o_ref[...] = x

    h = h_ref[...]
    gate = jnp.dot(h, wg_ref[...], preferred_element_type=F32)
    up = jnp.dot(h, wu_ref[...], preferred_element_type=F32)
    act = (jax.nn.silu(gate) * up).astype(BF16)
    o_ref[...] += jnp.dot(act, wd_ref[...], preferred_element_type=F32)


def ffn(x, g, w_gate, w_up, w_down, *, tm=512, tf=512):
    m, d = x.shape
    dff = w_gate.shape[1]
    return pl.pallas_call(
        _ffn_kernel,
        grid=(m // tm, dff // tf),
        in_specs=[pl.BlockSpec((tm, d), lambda i, f: (i, 0)),
                  pl.BlockSpec((1, d), lambda i, f: (0, 0)),
                  pl.BlockSpec((d, tf), lambda i, f: (0, f)),
                  pl.BlockSpec((d, tf), lambda i, f: (0, f)),
                  pl.BlockSpec((tf, d), lambda i, f: (f, 0))],
        out_specs=pl.BlockSpec((tm, d), lambda i, f: (i, 0)),
        out_shape=jax.ShapeDtypeStruct((m, d), F32),
        scratch_shapes=[pltpu.VMEM((tm, d), BF16)],
        compiler_params=_cparams(("parallel", "arbitrary")),
        name="ffn",
    )(x, g, w_gate, w_up, w_down)


def _final_norm_kernel(x_ref, g_ref, o_ref):
    o_ref[...] = _rms(x_ref[...], g_ref[...])


def final_rms_norm(x, g, *, tm=512):
    m, d = x.shape
    return pl.pallas_call(
        _final_norm_kernel,
        grid=(m // tm,),
        in_specs=[pl.BlockSpec((tm, d), lambda i: (i, 0)), pl.BlockSpec((1, d), lambda i: (0, 0))],
        out_specs=pl.BlockSpec((tm, d), lambda i: (i, 0)),
        out_shape=jax.ShapeDtypeStruct((m, d), F32),
        compiler_params=_cparams(("parallel",)),
        name="final_norm",
    )(x, g)


def _s5_discretize_kernel(are_ref, aim_ref, ldt_ref, bre_ref, bim_ref,
                          abre_ref, abim_ref, bbre_ref, bbim_ref):
    lam_re = jnp.minimum(are_ref[...], -1e-4)
    lam_im = aim_ref[...]
    dt = jnp.exp(ldt_ref[...])
    mag = jnp.exp(lam_re * dt)
    ab_re = mag * jnp.cos(lam_im * dt)
    ab_im = mag * jnp.sin(lam_im * dt)
    nr, ni = ab_re - 1.0, ab_im
    den = lam_re * lam_re + lam_im * lam_im
    f_re = (nr * lam_re + ni * lam_im) / den
    f_im = (ni * lam_re - nr * lam_im) / den
    br, bi = bre_ref[...], bim_ref[...]
    abre_ref[...] = ab_re
    abim_ref[...] = ab_im
    bbre_ref[...] = f_re * br - f_im * bi
    bbim_ref[...] = f_re * bi + f_im * br


def s5_discretize(a_re, a_im, log_dt, b_re_t, b_im_t):
    g, p = a_re.shape
    h = b_re_t.shape[1]
    outs = pl.pallas_call(
        _s5_discretize_kernel,
        out_shape=[jax.ShapeDtypeStruct((g, 1, p), F32), jax.ShapeDtypeStruct((g, 1, p), F32),
                   jax.ShapeDtypeStruct((g, h, p), F32), jax.ShapeDtypeStruct((g, h, p), F32)],
        name="s5_discretize",
    )(a_re.reshape(g, 1, p), a_im.reshape(g, 1, p), log_dt.reshape(g, 1, 1), b_re_t, b_im_t)
    return outs


def _s5_kernel(u_ref, bre_ref, bim_ref, cre_ref, cimn_ref, a_ref, d_ref, wglu_ref, bglu_ref,
               o_ref, xr_ref, xi_ref, st_ref):
    tc = u_ref.shape[0]
    pw = SSM_PACK * SSM_GROUP
    sw = SSM_PACK * SSM_STATE

    @pl.when(pl.program_id(1) == 0)
    def _():
        st_ref[...] = jnp.zeros_like(st_ref)

    u = u_ref[...]
    ub = u.astype(BF16)
    for k in range(N_PACKS):
        uk = ub[:, k * pw:(k + 1) * pw]
        xr_ref[:, k * sw:(k + 1) * sw] = jnp.dot(uk, bre_ref[k], preferred_element_type=F32)
        xi_ref[:, k * sw:(k + 1) * sw] = jnp.dot(uk, bim_ref[k], preferred_element_type=F32)

    def step(t, carry):
        a_re = a_ref[0:1, :]
        a_im = a_ref[1:2, :]
        s_re = st_ref[0:1, :]
        s_im = st_ref[1:2, :]
        n_re = a_re * s_re - a_im * s_im + xr_ref[pl.ds(t, 1), :]
        n_im = a_re * s_im + a_im * s_re + xi_ref[pl.ds(t, 1), :]
        xr_ref[pl.ds(t, 1), :] = n_re
        xi_ref[pl.ds(t, 1), :] = n_im
        st_ref[0:1, :] = n_re
        st_ref[1:2, :] = n_im
        return carry

    lax.fori_loop(0, tc, step, 0)

    ys = []
    for k in range(N_PACKS):
        xr = xr_ref[:, k * sw:(k + 1) * sw].astype(BF16)
        xi = xi_ref[:, k * sw:(k + 1) * sw].astype(BF16)
        ys.append(jnp.dot(xr, cre_ref[k], preferred_element_type=F32)
                  + jnp.dot(xi, cimn_ref[k], preferred_element_type=F32))
    y = jnp.concatenate(ys, axis=-1) + d_ref[...] * u
    y = jax.nn.gelu(y).astype(BF16)
    ab = jnp.dot(y, wglu_ref[...], preferred_element_type=F32) + bglu_ref[...]
    o_ref[...] = (ab[:, :D_SSM] * jax.nn.sigmoid(ab[:, D_SSM:])).astype(o_ref.dtype)


def s5_glu(proj, n_batch, bbd_re, bbd_im, cbd_re, cbd_imn, a_flat, d, w_glu, b_glu, *, tc=256):
    t = proj.shape[0]
    n_chunks = t // n_batch // tc
    c3 = lambda b, c: (0, 0, 0)
    c2 = lambda b, c: (0, 0)
    return pl.pallas_call(
        _s5_kernel,
        grid=(n_batch, n_chunks),
        in_specs=[pl.BlockSpec((tc, D_SSM), lambda b, c: (b * n_chunks + c, 0)),
                  pl.BlockSpec(bbd_re.shape, c3), pl.BlockSpec(bbd_im.shape, c3),
                  pl.BlockSpec(cbd_re.shape, c3), pl.BlockSpec(cbd_imn.shape, c3),
                  pl.BlockSpec(a_flat.shape, c2), pl.BlockSpec(d.shape, c2),
                  pl.BlockSpec(w_glu.shape, c2), pl.BlockSpec(b_glu.shape, c2)],
        out_specs=pl.BlockSpec((tc, D_SSM), lambda b, c: (b * n_chunks + c, 0)),
        out_shape=jax.ShapeDtypeStruct((t, D_SSM), BF16),
        scratch_shapes=[pltpu.VMEM((tc, N_STATE), F32), pltpu.VMEM((tc, N_STATE), F32),
                        pltpu.VMEM((2, N_STATE), F32)],
        compiler_params=_cparams(("arbitrary", "arbitrary")),
        name="s5_glu",
    )(proj, bbd_re, bbd_im, cbd_re, cbd_imn, a_flat, d, w_glu, b_glu)


def _s5_weights(a_re, a_im, log_dt, b_re, b_im, c_re, c_im):
    ab_re, ab_im, bb_re, bb_im = s5_discretize(a_re, a_im, log_dt,
                                               b_re.transpose(0, 2, 1), b_im.transpose(0, 2, 1))
    eye = jnp.eye(SSM_PACK, dtype=F32)

    def pack_b(bb):
        bb = bb.reshape(N_PACKS, SSM_PACK, SSM_GROUP, 1, SSM_STATE)
        m = bb * eye[None, :, None, :, None]
        return m.reshape(N_PACKS, SSM_PACK * SSM_GROUP, SSM_PACK * SSM_STATE).astype(BF16)

    def pack_c(c):
        c = c.reshape(N_PACKS, SSM_PACK, SSM_GROUP, SSM_STATE)
        m = c.transpose(0, 1, 3, 2)[:, :, :, None, :] * eye[None, :, None, :, None]
        return m.reshape(N_PACKS, SSM_PACK * SSM_STATE, SSM_PACK * SSM_GROUP).astype(BF16)

    a_flat = jnp.stack([ab_re.reshape(-1), ab_im.reshape(-1)])
    return pack_b(bb_re), pack_b(bb_im), pack_c(c_re), pack_c(-c_im), a_flat


def _compress_kernel(r_ref, n_ref, pos_ref, w1a_ref, w1b_ref, w2_ref, o_ref):
    r = r_ref[0]
    rows = r.shape[0]
    rowid = lax.broadcasted_iota(jnp.int32, (rows, 1), 0)
    nxt = jnp.where(rowid == rows - 1, n_ref[0, 0:1, :], pltpu.roll(r, rows - 1, 0))
    first = (r + pos_ref[0:1, :]).astype(BF16)
    second = (nxt + pos_ref[1:2, :]).astype(BF16)
    hid = (jnp.dot(first, w1a_ref[...], preferred_element_type=F32)
           + jnp.dot(second, w1b_ref[...], preferred_element_type=F32))
    hid = jax.nn.gelu(hid).astype(BF16)
    o_ref[0] = jnp.dot(hid, w2_ref[...], preferred_element_type=F32).astype(o_ref.dtype)


def compress(kc_rows, pos2, w1a, w1b, w2e):
    b, nc, w = kc_rows.shape
    tr = 128
    sub = 8
    full = lambda i, j: (0, 0)
    nxt_blk = lambda i, j: (i, jnp.minimum((j + 1) * (tr // sub), nc // sub - 1), 0)
    return pl.pallas_call(
        _compress_kernel,
        grid=(b, nc // tr),
        in_specs=[pl.BlockSpec((1, tr, w), lambda i, j: (i, j, 0)),
                  pl.BlockSpec((1, sub, w), nxt_blk),
                  pl.BlockSpec(pos2.shape, full), pl.BlockSpec(w1a.shape, full),
                  pl.BlockSpec(w1b.shape, full), pl.BlockSpec(w2e.shape, full)],
        out_specs=pl.BlockSpec((1, tr, N_KV * HEAD_DIM), lambda i, j: (i, j, 0)),
        out_shape=jax.ShapeDtypeStruct((b, nc, N_KV * HEAD_DIM), BF16),
        compiler_params=_cparams(("parallel", "parallel")),
        name="compress",
    )(kc_rows, kc_rows, pos2, w1a, w1b, w2e)


def _compress_weights(pos, w1, w2):
    eye = jnp.eye(N_KV, dtype=F32)
    w1 = w1.reshape(2, CMP_STRIDE, 1, HEAD_DIM, 1, CMP_HIDDEN)
    w1e = (w1 * eye[None, None, :, None, :, None]).reshape(
        2, CMP_STRIDE * N_KV * HEAD_DIM, N_KV * CMP_HIDDEN).astype(BF16)
    w2e = (w2[None, :, None, :] * eye[:, None, :, None]).reshape(
        N_KV * CMP_HIDDEN, N_KV * HEAD_DIM).astype(BF16)
    pos2 = jnp.broadcast_to(pos.reshape(2, CMP_STRIDE, 1, HEAD_DIM),
                            (2, CMP_STRIDE, N_KV, HEAD_DIM)).reshape(2, -1)
    return pos2, w1e[0], w1e[1], w2e


def _softmax_rows(s):
    m = jnp.max(s, axis=-1, keepdims=True)
    m = jnp.where(m < 0.5 * NEG, 0.0, m)
    e = jnp.exp(s - m)
    return e / jnp.maximum(jnp.sum(e, axis=-1, keepdims=True), 1e-30)


def _cmp_kernel(q_ref, k_ref, v_ref, tbl_ref, gate_ref, smat_ref, o_ref, sel_ref):
    a = pl.program_id(2)
    n_lane_tiles = k_ref.shape[2] // CMP_LANE
    tiles_per_super = FAR_DIST // Q_TILE
    ap = a // tiles_per_super
    q = q_ref[0, 0, 0]
    logits = lax.dot_general(q, k_ref[0, 0], (((1,), (1,)), ((), ())), preferred_element_type=F32)
    tbl = tbl_ref[0, 0]
    pieces = []
    for c in range(n_lane_tiles):
        lt = logits[:, c * CMP_LANE:(c + 1) * CMP_LANE]
        cur = lt + tbl[:, CMP_LANE:]
        prev = lt + tbl[:, :CMP_LANE]
        pieces.append(jnp.where(c == ap, cur, jnp.where(c == ap - 1, prev, jnp.where(c < ap, lt, NEG))))
    p = _softmax_rows(jnp.concatenate(pieces, axis=-1))
    gate = jax.nn.sigmoid(gate_ref[0, 0, 0])
    o_ref[0, 0, 0] = gate * jnp.dot(p.astype(BF16), v_ref[0, 0], preferred_element_type=F32)

    imp = p[0:Q_TILE] + p[Q_TILE:2 * Q_TILE] + p[2 * Q_TILE:3 * Q_TILE] + p[3 * Q_TILE:4 * Q_TILE]
    hi = imp.astype(BF16)
    r1 = imp - hi.astype(F32)
    mid = r1.astype(BF16)
    lo = (r1 - mid.astype(F32)).astype(BF16)
    smat = smat_ref[...]
    p_slc = (jnp.dot(hi, smat, preferred_element_type=F32) + jnp.dot(mid, smat, preferred_element_type=F32)
             + jnp.dot(lo, smat, preferred_element_type=F32))

    n_sel = p_slc.shape[1]
    t = a * Q_TILE + lax.broadcasted_iota(jnp.int32, (Q_TILE, n_sel), 0)
    j = lax.broadcasted_iota(jnp.int32, (Q_TILE, n_sel), 1)
    cur_blk = t // SEL_BLOCK
    valid = j * SEL_BLOCK <= t
    forced = (j == 0) | (j == cur_blk) | (j == cur_blk - 1)
    score = jnp.where(valid, jnp.where(forced, FORCED_SCORE, p_slc), -1.0)
    st = score.T
    jj = lax.broadcasted_iota(jnp.int32, st.shape, 0)
    rank = jnp.zeros(st.shape, F32)
    for jp in range(n_sel):
        row = st[jp:jp + 1, :]
        ahead = (row > st) | ((row == st) & (jj > jp))
        rank = rank + ahead.astype(F32)
    keep = (rank < SEL_TOP) & (st >= 0.0)
    sel_ref[0, 0] = (keep.astype(F32) - 1.0).T.astype(sel_ref.dtype)


def cmp_attention(qh, k_cmp, v_cmp, tbl_c, gate, smat):
    b, g, n_a, rows, dh = qh.shape
    nc = k_cmp.shape[2]
    n_sel = smat.shape[1]
    tps = FAR_DIST // Q_TILE
    return pl.pallas_call(
        _cmp_kernel,
        grid=(g, b, n_a),
        in_specs=[pl.BlockSpec((1, 1, 1, rows, dh), lambda gi, bi, ai: (bi, gi, ai, 0, 0)),
                  pl.BlockSpec((1, 1, nc, dh), lambda gi, bi, ai: (bi, gi, 0, 0)),
                  pl.BlockSpec((1, 1, nc, dh), lambda gi, bi, ai: (bi, gi, 0, 0)),
                  pl.BlockSpec((1, 1, rows, 2 * CMP_LANE), lambda gi, bi, ai: (gi, ai % tps, 0, 0)),
                  pl.BlockSpec((1, 1, 1, rows, 1), lambda gi, bi, ai: (bi, gi, ai, 0, 0)),
                  pl.BlockSpec(smat.shape, lambda gi, bi, ai: (0, 0))],
        out_specs=[pl.BlockSpec((1, 1, 1, rows, dh), lambda gi, bi, ai: (bi, gi, ai, 0, 0)),
                   pl.BlockSpec((1, 1, Q_TILE, n_sel), lambda gi, bi, ai: (bi, gi, ai, 0))],
        out_shape=[jax.ShapeDtypeStruct((b, g, n_a, rows, dh), F32),
                   jax.ShapeDtypeStruct((b, g, n_a * Q_TILE, n_sel), BF16)],
        compiler_params=_cparams(("parallel", "parallel", "parallel")),
        name="cmp_attention",
    )(qh, k_cmp, v_cmp, tbl_c, gate, smat)


def _sel_kernel(q_ref, selm_ref, ka_ref, v_ref, tbl_ref, gate_ref, o_ref, qa_ref, m_ref, l_ref, acc_ref):
    a = pl.program_id(2)
    n_sel = selm_ref.shape[3]
    tk = SEL_KEY_TILE
    selm = selm_ref[0, 0]
    for r in range(GQA_R):
        qa_ref[r * Q_TILE:(r + 1) * Q_TILE, 0:n_sel] = selm
    qa_ref[:, n_sel:] = q_ref[0, 0, 0]
    m_ref[...] = jnp.full_like(m_ref, NEG)
    l_ref[...] = jnp.zeros_like(l_ref)
    acc_ref[...] = jnp.zeros_like(acc_ref)
    n_tiles = a // (tk // Q_TILE) + 1
    first_near = jnp.maximum(0, (a - (SEL_NEAR_TILES - 2)) // 2)

    def tile(c, bias):
        off = pl.multiple_of(c * tk, tk)
        s = lax.dot_general(qa_ref[...], ka_ref[0, 0, pl.ds(off, tk), :], (((1,), (1,)), ((), ())),
                            preferred_element_type=F32)
        if bias is not None:
            s = s + bias
        m_old = m_ref[...]
        m_new = jnp.maximum(m_old, jnp.max(s, axis=-1, keepdims=True))
        alpha = jnp.exp(m_old - m_new)
        p = jnp.exp(s - m_new)
        l_ref[...] = alpha * l_ref[...] + jnp.sum(p, axis=-1, keepdims=True)
        acc_ref[...] = alpha * acc_ref[...] + jnp.dot(p.astype(BF16), v_ref[0, 0, pl.ds(off, tk), :],
                                                      preferred_element_type=F32)
        m_ref[...] = m_new

    def far_body(c, carry):
        tile(c, None)
        return carry

    def near_body(c, carry):
        tile(c, tbl_ref[0, a - 2 * c])
        return carry

    lax.fori_loop(0, first_near, far_body, 0)
    lax.fori_loop(first_near, n_tiles, near_body, 0)
    gate = jax.nn.sigmoid(gate_ref[0, 0, 0])
    o_ref[0, 0, 0] = acc_ref[...] * (gate / l_ref[...])


def sel_attention(qh, selm, ka, vs, tbl_s, gate):
    b, g, n_a, rows, dh = qh.shape
    seq = ka.shape[2]
    n_sel = selm.shape[3]
    return pl.pallas_call(
        _sel_kernel,
        grid=(g, b, n_a),
        in_specs=[pl.BlockSpec((1, 1, 1, rows, dh), lambda gi, bi, ai: (bi, gi, ai, 0, 0)),
                  pl.BlockSpec((1, 1, Q_TILE, n_sel), lambda gi, bi, ai: (bi, gi, ai, 0)),
                  pl.BlockSpec((1, 1, seq, n_sel + dh), lambda gi, bi, ai: (bi, gi, 0, 0)),
                  pl.BlockSpec((1, 1, seq, dh), lambda gi, bi, ai: (bi, gi, 0, 0)),
                  pl.BlockSpec((1, SEL_NEAR_TILES, rows, SEL_KEY_TILE), lambda gi, bi, ai: (gi, 0, 0, 0)),
                  pl.BlockSpec((1, 1, 1, rows, 1), lambda gi, bi, ai: (bi, gi, ai, 0, 0))],
        out_specs=pl.BlockSpec((1, 1, 1, rows, dh), lambda gi, bi, ai: (bi, gi, ai, 0, 0)),
        out_shape=jax.ShapeDtypeStruct((b, g, n_a, rows, dh), F32),
        scratch_shapes=[pltpu.VMEM((rows, n_sel + dh), BF16), pltpu.VMEM((rows, 1), F32),
                        pltpu.VMEM((rows, 1), F32), pltpu.VMEM((rows, dh), F32)],
        compiler_params=_cparams(("parallel", "parallel", "arbitrary")),
        name="sel_attention",
    )(qh, selm, ka, vs, tbl_s, gate)


def _win_kernel(q_ref, k_ref, v_ref, tbl_ref, gate_ref, o_ref):
    a = pl.program_id(2)
    span = WINDOW + Q_TILE
    off = pl.multiple_of(a * Q_TILE, Q_TILE)
    k = k_ref[0, 0, pl.ds(off, span), :]
    v = v_ref[0, 0, pl.ds(off, span), :]
    s = lax.dot_general(q_ref[0, 0, 0], k, (((1,), (1,)), ((), ())), preferred_element_type=F32)
    s = s + tbl_ref[0]
    kj = lax.broadcasted_iota(jnp.int32, s.shape, 1)
    s = jnp.where(kj >= WINDOW - a * Q_TILE, s, NEG)
    p = _softmax_rows(s)
    gate = jax.nn.sigmoid(gate_ref[0, 0, 0])
    o_ref[0, 0, 0] = gate * jnp.dot(p.astype(BF16), v, preferred_element_type=F32)


def win_attention(qh, kw_pad, vw_pad, tbl_w, gate):
    b, g, n_a, rows, dh = qh.shape
    seq_pad = kw_pad.shape[2]
    return pl.pallas_call(
        _win_kernel,
        grid=(g, b, n_a),
        in_specs=[pl.BlockSpec((1, 1, 1, rows, dh), lambda gi, bi, ai: (bi, gi, ai, 0, 0)),
                  pl.BlockSpec((1, 1, seq_pad, dh), lambda gi, bi, ai: (bi, gi, 0, 0)),
                  pl.BlockSpec((1, 1, seq_pad, dh), lambda gi, bi, ai: (bi, gi, 0, 0)),
                  pl.BlockSpec((1, rows, WINDOW + Q_TILE), lambda gi, bi, ai: (gi, 0, 0)),
                  pl.BlockSpec((1, 1, 1, rows, 1), lambda gi, bi, ai: (bi, gi, ai, 0, 0))],
        out_specs=pl.BlockSpec((1, 1, 1, rows, dh), lambda gi, bi, ai: (bi, gi, ai, 0, 0)),
        out_shape=jax.ShapeDtypeStruct((b, g, n_a, rows, dh), F32),
        compiler_params=_cparams(("parallel", "parallel", "parallel")),
        name="win_attention",
    )(qh, kw_pad, vw_pad, tbl_w, gate)


def _rel_bucket(dist):
    n = jnp.maximum(dist, 0)
    nf = jnp.maximum(n, 1).astype(F32)
    large = RPB_MAX_EXACT + (jnp.log(nf / RPB_MAX_EXACT) / math.log(RPB_MAX_DIST / RPB_MAX_EXACT)
                             * (N_BUCKETS - RPB_MAX_EXACT)).astype(jnp.int32)
    large = jnp.minimum(large, N_BUCKETS - 1)
    return jnp.where(n < RPB_MAX_EXACT, n, large)


def _bias_tables(rpb_table):
    tab = rpb_table.astype(F32)
    by_dist = tab[_rel_bucket(jnp.arange(FAR_DIST))].T.reshape(N_KV, GQA_R, FAR_DIST)
    rel = by_dist - tab[N_BUCKETS - 1].reshape(N_KV, GQA_R, 1)

    def expand(src, dist, ok):
        vals = jnp.where(ok, src[:, :, jnp.clip(dist, 0, FAR_DIST - 1)], NEG)
        nd = vals.ndim
        vals = jnp.moveaxis(vals, 1, nd - 3)
        return vals.reshape(vals.shape[:nd - 3] + (ROWS, vals.shape[-1]))

    i = jnp.arange(Q_TILE)
    am = jnp.arange(FAR_DIST // Q_TILE)
    m = jnp.arange(2 * CMP_LANE)
    d_c = (Q_TILE * am[:, None, None] + i[None, :, None]
           - CMP_STRIDE * (m[None, None, :] - CMP_LANE) - (CMP_BLOCK - 1))
    tbl_c = expand(rel, d_c, d_c >= 0)
    kk = jnp.arange(SEL_NEAR_TILES)
    jj = jnp.arange(SEL_KEY_TILE)
    d_s = Q_TILE * kk[:, None, None] + i[None, :, None] - jj[None, None, :]
    tbl_s = expand(rel, d_s, d_s >= 0)
    kj = jnp.arange(WINDOW + Q_TILE)
    d_w = i[:, None] + WINDOW - kj[None, :]
    tbl_w = expand(by_dist, d_w, (d_w >= 0) & (d_w < WINDOW))
    return tbl_c, tbl_s, tbl_w


def _slc_matrix(n_cmp_pad, n_sel):
    sel_ratio, cmp_ratio = SEL_BLOCK // CMP_STRIDE, CMP_BLOCK // CMP_STRIDE
    mat = np.zeros((n_cmp_pad, n_sel), np.float32)
    for j in range(n_sel):
        for mm in range(sel_ratio):
            for nn in range(cmp_ratio):
                idx = sel_ratio * j - mm - nn
                if 0 <= idx < n_cmp_pad - 1:
                    mat[idx, j] += 1.0
    return jnp.asarray(mat, BF16)


def _to_heads(cols, n_batch, seq):
    return cols.reshape(n_batch, seq, N_KV, HEAD_DIM).transpose(0, 2, 1, 3)


def _to_tiles(cols, n_batch, seq, width):
    c = cols.reshape(n_batch, seq // Q_TILE, Q_TILE, N_KV, GQA_R, width)
    return c.transpose(0, 3, 1, 4, 2, 5).reshape(n_batch, N_KV, seq // Q_TILE, ROWS, width)


def _from_tiles(o, n_batch, seq):
    o = o.reshape(n_batch, N_KV, seq // Q_TILE, GQA_R, Q_TILE, HEAD_DIM)
    return o.transpose(0, 2, 4, 1, 3, 5).reshape(n_batch * seq, N_HEADS * HEAD_DIM)


def nsa_branches(proj, n_batch, seq, cmp_k_w, cmp_v_w, tables, smat, onehot):
    tbl_c, tbl_s, tbl_w = tables
    kv = N_KV * HEAD_DIM
    qh = _to_tiles((proj[:, COL_Q:COL_Q + N_HEADS * HEAD_DIM] * (HEAD_DIM ** -0.5)).astype(BF16),
                   n_batch, seq, HEAD_DIM)
    gates = proj[:, COL_GATE:COL_GATE + 3 * N_HEADS].reshape(-1, N_HEADS, 3)
    gate = [_to_tiles(gates[:, :, c], n_batch, seq, 1) for c in range(3)]

    rows16 = lambda col: proj[:, col:col + kv].reshape(n_batch, seq // CMP_STRIDE, CMP_STRIDE * kv)
    k_cmp = compress(rows16(COL_KC), *cmp_k_w)
    v_cmp = compress(rows16(COL_VC), *cmp_v_w)
    heads = lambda c: c.reshape(n_batch, -1, N_KV, HEAD_DIM).transpose(0, 2, 1, 3)
    o_cmp, selm = cmp_attention(qh, heads(k_cmp), heads(v_cmp), tbl_c, gate[0], smat)

    ks = _to_heads(proj[:, COL_KS:COL_KS + kv], n_batch, seq).astype(BF16)
    vs = _to_heads(proj[:, COL_VS:COL_VS + kv], n_batch, seq).astype(BF16)
    ka = jnp.concatenate([jnp.broadcast_to(onehot, (n_batch, N_KV) + onehot.shape), ks], axis=-1)
    o_sel = sel_attention(qh, selm, ka, vs, tbl_s, gate[1])

    pad = ((0, 0), (0, 0), (WINDOW, 0), (0, 0))
    kw = jnp.pad(_to_heads(proj[:, COL_KW:COL_KW + kv], n_batch, seq).astype(BF16), pad)
    vw = jnp.pad(_to_heads(proj[:, COL_VW:COL_VW + kv], n_batch, seq).astype(BF16), pad)
    o_win = win_attention(qh, kw, vw, tbl_w, gate[2])
    return tuple(_from_tiles(o, n_batch, seq) for o in (o_cmp, o_sel, o_win))


def kernel(x, rpb_table, attn_norm, ffn_norm, final_norm, w_in, ssm_a_re, ssm_a_im, ssm_log_dt,
           ssm_b_re, ssm_b_im, ssm_c_re, ssm_c_im, ssm_d, w_glu, b_glu, cmp_pos_k, cmp_w1_k, cmp_w2_k,
           cmp_pos_v, cmp_w1_v, cmp_w2_v, w_out, w_ffn_gate, w_ffn_up, w_ffn_down):
    n_batch, seq, d_model = x.shape
    depth = w_in.shape[0]
    n_sel = seq // SEL_BLOCK
    tables = _bias_tables(rpb_table)
    smat = _slc_matrix(seq // CMP_STRIDE, n_sel)
    blk = jnp.arange(seq) // SEL_BLOCK
    onehot = jnp.where(blk[:, None] == jnp.arange(n_sel)[None, :], ONEHOT_BIG, 0.0).astype(BF16)

    xf = x.reshape(n_batch * seq, d_model)
    for l in range(depth):
        w_in_l = jnp.pad(w_in[l], ((0, 0), (0, IN_COLS_PAD - IN_COLS))).astype(BF16)
        proj = norm_matmul(xf, attn_norm[l].reshape(1, -1), w_in_l)
        s5w = _s5_weights(ssm_a_re[l], ssm_a_im[l], ssm_log_dt[l], ssm_b_re[l], ssm_b_im[l],
                          ssm_c_re[l], ssm_c_im[l])
        y_ssm = s5_glu(proj, n_batch, *s5w, ssm_d[l].reshape(1, -1), w_glu[l].astype(BF16),
                       b_glu[l].reshape(1, -1))
        o_cmp, o_sel, o_win = nsa_branches(
            proj, n_batch, seq,
            _compress_weights(cmp_pos_k[l], cmp_w1_k[l], cmp_w2_k[l]),
            _compress_weights(cmp_pos_v[l], cmp_w1_v[l], cmp_w2_v[l]),
            tables, smat, onehot)
        w_out_l = w_out[l].astype(BF16)
        xf = out_proj(xf, y_ssm, o_cmp, o_sel, o_win, w_out_l[:D_SSM], w_out_l[D_SSM:])
        xf = ffn(xf, ffn_norm[l].reshape(1, -1), w_ffn_gate[l].astype(BF16),
                 w_ffn_up[l].astype(BF16), w_ffn_down[l].astype(BF16))
    out = final_rms_norm(xf, final_norm.reshape(1, -1))
    return out.reshape(n_batch, seq, d_model)
```

```python
import math

import jax
import jax.numpy as jnp
import numpy as np
from jax import lax
from jax.experimental import pallas as pl
from jax.experimental.pallas import tpu as pltpu

F32 = jnp.float32
BF16 = jnp.bfloat16

D_MODEL = 2048
DEPTH = 4
D_SSM = 1024
SSM_GROUP = 16
N_SSM_GROUPS = 64
SSM_STATE = 64
N_STATE = N_SSM_GROUPS * SSM_STATE
SSM_PACK = 16
N_PACKS = N_SSM_GROUPS // SSM_PACK
N_HEADS = 16
N_KV = 4
GQA_R = 4
HEAD_DIM = 64
CMP_BLOCK = 32
CMP_STRIDE = 16
CMP_HIDDEN = 128
SEL_BLOCK = 64
SEL_TOP = 16
WINDOW = 512
Q_TILE = 128
ROWS = GQA_R * Q_TILE
FORCED_SCORE = 1e4
N_BUCKETS = 32
RPB_MAX_EXACT = 16
RPB_MAX_DIST = 1024
D_FF = 5632
NORM_EPS = 1e-6
IN_COLS = 3632
IN_COLS_PAD = 3840
COL_U, COL_Q, COL_KC, COL_VC, COL_KS, COL_VS, COL_KW, COL_VW, COL_GATE = (
    0, 1024, 2048, 2304, 2560, 2816, 3072, 3328, 3584)
NEG = -1e30
ONEHOT_BIG = 1e30
FAR_DIST = 2048
SEL_KEY_TILE = 256
SEL_NEAR_TILES = 11
CMP_LANE = 128
V_ROWS = 80
VMEM_LIMIT = 56 * 1024 * 1024


def _cparams(sem):
    return pltpu.CompilerParams(dimension_semantics=sem, vmem_limit_bytes=VMEM_LIMIT)


def _rms(x, g):
    ms = jnp.mean(x * x, axis=-1, keepdims=True)
    return x * lax.rsqrt(ms + NORM_EPS) * g


def _norm_matmul_kernel(x_ref, g_ref, w_ref, o_ref, h_ref):
    @pl.when(pl.program_id(1) == 0)
    def _():
        h_ref[...] = _rms(x_ref[...], g_ref[...]).astype(BF16)

    o_ref[...] = jnp.dot(h_ref[...], w_ref[...], preferred_element_type=F32)


def norm_matmul(x, g, w, *, tm=512, tn=768):
    m, k = x.shape
    n = w.shape[1]
    return pl.pallas_call(
        _norm_matmul_kernel,
        grid=(m // tm, n // tn),
        in_specs=[pl.BlockSpec((tm, k), lambda i, j: (i, 0)),
                  pl.BlockSpec((1, k), lambda i, j: (0, 0)),
                  pl.BlockSpec((k, tn), lambda i, j: (0, j))],
        out_specs=pl.BlockSpec((tm, tn), lambda i, j: (i, j)),
        out_shape=jax.ShapeDtypeStruct((m, n), F32),
        scratch_shapes=[pltpu.VMEM((tm, k), BF16)],
        compiler_params=_cparams(("parallel", "arbitrary")),
        name="norm_matmul",
    )(x, g, w)


def _out_proj_kernel(x_ref, ys_ref, oc_ref, os_ref, ow_ref, w1_ref, w2_ref, o_ref):
    y_nsa = (oc_ref[...] + os_ref[...] + ow_ref[...]).astype(BF16)
    o_ref[...] = (x_ref[...]
                  + jnp.dot(ys_ref[...], w1_ref[...], preferred_element_type=F32)
                  + jnp.dot(y_nsa, w2_ref[...], preferred_element_type=F32))


def out_proj(x, y_ssm, o_cmp, o_sel, o_win, w_ssm, w_nsa, *, tm=512):
    m, d = x.shape
    k = y_ssm.shape[1]
    row = lambda i: (i, 0)
    full = lambda i: (0, 0)
    return pl.pallas_call(
        _out_proj_kernel,
        grid=(m // tm,),
        in_specs=[pl.BlockSpec((tm, d), row), pl.BlockSpec((tm, k), row), pl.BlockSpec((tm, k), row),
                  pl.BlockSpec((tm, k), row), pl.BlockSpec((tm, k), row),
                  pl.BlockSpec((k, d), full), pl.BlockSpec((k, d), full)],
        out_specs=pl.BlockSpec((tm, d), row),
        out_shape=jax.ShapeDtypeStruct((m, d), F32),
        compiler_params=_cparams(("parallel",)),
        name="out_proj",
    )(x, y_ssm, o_cmp, o_sel, o_win, w_ssm, w_nsa)


def _ffn_kernel(x_ref, g_ref, wg_ref, wu_ref, wd_ref, o_ref, h_ref):
    f = pl.program_id(1)

    @pl.when(f == 0)
    def _():
        x = x_ref[...]
        h_ref[...] = _rms(x, g_ref[...]).astype(BF16)
        o_ref[...] = x

    h = h_ref[...]
    gate = jnp.dot(h, wg_ref[...], preferred_element_type=F32)
    up = jnp.dot(h, wu_ref[...], preferred_element_type=F32)
    act = (jax.nn.silu(gate) * up).astype(BF16)
    o_ref[...] += jnp.dot(act, wd_ref[...], preferred_element_type=F32)


def ffn(x, g, w_gate, w_up, w_down, *, tm=512, tf=512):
    m, d = x.shape
    dff = w_gate.shape[1]
    return pl.pallas_call(
        _ffn_kernel,
        grid=(m // tm, dff // tf),
        in_specs=[pl.BlockSpec((tm, d), lambda i, f: (i, 0)),
                  pl.BlockSpec((1, d), lambda i, f: (0, 0)),
                  pl.BlockSpec((d, tf), lambda i, f: (0, f)),
                  pl.BlockSpec((d, tf), lambda i, f: (0, f)),
                  pl.BlockSpec((tf, d), lambda i, f: (f, 0))],
        out_specs=pl.BlockSpec((tm, d), lambda i, f: (i, 0)),
        out_shape=jax.ShapeDtypeStruct((m, d), F32),
        scratch_shapes=[pltpu.VMEM((tm, d), BF16)],
        compiler_params=_cparams(("parallel", "arbitrary")),
        name="ffn",
    )(x, g, w_gate, w_up, w_down)


def _final_norm_kernel(x_ref, g_ref, o_ref):
    o_ref[...] = _rms(x_ref[...], g_ref[...])


def final_rms_norm(x, g, *, tm=512):
    m, d = x.shape
    return pl.pallas_call(
        _final_norm_kernel,
        grid=(m // tm,),
        in_specs=[pl.BlockSpec((tm, d), lambda i: (i, 0)), pl.BlockSpec((1, d), lambda i: (0, 0))],
        out_specs=pl.BlockSpec((tm, d), lambda i: (i, 0)),
        out_shape=jax.ShapeDtypeStruct((m, d), F32),
        compiler_params=_cparams(("parallel",)),
        name="final_norm",
    )(x, g)


def _s5_discretize_kernel(are_ref, aim_ref, ldt_ref, bre_ref, bim_ref,
                          abre_ref, abim_ref, bbre_ref, bbim_ref):
    lam_re = jnp.minimum(are_ref[...], -1e-4)
    lam_im = aim_ref[...]
    dt = jnp.exp(ldt_ref[...])
    mag = jnp.exp(lam_re * dt)
    ab_re = mag * jnp.cos(lam_im * dt)
    ab_im = mag * jnp.sin(lam_im * dt)
    nr, ni = ab_re - 1.0, ab_im
    den = lam_re * lam_re + lam_im * lam_im
    f_re = (nr * lam_re + ni * lam_im) / den
    f_im = (ni * lam_re - nr * lam_im) / den
    br, bi = bre_ref[...], bim_ref[...]
    abre_ref[...] = ab_re
    abim_ref[...] = ab_im
    bbre_ref[...] = f_re * br - f_im * bi
    bbim_ref[...] = f_re * bi + f_im * br


def s5_discretize(a_re, a_im, log_dt, b_re_t, b_im_t):
    g, p = a_re.shape
    h = b_re_t.shape[1]
    outs = pl.pallas_call(
        _s5_discretize_kernel,
        out_shape=[jax.ShapeDtypeStruct((g, 1, p), F32), jax.ShapeDtypeStruct((g, 1, p), F32),
                   jax.ShapeDtypeStruct((g, h, p), F32), jax.ShapeDtypeStruct((g, h, p), F32)],
        name="s5_discretize",
    )(a_re.reshape(g, 1, p), a_im.reshape(g, 1, p), log_dt.reshape(g, 1, 1), b_re_t, b_im_t)
    return outs


def _s5_kernel(u_ref, bre_ref, bim_ref, cre_ref, cimn_ref, a_ref, d_ref, wglu_ref, bglu_ref,
               o_ref, xr_ref, xi_ref, st_ref):
    tc = u_ref.shape[0]
    pw = SSM_PACK * SSM_GROUP
    sw = SSM_PACK * SSM_STATE

    @pl.when(pl.program_id(1) == 0)
    def _():
        st_ref[...] = jnp.zeros_like(st_ref)

    u = u_ref[...]
    ub = u.astype(BF16)
    for k in range(N_PACKS):
        uk = ub[:, k * pw:(k + 1) * pw]
        xr_ref[:, k * sw:(k + 1) * sw] = jnp.dot(uk, bre_ref[k], preferred_element_type=F32)
        xi_ref[:, k * sw:(k + 1) * sw] = jnp.dot(uk, bim_ref[k], preferred_element_type=F32)

    def step(t, carry):
        a_re = a_ref[0:1, :]
        a_im = a_ref[1:2, :]
        s_re = st_ref[0:1, :]
        s_im = st_ref[1:2, :]
        n_re = a_re * s_re - a_im * s_im + xr_ref[pl.ds(t, 1), :]
        n_im = a_re * s_im + a_im * s_re + xi_ref[pl.ds(t, 1), :]
        xr_ref[pl.ds(t, 1), :] = n_re
        xi_ref[pl.ds(t, 1), :] = n_im
        st_ref[0:1, :] = n_re
        st_ref[1:2, :] = n_im
        return carry

    lax.fori_loop(0, tc, step, 0)

    ys = []
    for k in range(N_PACKS):
        xr = xr_ref[:, k * sw:(k + 1) * sw].astype(BF16)
        xi = xi_ref[:, k * sw:(k + 1) * sw].astype(BF16)
        ys.append(jnp.dot(xr, cre_ref[k], preferred_element_type=F32)
                  + jnp.dot(xi, cimn_ref[k], preferred_element_type=F32))
    y = jnp.concatenate(ys, axis=-1) + d_ref[...] * u
    y = jax.nn.gelu(y).astype(BF16)
    ab = jnp.dot(y, wglu_ref[...], preferred_element_type=F32) + bglu_ref[...]
    o_ref[...] = (ab[:, :D_SSM] * jax.nn.sigmoid(ab[:, D_SSM:])).astype(o_ref.dtype)


def s5_glu(proj, n_batch, bbd_re, bbd_im, cbd_re, cbd_imn, a_flat, d, w_glu, b_glu, *, tc=256):
    t = proj.shape[0]
    n_chunks = t // n_batch // tc
    c3 = lambda b, c: (0, 0, 0)
    c2 = lambda b, c: (0, 0)
    return pl.pallas_call(
        _s5_kernel,
        grid=(n_batch, n_chunks),
        in_specs=[pl.BlockSpec((tc, D_SSM), lambda b, c: (b * n_chunks + c, 0)),
                  pl.BlockSpec(bbd_re.shape, c3), pl.BlockSpec(bbd_im.shape, c3),
                  pl.BlockSpec(cbd_re.shape, c3), pl.BlockSpec(cbd_imn.shape, c3),
                  pl.BlockSpec(a_flat.shape, c2), pl.BlockSpec(d.shape, c2),
                  pl.BlockSpec(w_glu.shape, c2), pl.BlockSpec(b_glu.shape, c2)],
        out_specs=pl.BlockSpec((tc, D_SSM), lambda b, c: (b * n_chunks + c, 0)),
        out_shape=jax.ShapeDtypeStruct((t, D_SSM), BF16),
        scratch_shapes=[pltpu.VMEM((tc, N_STATE), F32), pltpu.VMEM((tc, N_STATE), F32),
                        pltpu.VMEM((2, N_STATE), F32)],
        compiler_params=_cparams(("arbitrary", "arbitrary")),
        name="s5_glu",
    )(proj, bbd_re, bbd_im, cbd_re, cbd_imn, a_flat, d, w_glu, b_glu)


def _s5_weights(a_re, a_im, log_dt, b_re, b_im, c_re, c_im):
    ab_re, ab_im, bb_re, bb_im = s5_discretize(a_re, a_im, log_dt,
                                               b_re.transpose(0, 2, 1), b_im.transpose(0, 2, 1))
    eye = jnp.eye(SSM_PACK, dtype=F32)

    def pack_b(bb):
        bb = bb.reshape(N_PACKS, SSM_PACK, SSM_GROUP, 1, SSM_STATE)
        m = bb * eye[None, :, None, :, None]
        return m.reshape(N_PACKS, SSM_PACK * SSM_GROUP, SSM_PACK * SSM_STATE).astype(BF16)

    def pack_c(c):
        c = c.reshape(N_PACKS, SSM_PACK, SSM_GROUP, SSM_STATE)
        m = c.transpose(0, 1, 3, 2)[:, :, :, None, :] * eye[None, :, None, :, None]
        return m.reshape(N_PACKS, SSM_PACK * SSM_STATE, SSM_PACK * SSM_GROUP).astype(BF16)

    a_flat = jnp.stack([ab_re.reshape(-1), ab_im.reshape(-1)])
    return pack_b(bb_re), pack_b(bb_im), pack_c(c_re), pack_c(-c_im), a_flat


def _compress_kernel(r_ref, n_ref, pos_ref, w1a_ref, w1b_ref, w2_ref, o_ref):
    r = r_ref[0]
    rows = r.shape[0]
    rowid = lax.broadcasted_iota(jnp.int32, (rows, 1), 0)
    nxt = jnp.where(rowid == rows - 1, n_ref[0, 0:1, :], pltpu.roll(r, rows - 1, 0))
    first = (r + pos_ref[0:1, :]).astype(BF16)
    second = (nxt + pos_ref[1:2, :]).astype(BF16)
    hid = (jnp.dot(first, w1a_ref[...], preferred_element_type=F32)
           + jnp.dot(second, w1b_ref[...], preferred_element_type=F32))
    hid = jax.nn.gelu(hid).astype(BF16)
    o_ref[0] = jnp.dot(hid, w2_ref[...], preferred_element_type=F32).astype(o_ref.dtype)


def compress(kc_rows, pos2, w1a, w1b, w2e):
    b, nc, w = kc_rows.shape
    tr = 128
    sub = 8
    full = lambda i, j: (0, 0)
    nxt_blk = lambda i, j: (i, jnp.minimum((j + 1) * (tr // sub), nc // sub - 1), 0)
    return pl.pallas_call(
        _compress_kernel,
        grid=(b, nc // tr),
        in_specs=[pl.BlockSpec((1, tr, w), lambda i, j: (i, j, 0)),
                  pl.BlockSpec((1, sub, w), nxt_blk),
                  pl.BlockSpec(pos2.shape, full), pl.BlockSpec(w1a.shape, full),
                  pl.BlockSpec(w1b.shape, full), pl.BlockSpec(w2e.shape, full)],
        out_specs=pl.BlockSpec((1, tr, N_KV * HEAD_DIM), lambda i, j: (i, j, 0)),
        out_shape=jax.ShapeDtypeStruct((b, nc, N_KV * HEAD_DIM), BF16),
        compiler_params=_cparams(("parallel", "parallel")),
        name="compress",
    )(kc_rows, kc_rows, pos2, w1a, w1b, w2e)


def _compress_weights(pos, w1, w2):
    eye = jnp.eye(N_KV, dtype=F32)
    w1 = w1.reshape(2, CMP_STRIDE, 1, HEAD_DIM, 1, CMP_HIDDEN)
    w1e = (w1 * eye[None, None, :, None, :, None]).reshape(
        2, CMP_STRIDE * N_KV * HEAD_DIM, N_KV * CMP_HIDDEN).astype(BF16)
    w2e = (w2[None, :, None, :] * eye[:, None, :, None]).reshape(
        N_KV * CMP_HIDDEN, N_KV * HEAD_DIM).astype(BF16)
    pos2 = jnp.broadcast_to(pos.reshape(2, CMP_STRIDE, 1, HEAD_DIM),
                            (2, CMP_STRIDE, N_KV, HEAD_DIM)).reshape(2, -1)
    return pos2, w1e[0], w1e[1], w2e


def _softmax_cols(s):
    m = jnp.max(s, axis=0, keepdims=True)
    m = jnp.where(m < 0.5 * NEG, 0.0, m)
    e = jnp.exp(s - m)
    return e * (1.0 / jnp.maximum(jnp.sum(e, axis=0, keepdims=True), 1e-30))


def _cmp_kernel(qt_ref, k_ref, vt_ref, tbl_ref, gate_ref, smat_ref, o_ref, sel_ref, st_ref, rank_ref):
    a = pl.program_id(2)
    n_row_tiles = k_ref.shape[2] // CMP_LANE
    ap = a // (FAR_DIST // Q_TILE)
    s = jnp.dot(k_ref[0, 0], qt_ref[0, 0, 0], preferred_element_type=F32)
    tbl = tbl_ref[0, 0]
    pieces = []
    for c in range(n_row_tiles):
        st = s[c * CMP_LANE:(c + 1) * CMP_LANE]
        cur = st + tbl[CMP_LANE:]
        prev = st + tbl[:CMP_LANE]
        pieces.append(jnp.where(c == ap, cur, jnp.where(c == ap - 1, prev, jnp.where(c < ap, st, NEG))))
    p = _softmax_cols(jnp.concatenate(pieces, axis=0))
    gate = jax.nn.sigmoid(gate_ref[0, 0, 0])
    o_ref[0, 0, 0] = gate * jnp.dot(vt_ref[0, 0], p.astype(BF16), preferred_element_type=F32)

    imp = (p[:, 0:Q_TILE] + p[:, Q_TILE:2 * Q_TILE] + p[:, 2 * Q_TILE:3 * Q_TILE]
           + p[:, 3 * Q_TILE:4 * Q_TILE])
    hi = imp.astype(BF16)
    r1 = imp - hi.astype(F32)
    mid = r1.astype(BF16)
    lo = (r1 - mid.astype(F32)).astype(BF16)
    smat = smat_ref[...]
    st = (jnp.dot(smat, hi, preferred_element_type=F32) + jnp.dot(smat, mid, preferred_element_type=F32)
          + jnp.dot(smat, lo, preferred_element_type=F32))

    t = a * Q_TILE + lax.broadcasted_iota(jnp.int32, st.shape, 1)
    jj = lax.broadcasted_iota(jnp.int32, st.shape, 0)
    cur_blk = t // SEL_BLOCK
    valid = jj * SEL_BLOCK <= t
    forced = (jj == 0) | (jj == cur_blk) | (jj == cur_blk - 1)
    st = jnp.where(valid, jnp.where(forced, FORCED_SCORE, st), -1.0)
    st_ref[...] = st
    rank_ref[...] = jnp.zeros_like(rank_ref)
    sub = 8
    chunk = 16
    n_sel = st.shape[0]
    last_valid = (a * Q_TILE + Q_TILE - 1) // SEL_BLOCK
    for q0 in range(0, n_sel, chunk):
        @pl.when(q0 <= last_valid)
        def _(q0=q0):
            s_all = st_ref[...]
            parts = []
            for v in range(n_sel // sub):
                sv = s_all[v * sub:(v + 1) * sub]
                jv = jj[v * sub:(v + 1) * sub]
                cnt = rank_ref[v * sub:(v + 1) * sub, :]
                for jp in range(q0, min(q0 + chunk, n_sel)):
                    row = s_all[jp:jp + 1, :]
                    if jp < v * sub:
                        ahead = row >= sv
                    elif jp >= (v + 1) * sub:
                        ahead = row > sv
                    else:
                        ahead = (row > sv) | ((row == sv) & (jv > jp))
                    cnt = cnt + jnp.where(ahead, 1.0, 0.0)
                parts.append(cnt)
            rank_ref[...] = jnp.concatenate(parts, axis=0)

    keep = (rank_ref[...] < SEL_TOP) & (st >= 0.0)
    sel_ref[0, 0, 0] = (keep.astype(F32) - 1.0).astype(sel_ref.dtype)


def cmp_attention(qt, k_cmp, vt_cmp, tbl_c, gate, smat):
    b, g, n_a, dh, rows = qt.shape
    nc = k_cmp.shape[2]
    n_sel = smat.shape[0]
    tps = FAR_DIST // Q_TILE
    return pl.pallas_call(
        _cmp_kernel,
        grid=(g, b, n_a),
        in_specs=[pl.BlockSpec((1, 1, 1, dh, rows), lambda gi, bi, ai: (bi, gi, ai, 0, 0)),
                  pl.BlockSpec((1, 1, nc, dh), lambda gi, bi, ai: (bi, gi, 0, 0)),
                  pl.BlockSpec((1, 1, dh, nc), lambda gi, bi, ai: (bi, gi, 0, 0)),
                  pl.BlockSpec((1, 1, 2 * CMP_LANE, rows), lambda gi, bi, ai: (gi, ai % tps, 0, 0)),
                  pl.BlockSpec((1, 1, 1, 1, rows), lambda gi, bi, ai: (bi, gi, ai, 0, 0)),
                  pl.BlockSpec(smat.shape, lambda gi, bi, ai: (0, 0))],
        out_specs=[pl.BlockSpec((1, 1, 1, dh, rows), lambda gi, bi, ai: (bi, gi, ai, 0, 0)),
                   pl.BlockSpec((1, 1, 1, n_sel, Q_TILE), lambda gi, bi, ai: (bi, gi, ai, 0, 0))],
        out_shape=[jax.ShapeDtypeStruct((b, g, n_a, dh, rows), F32),
                   jax.ShapeDtypeStruct((b, g, n_a, n_sel, Q_TILE), BF16)],
        scratch_shapes=[pltpu.VMEM((n_sel, Q_TILE), F32), pltpu.VMEM((n_sel, Q_TILE), F32)],
        compiler_params=_cparams(("parallel", "parallel", "parallel")),
        name="cmp_attention",
    )(qt, k_cmp, vt_cmp, tbl_c, gate, smat)


def _sel_kernel(qt_ref, sel_ref, ka_ref, vt_ref, tbl_ref, gate_ref, o_ref, qa_ref, m_ref, acc_ref):
    a = pl.program_id(2)
    n_sel = sel_ref.shape[3]
    tk = SEL_KEY_TILE
    sel = sel_ref[0, 0, 0]
    for r in range(GQA_R):
        qa_ref[0:n_sel, r * Q_TILE:(r + 1) * Q_TILE] = sel
    qa_ref[n_sel:, :] = qt_ref[0, 0, 0]
    m_ref[...] = jnp.full_like(m_ref, NEG)
    acc_ref[...] = jnp.zeros_like(acc_ref)
    n_tiles = a // (tk // Q_TILE) + 1
    first_near = jnp.maximum(0, (a - (SEL_NEAR_TILES - 2)) // 2)
    first_near = first_near + first_near % 2
    near_pairs = (n_tiles - first_near) // 2

    def attend(tiles):
        m = m_first = m_ref[...]
        done = []
        for c, bias in tiles:
            off = pl.multiple_of(c * tk, tk)
            s = jnp.dot(ka_ref[0, 0, pl.ds(off, tk), :], qa_ref[...], preferred_element_type=F32)
            if bias is not None:
                s = s + bias
            m = jnp.maximum(m, jnp.max(s, axis=0, keepdims=True))
            done.append((off, jnp.exp(s - m).astype(BF16), m))
        acc = jnp.exp(m_first - m) * acc_ref[...]
        for off, p, m_t in done:
            pv = jnp.dot(vt_ref[0, 0, :, pl.ds(off, tk)], p, preferred_element_type=F32)
            acc = acc + (pv if m_t is m else jnp.exp(m_t - m) * pv)
        acc_ref[...] = acc
        m_ref[...] = m

    def far_body(i, carry):
        attend([(2 * i, None), (2 * i + 1, None)])
        return carry

    def near_body(i, carry):
        c = first_near + 2 * i
        attend([(c, tbl_ref[0, a - 2 * c]), (c + 1, tbl_ref[0, a - 2 * c - 2])])
        return carry

    lax.fori_loop(0, first_near // 2, far_body, 0)
    lax.fori_loop(0, near_pairs, near_body, 0)

    @pl.when(first_near + 2 * near_pairs < n_tiles)
    def _():
        c = n_tiles - 1
        attend([(c, tbl_ref[0, a - 2 * c])])

    gate = jax.nn.sigmoid(gate_ref[0, 0, 0])
    acc = acc_ref[...]
    o_ref[0, 0, 0] = acc[0:HEAD_DIM] * (gate / acc[HEAD_DIM:HEAD_DIM + 1])


def sel_attention(qt, sel, ka, vt_aug, tbl_s, gate):
    b, g, n_a, dh, rows = qt.shape
    seq = ka.shape[2]
    n_sel = sel.shape[3]
    return pl.pallas_call(
        _sel_kernel,
        grid=(g, b, n_a),
        in_specs=[pl.BlockSpec((1, 1, 1, dh, rows), lambda gi, bi, ai: (bi, gi, ai, 0, 0)),
                  pl.BlockSpec((1, 1, 1, n_sel, Q_TILE), lambda gi, bi, ai: (bi, gi, ai, 0, 0)),
                  pl.BlockSpec((1, 1, seq, n_sel + dh), lambda gi, bi, ai: (bi, gi, 0, 0)),
                  pl.BlockSpec((1, 1, V_ROWS, seq), lambda gi, bi, ai: (bi, gi, 0, 0)),
                  pl.BlockSpec((1, SEL_NEAR_TILES, SEL_KEY_TILE, rows), lambda gi, bi, ai: (gi, 0, 0, 0)),
                  pl.BlockSpec((1, 1, 1, 1, rows), lambda gi, bi, ai: (bi, gi, ai, 0, 0))],
        out_specs=pl.BlockSpec((1, 1, 1, dh, rows), lambda gi, bi, ai: (bi, gi, ai, 0, 0)),
        out_shape=jax.ShapeDtypeStruct((b, g, n_a, dh, rows), F32),
        scratch_shapes=[pltpu.VMEM((n_sel + dh, rows), BF16), pltpu.VMEM((1, rows), F32),
                        pltpu.VMEM((V_ROWS, rows), F32)],
        compiler_params=_cparams(("parallel", "parallel", "arbitrary")),
        name="sel_attention",
    )(qt, sel, ka, vt_aug, tbl_s, gate)


def _win_kernel(qt_ref, k_ref, vt_ref, tbl_ref, gate_ref, o_ref):
    a = pl.program_id(2)
    span = WINDOW + Q_TILE
    off = pl.multiple_of(a * Q_TILE, Q_TILE)
    s = jnp.dot(k_ref[0, 0, pl.ds(off, span), :], qt_ref[0, 0, 0], preferred_element_type=F32)
    s = s + tbl_ref[0]
    kj = lax.broadcasted_iota(jnp.int32, s.shape, 0)
    s = jnp.where(kj >= WINDOW - a * Q_TILE, s, NEG)
    p = _softmax_cols(s)
    gate = jax.nn.sigmoid(gate_ref[0, 0, 0])
    o_ref[0, 0, 0] = gate * jnp.dot(vt_ref[0, 0, :, pl.ds(off, span)], p.astype(BF16),
                                    preferred_element_type=F32)


def win_attention(qt, kw_pad, vwt_pad, tbl_w, gate):
    b, g, n_a, dh, rows = qt.shape
    seq_pad = kw_pad.shape[2]
    return pl.pallas_call(
        _win_kernel,
        grid=(g, b, n_a),
        in_specs=[pl.BlockSpec((1, 1, 1, dh, rows), lambda gi, bi, ai: (bi, gi, ai, 0, 0)),
                  pl.BlockSpec((1, 1, seq_pad, dh), lambda gi, bi, ai: (bi, gi, 0, 0)),
                  pl.BlockSpec((1, 1, dh, seq_pad), lambda gi, bi, ai: (bi, gi, 0, 0)),
                  pl.BlockSpec((1, WINDOW + Q_TILE, rows), lambda gi, bi, ai: (gi, 0, 0)),
                  pl.BlockSpec((1, 1, 1, 1, rows), lambda gi, bi, ai: (bi, gi, ai, 0, 0))],
        out_specs=pl.BlockSpec((1, 1, 1, dh, rows), lambda gi, bi, ai: (bi, gi, ai, 0, 0)),
        out_shape=jax.ShapeDtypeStruct((b, g, n_a, dh, rows), F32),
        compiler_params=_cparams(("parallel", "parallel", "parallel")),
        name="win_attention",
    )(qt, kw_pad, vwt_pad, tbl_w, gate)


def _rel_bucket(dist):
    n = jnp.maximum(dist, 0)
    nf = jnp.maximum(n, 1).astype(F32)
    large = RPB_MAX_EXACT + (jnp.log(nf / RPB_MAX_EXACT) / math.log(RPB_MAX_DIST / RPB_MAX_EXACT)
                             * (N_BUCKETS - RPB_MAX_EXACT)).astype(jnp.int32)
    large = jnp.minimum(large, N_BUCKETS - 1)
    return jnp.where(n < RPB_MAX_EXACT, n, large)


def _toeplitz(e, n_rows):
    w = e.shape[-1]
    lead = e.shape[:-1]
    flat = jnp.broadcast_to(e[..., None, :], lead + (n_rows, w)).reshape(lead + (n_rows * w,))
    return flat[..., :n_rows * (w - 1)].reshape(lead + (n_rows, w - 1))


def _bias_tables(rpb_table):
    tab = rpb_table.astype(F32)
    by_dist = tab[_rel_bucket(jnp.arange(FAR_DIST))].T
    rel = by_dist - tab[N_BUCKETS - 1][:, None]
    neg = lambda n: jnp.full((N_HEADS, n), NEG, F32)
    zero = lambda n: jnp.zeros((N_HEADS, n), F32)

    def to_rows(m):
        m = m.reshape((N_KV, GQA_R) + m.shape[1:])
        m = jnp.moveaxis(m, 1, -2)
        return m.reshape(m.shape[:-2] + (ROWS,))

    tk = SEL_KEY_TILE
    ext = jnp.concatenate([neg(tk), rel], axis=1)
    win = jnp.stack([ext[:, Q_TILE * k + 1:Q_TILE * k + 1 + tk + Q_TILE] for k in range(SEL_NEAR_TILES)],
                    axis=1)
    tbl_s = to_rows(_toeplitz(win, tk)[..., tk - 1:tk - 1 + Q_TILE])

    span = WINDOW + Q_TILE
    ext = jnp.concatenate([neg(Q_TILE - 1), by_dist[:, :WINDOW], neg(Q_TILE + 1)], axis=1)
    tbl_w = to_rows(_toeplitz(ext, span)[..., span - 1:span - 1 + Q_TILE])

    n_m = 2 * CMP_LANE
    n_tau = FAR_DIST // CMP_STRIDE
    base = CMP_STRIDE * (n_m - 1) - (FAR_DIST - CMP_BLOCK + 1)
    length = CMP_STRIDE * (n_m + n_tau)
    ext = jnp.concatenate([neg(base), rel, zero(length - base - FAR_DIST)], axis=1)
    ev = ext.reshape(N_HEADS, n_m + n_tau, CMP_STRIDE).transpose(0, 2, 1)
    tz = _toeplitz(ev, n_m)[..., n_m - 1:n_m - 1 + n_tau]
    tz = tz.transpose(0, 2, 3, 1).reshape(N_HEADS, n_m, FAR_DIST // Q_TILE, Q_TILE)
    tbl_c = to_rows(tz.transpose(0, 2, 1, 3))
    return tbl_c, tbl_s, tbl_w


def _slc_matrix(n_cmp_pad, n_sel):
    sel_ratio, cmp_ratio = SEL_BLOCK // CMP_STRIDE, CMP_BLOCK // CMP_STRIDE
    mat = np.zeros((n_sel, n_cmp_pad), np.float32)
    for j in range(n_sel):
        for mm in range(sel_ratio):
            for nn in range(cmp_ratio):
                idx = sel_ratio * j - mm - nn
                if 0 <= idx < n_cmp_pad - 1:
                    mat[j, idx] += 1.0
    return jnp.asarray(mat, BF16)


def _to_heads(cols, n_batch, seq):
    return cols.reshape(n_batch, seq, N_KV, HEAD_DIM).transpose(0, 2, 1, 3)


def _to_heads_t(cols, n_batch, seq):
    return cols.reshape(n_batch, seq, N_KV, HEAD_DIM).transpose(0, 2, 3, 1)


def _to_tiles_t(cols, n_batch, seq, width):
    c = cols.reshape(n_batch, seq // Q_TILE, Q_TILE, N_KV, GQA_R, width)
    return c.transpose(0, 3, 1, 5, 4, 2).reshape(n_batch, N_KV, seq // Q_TILE, width, ROWS)


def _from_tiles_t(o, n_batch, seq):
    o = o.reshape(n_batch, N_KV, seq // Q_TILE, HEAD_DIM, GQA_R, Q_TILE)
    return o.transpose(0, 2, 5, 1, 4, 3).reshape(n_batch * seq, N_HEADS * HEAD_DIM)


def nsa_branches(proj, n_batch, seq, cmp_k_w, cmp_v_w, tables, smat, onehot):
    tbl_c, tbl_s, tbl_w = tables
    kv = N_KV * HEAD_DIM
    qt = _to_tiles_t((proj[:, COL_Q:COL_Q + N_HEADS * HEAD_DIM] * (HEAD_DIM ** -0.5)).astype(BF16),
                     n_batch, seq, HEAD_DIM)
    gates = proj[:, COL_GATE:COL_GATE + 3 * N_HEADS].reshape(-1, N_HEADS, 3)
    gate = [_to_tiles_t(gates[:, :, c], n_batch, seq, 1) for c in range(3)]

    rows16 = lambda col: proj[:, col:col + kv].reshape(n_batch, seq // CMP_STRIDE, CMP_STRIDE * kv)
    k_cmp = compress(rows16(COL_KC), *cmp_k_w)
    v_cmp = compress(rows16(COL_VC), *cmp_v_w)
    k_cmp = k_cmp.reshape(n_batch, -1, N_KV, HEAD_DIM).transpose(0, 2, 1, 3)
    vt_cmp = v_cmp.reshape(n_batch, -1, N_KV, HEAD_DIM).transpose(0, 2, 3, 1)
    o_cmp, sel = cmp_attention(qt, k_cmp, vt_cmp, tbl_c, gate[0], smat)

    ks = _to_heads(proj[:, COL_KS:COL_KS + kv], n_batch, seq).astype(BF16)
    ka = jnp.concatenate([jnp.broadcast_to(onehot, (n_batch, N_KV) + onehot.shape), ks], axis=-1)
    vst = _to_heads_t(proj[:, COL_VS:COL_VS + kv], n_batch, seq).astype(BF16)
    extra = jnp.zeros((n_batch, N_KV, V_ROWS - HEAD_DIM, seq), BF16).at[:, :, 0].set(1.0)
    o_sel = sel_attention(qt, sel, ka, jnp.concatenate([vst, extra], axis=2), tbl_s, gate[1])

    kw = jnp.pad(_to_heads(proj[:, COL_KW:COL_KW + kv], n_batch, seq).astype(BF16),
                 ((0, 0), (0, 0), (WINDOW, 0), (0, 0)))
    vwt = jnp.pad(_to_heads_t(proj[:, COL_VW:COL_VW + kv], n_batch, seq).astype(BF16),
                  ((0, 0), (0, 0), (0, 0), (WINDOW, 0)))
    o_win = win_attention(qt, kw, vwt, tbl_w, gate[2])
    return tuple(_from_tiles_t(o, n_batch, seq) for o in (o_cmp, o_sel, o_win))


def kernel(x, rpb_table, attn_norm, ffn_norm, final_norm, w_in, ssm_a_re, ssm_a_im, ssm_log_dt,
           ssm_b_re, ssm_b_im, ssm_c_re, ssm_c_im, ssm_d, w_glu, b_glu, cmp_pos_k, cmp_w1_k, cmp_w2_k,
           cmp_pos_v, cmp_w1_v, cmp_w2_v, w_out, w_ffn_gate, w_ffn_up, w_ffn_down):
    n_batch, seq, d_model = x.shape
    depth = w_in.shape[0]
    n_sel = seq // SEL_BLOCK
    tables = _bias_tables(rpb_table)
    smat = _slc_matrix(seq // CMP_STRIDE, n_sel)
    blk = jnp.arange(seq) // SEL_BLOCK
    onehot = jnp.where(blk[:, None] == jnp.arange(n_sel)[None, :], ONEHOT_BIG, 0.0).astype(BF16)

    xf = x.reshape(n_batch * seq, d_model)
    for l in range(depth):
        w_in_l = jnp.pad(w_in[l], ((0, 0), (0, IN_COLS_PAD - IN_COLS))).astype(BF16)
        proj = norm_matmul(xf, attn_norm[l].reshape(1, -1), w_in_l)
        s5w = _s5_weights(ssm_a_re[l], ssm_a_im[l], ssm_log_dt[l], ssm_b_re[l], ssm_b_im[l],
                          ssm_c_re[l], ssm_c_im[l])
        y_ssm = s5_glu(proj, n_batch, *s5w, ssm_d[l].reshape(1, -1), w_glu[l].astype(BF16),
                       b_glu[l].reshape(1, -1))
        o_cmp, o_sel, o_win = nsa_branches(
            proj, n_batch, seq,
            _compress_weights(cmp_pos_k[l], cmp_w1_k[l], cmp_w2_k[l]),
            _compress_weights(cmp_pos_v[l], cmp_w1_v[l], cmp_w2_v[l]),
            tables, smat, onehot)
        w_out_l = w_out[l].astype(BF16)
        xf = out_proj(xf, y_ssm, o_cmp, o_sel, o_win, w_out_l[:D_SSM], w_out_l[D_SSM:])
        xf = ffn(xf, ffn_norm[l].reshape(1, -1), w_ffn_gate[l].astype(BF16),
                 w_ffn_up[l].astype(BF16), w_ffn_down[l].astype(BF16))
    out = final_rms_norm(xf, final_norm.reshape(1, -1))
    return out.reshape(n_batch, seq, d_model)
```

```python
import math

import jax
import jax.numpy as jnp
import numpy as np
from jax import lax
from jax.experimental import pallas as pl
from jax.experimental.pallas import tpu as pltpu

F32 = jnp.float32
BF16 = jnp.bfloat16

D_MODEL = 2048
DEPTH = 4
D_SSM = 1024
SSM_GROUP = 16
N_SSM_GROUPS = 64
SSM_STATE = 64
N_STATE = N_SSM_GROUPS * SSM_STATE
SSM_PACK = 16
N_PACKS = N_SSM_GROUPS // SSM_PACK
N_HEADS = 16
N_KV = 4
GQA_R = 4
HEAD_DIM = 64
CMP_BLOCK = 32
CMP_STRIDE = 16
CMP_HIDDEN = 128
SEL_BLOCK = 64
SEL_TOP = 16
WINDOW = 512
Q_TILE = 128
ROWS = GQA_R * Q_TILE
FORCED_SCORE = 1e4
N_BUCKETS = 32
RPB_MAX_EXACT = 16
RPB_MAX_DIST = 1024
D_FF = 5632
NORM_EPS = 1e-6
IN_COLS = 3632
IN_COLS_PAD = 3840
COL_U, COL_Q, COL_KC, COL_VC, COL_KS, COL_VS, COL_KW, COL_VW, COL_GATE = (
    0, 1024, 2048, 2304, 2560, 2816, 3072, 3328, 3584)
NEG = -1e30
ONEHOT_BIG = 1e30
FAR_DIST = 2048
SEL_KEY_TILE = 256
SEL_NEAR_TILES = 11
CMP_LANE = 128
V_ROWS = 80
VMEM_LIMIT = 56 * 1024 * 1024


def _cparams(sem):
    return pltpu.CompilerParams(dimension_semantics=sem, vmem_limit_bytes=VMEM_LIMIT)


def _rms(x, g):
    ms = jnp.mean(x * x, axis=-1, keepdims=True)
    return x * lax.rsqrt(ms + NORM_EPS) * g


def _norm_matmul_kernel(x_ref, g_ref, w_ref, o_ref, h_ref):
    @pl.when(pl.program_id(1) == 0)
    def _():
        h_ref[...] = _rms(x_ref[...], g_ref[...]).astype(BF16)

    o_ref[...] = jnp.dot(h_ref[...], w_ref[...], preferred_element_type=F32)


def norm_matmul(x, g, w, *, tm=512, tn=768):
    m, k = x.shape
    n = w.shape[1]
    return pl.pallas_call(
        _norm_matmul_kernel,
        grid=(m // tm, n // tn),
        in_specs=[pl.BlockSpec((tm, k), lambda i, j: (i, 0)),
                  pl.BlockSpec((1, k), lambda i, j: (0, 0)),
                  pl.BlockSpec((k, tn), lambda i, j: (0, j))],
        out_specs=pl.BlockSpec((tm, tn), lambda i, j: (i, j)),
        out_shape=jax.ShapeDtypeStruct((m, n), F32),
        scratch_shapes=[pltpu.VMEM((tm, k), BF16)],
        compiler_params=_cparams(("parallel", "arbitrary")),
        name="norm_matmul",
    )(x, g, w)


def _out_proj_kernel(x_ref, ys_ref, yn_ref, w1_ref, w2_ref, o_ref):
    o_ref[...] = (x_ref[...]
                  + jnp.dot(ys_ref[...], w1_ref[...], preferred_element_type=F32)
                  + jnp.dot(yn_ref[...], w2_ref[...], preferred_element_type=F32))


def out_proj(x, y_ssm, y_nsa, w_ssm, w_nsa, *, tm=512):
    m, d = x.shape
    k = y_ssm.shape[1]
    row = lambda i: (i, 0)
    full = lambda i: (0, 0)
    return pl.pallas_call(
        _out_proj_kernel,
        grid=(m // tm,),
        in_specs=[pl.BlockSpec((tm, d), row), pl.BlockSpec((tm, k), row), pl.BlockSpec((tm, k), row),
                  pl.BlockSpec((k, d), full), pl.BlockSpec((k, d), full)],
        out_specs=pl.BlockSpec((tm, d), row),
        out_shape=jax.ShapeDtypeStruct((m, d), F32),
        compiler_params=_cparams(("parallel",)),
        name="out_proj",
    )(x, y_ssm, y_nsa, w_ssm, w_nsa)


def _ffn_kernel(x_ref, g_ref, wg_ref, wu_ref, wd_ref, o_ref, h_ref):
    f = pl.program_id(1)

    @pl.when(f == 0)
    def _():
        x = x_ref[...]
        h_ref[...] = _rms(x, g_ref[...]).astype(BF16)
        o_ref[...] = x

    h = h_ref[...]
    gate = jnp.dot(h, wg_ref[...], preferred_element_type=F32)
    up = jnp.dot(h, wu_ref[...], preferred_element_type=F32)
    act = (jax.nn.silu(gate) * up).astype(BF16)
    o_ref[...] += jnp.dot(act, wd_ref[...], preferred_element_type=F32)


def ffn(x, g, w_gate, w_up, w_down, *, tm=512, tf=512):
    m, d = x.shape
    dff = w_gate.shape[1]
    return pl.pallas_call(
        _ffn_kernel,
        grid=(m // tm, dff // tf),
        in_specs=[pl.BlockSpec((tm, d), lambda i, f: (i, 0)),
                  pl.BlockSpec((1, d), lambda i, f: (0, 0)),
                  pl.BlockSpec((d, tf), lambda i, f: (0, f)),
                  pl.BlockSpec((d, tf), lambda i, f: (0, f)),
                  pl.BlockSpec((tf, d), lambda i, f: (f, 0))],
        out_specs=pl.BlockSpec((tm, d), lambda i, f: (i, 0)),
        out_shape=jax.ShapeDtypeStruct((m, d), F32),
        scratch_shapes=[pltpu.VMEM((tm, d), BF16)],
        compiler_params=_cparams(("parallel", "arbitrary")),
        name="ffn",
    )(x, g, w_gate, w_up, w_down)


def _final_norm_kernel(x_ref, g_ref, o_ref):
    o_ref[...] = _rms(x_ref[...], g_ref[...])


def final_rms_norm(x, g, *, tm=512):
    m, d = x.shape
    return pl.pallas_call(
        _final_norm_kernel,
        grid=(m // tm,),
        in_specs=[pl.BlockSpec((tm, d), lambda i: (i, 0)), pl.BlockSpec((1, d), lambda i: (0, 0))],
        out_specs=pl.BlockSpec((tm, d), lambda i: (i, 0)),
        out_shape=jax.ShapeDtypeStruct((m, d), F32),
        compiler_params=_cparams(("parallel",)),
        name="final_norm",
    )(x, g)


def _s5_discretize_kernel(are_ref, aim_ref, ldt_ref, bre_ref, bim_ref,
                          abre_ref, abim_ref, bbre_ref, bbim_ref):
    lam_re = jnp.minimum(are_ref[...], -1e-4)
    lam_im = aim_ref[...]
    dt = jnp.exp(ldt_ref[...])
    mag = jnp.exp(lam_re * dt)
    ab_re = mag * jnp.cos(lam_im * dt)
    ab_im = mag * jnp.sin(lam_im * dt)
    nr, ni = ab_re - 1.0, ab_im
    den = lam_re * lam_re + lam_im * lam_im
    f_re = (nr * lam_re + ni * lam_im) / den
    f_im = (ni * lam_re - nr * lam_im) / den
    br, bi = bre_ref[...], bim_ref[...]
    abre_ref[...] = ab_re
    abim_ref[...] = ab_im
    bbre_ref[...] = f_re * br - f_im * bi
    bbim_ref[...] = f_re * bi + f_im * br


def s5_discretize(a_re, a_im, log_dt, b_re_t, b_im_t):
    g, p = a_re.shape
    h = b_re_t.shape[1]
    outs = pl.pallas_call(
        _s5_discretize_kernel,
        out_shape=[jax.ShapeDtypeStruct((g, 1, p), F32), jax.ShapeDtypeStruct((g, 1, p), F32),
                   jax.ShapeDtypeStruct((g, h, p), F32), jax.ShapeDtypeStruct((g, h, p), F32)],
        name="s5_discretize",
    )(a_re.reshape(g, 1, p), a_im.reshape(g, 1, p), log_dt.reshape(g, 1, 1), b_re_t, b_im_t)
    return outs


def _s5_kernel(u_ref, bre_ref, bim_ref, cre_ref, cimn_ref, a_ref, d_ref, wglu_ref, bglu_ref,
               o_ref, xr_ref, xi_ref, st_ref):
    tc = u_ref.shape[0]
    pw = SSM_PACK * SSM_GROUP
    sw = SSM_PACK * SSM_STATE

    @pl.when(pl.program_id(1) == 0)
    def _():
        st_ref[...] = jnp.zeros_like(st_ref)

    u = u_ref[...]
    ub = u.astype(BF16)
    for k in range(N_PACKS):
        uk = ub[:, k * pw:(k + 1) * pw]
        xr_ref[:, k * sw:(k + 1) * sw] = jnp.dot(uk, bre_ref[k], preferred_element_type=F32)
        xi_ref[:, k * sw:(k + 1) * sw] = jnp.dot(uk, bim_ref[k], preferred_element_type=F32)

    def step(t, carry):
        a_re = a_ref[0:1, :]
        a_im = a_ref[1:2, :]
        s_re = st_ref[0:1, :]
        s_im = st_ref[1:2, :]
        n_re = a_re * s_re - a_im * s_im + xr_ref[pl.ds(t, 1), :]
        n_im = a_re * s_im + a_im * s_re + xi_ref[pl.ds(t, 1), :]
        xr_ref[pl.ds(t, 1), :] = n_re
        xi_ref[pl.ds(t, 1), :] = n_im
        st_ref[0:1, :] = n_re
        st_ref[1:2, :] = n_im
        return carry

    lax.fori_loop(0, tc, step, 0)

    ys = []
    for k in range(N_PACKS):
        xr = xr_ref[:, k * sw:(k + 1) * sw].astype(BF16)
        xi = xi_ref[:, k * sw:(k + 1) * sw].astype(BF16)
        ys.append(jnp.dot(xr, cre_ref[k], preferred_element_type=F32)
                  + jnp.dot(xi, cimn_ref[k], preferred_element_type=F32))
    y = jnp.concatenate(ys, axis=-1) + d_ref[...] * u
    y = jax.nn.gelu(y).astype(BF16)
    ab = jnp.dot(y, wglu_ref[...], preferred_element_type=F32) + bglu_ref[...]
    o_ref[...] = (ab[:, :D_SSM] * jax.nn.sigmoid(ab[:, D_SSM:])).astype(o_ref.dtype)


def s5_glu(proj, n_batch, bbd_re, bbd_im, cbd_re, cbd_imn, a_flat, d, w_glu, b_glu, *, tc=256):
    t = proj.shape[0]
    n_chunks = t // n_batch // tc
    c3 = lambda b, c: (0, 0, 0)
    c2 = lambda b, c: (0, 0)
    return pl.pallas_call(
        _s5_kernel,
        grid=(n_batch, n_chunks),
        in_specs=[pl.BlockSpec((tc, D_SSM), lambda b, c: (b * n_chunks + c, 0)),
                  pl.BlockSpec(bbd_re.shape, c3), pl.BlockSpec(bbd_im.shape, c3),
                  pl.BlockSpec(cbd_re.shape, c3), pl.BlockSpec(cbd_imn.shape, c3),
                  pl.BlockSpec(a_flat.shape, c2), pl.BlockSpec(d.shape, c2),
                  pl.BlockSpec(w_glu.shape, c2), pl.BlockSpec(b_glu.shape, c2)],
        out_specs=pl.BlockSpec((tc, D_SSM), lambda b, c: (b * n_chunks + c, 0)),
        out_shape=jax.ShapeDtypeStruct((t, D_SSM), BF16),
        scratch_shapes=[pltpu.VMEM((tc, N_STATE), F32), pltpu.VMEM((tc, N_STATE), F32),
                        pltpu.VMEM((2, N_STATE), F32)],
        compiler_params=_cparams(("arbitrary", "arbitrary")),
        name="s5_glu",
    )(proj, bbd_re, bbd_im, cbd_re, cbd_imn, a_flat, d, w_glu, b_glu)


def _s5_weights(a_re, a_im, log_dt, b_re, b_im, c_re, c_im):
    ab_re, ab_im, bb_re, bb_im = s5_discretize(a_re, a_im, log_dt,
                                               b_re.transpose(0, 2, 1), b_im.transpose(0, 2, 1))
    eye = jnp.eye(SSM_PACK, dtype=F32)

    def pack_b(bb):
        bb = bb.reshape(N_PACKS, SSM_PACK, SSM_GROUP, 1, SSM_STATE)
        m = bb * eye[None, :, None, :, None]
        return m.reshape(N_PACKS, SSM_PACK * SSM_GROUP, SSM_PACK * SSM_STATE).astype(BF16)

    def pack_c(c):
        c = c.reshape(N_PACKS, SSM_PACK, SSM_GROUP, SSM_STATE)
        m = c.transpose(0, 1, 3, 2)[:, :, :, None, :] * eye[None, :, None, :, None]
        return m.reshape(N_PACKS, SSM_PACK * SSM_STATE, SSM_PACK * SSM_GROUP).astype(BF16)

    a_flat = jnp.stack([ab_re.reshape(-1), ab_im.reshape(-1)])
    return pack_b(bb_re), pack_b(bb_im), pack_c(c_re), pack_c(-c_im), a_flat


def _compress_kernel(r_ref, n_ref, pos_ref, w1a_ref, w1b_ref, w2_ref, o_ref):
    r = r_ref[0]
    rows = r.shape[0]
    rowid = lax.broadcasted_iota(jnp.int32, (rows, 1), 0)
    nxt = jnp.where(rowid == rows - 1, n_ref[0, 0:1, :], pltpu.roll(r, rows - 1, 0))
    first = (r + pos_ref[0:1, :]).astype(BF16)
    second = (nxt + pos_ref[1:2, :]).astype(BF16)
    hid = (jnp.dot(first, w1a_ref[...], preferred_element_type=F32)
           + jnp.dot(second, w1b_ref[...], preferred_element_type=F32))
    hid = jax.nn.gelu(hid).astype(BF16)
    o_ref[0] = jnp.dot(hid, w2_ref[...], preferred_element_type=F32).astype(o_ref.dtype)


def compress(kc_rows, pos2, w1a, w1b, w2e):
    b, nc, w = kc_rows.shape
    tr = 128
    sub = 8
    full = lambda i, j: (0, 0)
    nxt_blk = lambda i, j: (i, jnp.minimum((j + 1) * (tr // sub), nc // sub - 1), 0)
    return pl.pallas_call(
        _compress_kernel,
        grid=(b, nc // tr),
        in_specs=[pl.BlockSpec((1, tr, w), lambda i, j: (i, j, 0)),
                  pl.BlockSpec((1, sub, w), nxt_blk),
                  pl.BlockSpec(pos2.shape, full), pl.BlockSpec(w1a.shape, full),
                  pl.BlockSpec(w1b.shape, full), pl.BlockSpec(w2e.shape, full)],
        out_specs=pl.BlockSpec((1, tr, N_KV * HEAD_DIM), lambda i, j: (i, j, 0)),
        out_shape=jax.ShapeDtypeStruct((b, nc, N_KV * HEAD_DIM), BF16),
        compiler_params=_cparams(("parallel", "parallel")),
        name="compress",
    )(kc_rows, kc_rows, pos2, w1a, w1b, w2e)


def _compress_weights(pos, w1, w2):
    eye = jnp.eye(N_KV, dtype=F32)
    w1 = w1.reshape(2, CMP_STRIDE, 1, HEAD_DIM, 1, CMP_HIDDEN)
    w1e = (w1 * eye[None, None, :, None, :, None]).reshape(
        2, CMP_STRIDE * N_KV * HEAD_DIM, N_KV * CMP_HIDDEN).astype(BF16)
    w2e = (w2[None, :, None, :] * eye[:, None, :, None]).reshape(
        N_KV * CMP_HIDDEN, N_KV * HEAD_DIM).astype(BF16)
    pos2 = jnp.broadcast_to(pos.reshape(2, CMP_STRIDE, 1, HEAD_DIM),
                            (2, CMP_STRIDE, N_KV, HEAD_DIM)).reshape(2, -1)
    return pos2, w1e[0], w1e[1], w2e


def _kv_prep_kernel(ks_ref, vs_ref, kw_ref, vw_ref, ka_ref, vst_ref, kwp_ref, vwtp_ref):
    j = pl.program_id(1)
    ts = ks_ref.shape[0]
    n_sel = ka_ref.shape[3] - HEAD_DIM
    blk0 = jnp.maximum(j - 1, 0) * (ts // SEL_BLOCK)
    row_blk = blk0 + lax.broadcasted_iota(jnp.int32, (ts, n_sel), 0) // SEL_BLOCK
    col = lax.broadcasted_iota(jnp.int32, (ts, n_sel), 1)
    onehot = jnp.where(row_blk == col, ONEHOT_BIG, 0.0).astype(BF16)
    ks = ks_ref[...]
    vs_t = vs_ref[...].T
    pad_rows = lax.broadcasted_iota(jnp.int32, (V_ROWS - HEAD_DIM, ts), 0)
    extra = jnp.where(pad_rows == 0, 1.0, 0.0).astype(BF16)
    for g in range(N_KV):
        lo, hi = g * HEAD_DIM, (g + 1) * HEAD_DIM
        ka_ref[0, g, :, 0:n_sel] = onehot
        ka_ref[0, g, :, n_sel:] = ks[:, lo:hi].astype(BF16)
        vst_ref[0, g, 0:HEAD_DIM, :] = vs_t[lo:hi].astype(BF16)
        vst_ref[0, g, HEAD_DIM:, :] = extra

    @pl.when(j == 0)
    def _():
        kwp_ref[...] = jnp.zeros_like(kwp_ref)
        vwtp_ref[...] = jnp.zeros_like(vwtp_ref)

    @pl.when(j > 0)
    def _():
        kw = kw_ref[...]
        vw_t = vw_ref[...].T
        for g in range(N_KV):
            lo, hi = g * HEAD_DIM, (g + 1) * HEAD_DIM
            kwp_ref[0, g] = kw[:, lo:hi].astype(BF16)
            vwtp_ref[0, g] = vw_t[lo:hi].astype(BF16)


def kv_prep(proj, n_batch, seq):
    ts = WINDOW
    nblk = seq // ts
    n_sel = seq // SEL_BLOCK
    kvw = N_KV * HEAD_DIM
    src = lambda col: pl.BlockSpec((ts, kvw), lambda b, j: (b * nblk + jnp.maximum(j - 1, 0), col // kvw))
    same = lambda b, j: (b, 0, jnp.maximum(j - 1, 0), 0)
    same_t = lambda b, j: (b, 0, 0, jnp.maximum(j - 1, 0))
    return pl.pallas_call(
        _kv_prep_kernel,
        grid=(n_batch, nblk + 1),
        in_specs=[src(COL_KS), src(COL_VS), src(COL_KW), src(COL_VW)],
        out_specs=[pl.BlockSpec((1, N_KV, ts, n_sel + HEAD_DIM), same),
                   pl.BlockSpec((1, N_KV, V_ROWS, ts), same_t),
                   pl.BlockSpec((1, N_KV, ts, HEAD_DIM), lambda b, j: (b, 0, j, 0)),
                   pl.BlockSpec((1, N_KV, HEAD_DIM, ts), lambda b, j: (b, 0, 0, j))],
        out_shape=[jax.ShapeDtypeStruct((n_batch, N_KV, seq, n_sel + HEAD_DIM), BF16),
                   jax.ShapeDtypeStruct((n_batch, N_KV, V_ROWS, seq), BF16),
                   jax.ShapeDtypeStruct((n_batch, N_KV, seq + WINDOW, HEAD_DIM), BF16),
                   jax.ShapeDtypeStruct((n_batch, N_KV, HEAD_DIM, seq + WINDOW), BF16)],
        compiler_params=_cparams(("parallel", "arbitrary")),
        name="kv_prep",
    )(proj, proj, proj, proj)


def _softmax_cols(s):
    m = jnp.max(s, axis=0, keepdims=True)
    m = jnp.where(m < 0.5 * NEG, 0.0, m)
    e = jnp.exp(s - m)
    return e * (1.0 / jnp.maximum(jnp.sum(e, axis=0, keepdims=True), 1e-30))


def _cmp_branch(a, qt, gate, k_ref, vt_ref, tbl_ref, smat_ref, st_ref, rank_ref):
    n_row_tiles = k_ref.shape[2] // CMP_LANE
    ap = a // (FAR_DIST // Q_TILE)
    s = jnp.dot(k_ref[0, 0], qt, preferred_element_type=F32)
    tbl = tbl_ref[0, 0]
    pieces = []
    for c in range(n_row_tiles):
        st = s[c * CMP_LANE:(c + 1) * CMP_LANE]
        cur = st + tbl[CMP_LANE:]
        prev = st + tbl[:CMP_LANE]
        pieces.append(jnp.where(c == ap, cur, jnp.where(c == ap - 1, prev, jnp.where(c < ap, st, NEG))))
    p = _softmax_cols(jnp.concatenate(pieces, axis=0))
    o = gate * jnp.dot(vt_ref[0, 0], p.astype(BF16), preferred_element_type=F32)

    imp = (p[:, 0:Q_TILE] + p[:, Q_TILE:2 * Q_TILE] + p[:, 2 * Q_TILE:3 * Q_TILE]
           + p[:, 3 * Q_TILE:4 * Q_TILE])
    hi = imp.astype(BF16)
    r1 = imp - hi.astype(F32)
    mid = r1.astype(BF16)
    lo = (r1 - mid.astype(F32)).astype(BF16)
    smat = smat_ref[...]
    st = (jnp.dot(smat, hi, preferred_element_type=F32) + jnp.dot(smat, mid, preferred_element_type=F32)
          + jnp.dot(smat, lo, preferred_element_type=F32))

    t = a * Q_TILE + lax.broadcasted_iota(jnp.int32, st.shape, 1)
    jj = lax.broadcasted_iota(jnp.int32, st.shape, 0)
    cur_blk = t // SEL_BLOCK
    valid = jj * SEL_BLOCK <= t
    forced = (jj == 0) | (jj == cur_blk) | (jj == cur_blk - 1)
    st = jnp.where(valid, jnp.where(forced, FORCED_SCORE, st), -1.0)
    st_ref[...] = st
    rank_ref[...] = jnp.zeros_like(rank_ref)
    sub = 8
    chunk = 16
    n_sel = st.shape[0]
    last_valid = (a * Q_TILE + Q_TILE - 1) // SEL_BLOCK
    for q0 in range(0, n_sel, chunk):
        @pl.when(q0 <= last_valid)
        def _(q0=q0):
            s_all = st_ref[...]
            parts = []
            for v in range(n_sel // sub):
                sv = s_all[v * sub:(v + 1) * sub]
                jv = jj[v * sub:(v + 1) * sub]
                cnt = rank_ref[v * sub:(v + 1) * sub, :]
                for jp in range(q0, min(q0 + chunk, n_sel)):
                    row = s_all[jp:jp + 1, :]
                    if jp < v * sub:
                        ahead = row >= sv
                    elif jp >= (v + 1) * sub:
                        ahead = row > sv
                    else:
                        ahead = (row > sv) | ((row == sv) & (jv > jp))
                    cnt = cnt + jnp.where(ahead, 1.0, 0.0)
                parts.append(cnt)
            rank_ref[...] = jnp.concatenate(parts, axis=0)

    keep = (rank_ref[...] < SEL_TOP) & (st >= 0.0)
    return o, (keep.astype(F32) - 1.0).astype(BF16)


def _sel_branch(a, gate, ka_ref, vt_ref, tbl_ref, qa_ref, m_ref, acc_ref):
    tk = SEL_KEY_TILE
    m_ref[...] = jnp.full_like(m_ref, NEG)
    acc_ref[...] = jnp.zeros_like(acc_ref)
    n_tiles = a // (tk // Q_TILE) + 1
    first_near = jnp.maximum(0, (a - (SEL_NEAR_TILES - 2)) // 2)
    first_near = first_near + first_near % 2
    near_pairs = (n_tiles - first_near) // 2

    def attend(tiles):
        m = m_first = m_ref[...]
        done = []
        for c, bias in tiles:
            off = pl.multiple_of(c * tk, tk)
            s = jnp.dot(ka_ref[0, 0, pl.ds(off, tk), :], qa_ref[...], preferred_element_type=F32)
            if bias is not None:
                s = s + bias
            m = jnp.maximum(m, jnp.max(s, axis=0, keepdims=True))
            done.append((off, jnp.exp(s - m).astype(BF16), m))
        acc = jnp.exp(m_first - m) * acc_ref[...]
        for off, p, m_t in done:
            pv = jnp.dot(vt_ref[0, 0, :, pl.ds(off, tk)], p, preferred_element_type=F32)
            acc = acc + (pv if m_t is m else jnp.exp(m_t - m) * pv)
        acc_ref[...] = acc
        m_ref[...] = m

    def far_body(i, carry):
        attend([(2 * i, None), (2 * i + 1, None)])
        return carry

    def near_body(i, carry):
        c = first_near + 2 * i
        attend([(c, tbl_ref[0, a - 2 * c]), (c + 1, tbl_ref[0, a - 2 * c - 2])])
        return carry

    lax.fori_loop(0, first_near // 2, far_body, 0)
    lax.fori_loop(0, near_pairs, near_body, 0)

    @pl.when(first_near + 2 * near_pairs < n_tiles)
    def _():
        c = n_tiles - 1
        attend([(c, tbl_ref[0, a - 2 * c])])

    acc = acc_ref[...]
    return acc[0:HEAD_DIM] * (gate / acc[HEAD_DIM:HEAD_DIM + 1])


def _win_branch(a, qt, gate, k_ref, vt_ref, tbl_ref):
    span = WINDOW + Q_TILE
    off = pl.multiple_of(a * Q_TILE, Q_TILE)
    s = jnp.dot(k_ref[0, 0, pl.ds(off, span), :], qt, preferred_element_type=F32)
    s = s + tbl_ref[0]
    kj = lax.broadcasted_iota(jnp.int32, s.shape, 0)
    s = jnp.where(kj >= WINDOW - a * Q_TILE, s, NEG)
    p = _softmax_cols(s)
    return gate * jnp.dot(vt_ref[0, 0, :, pl.ds(off, span)], p.astype(BF16), preferred_element_type=F32)


def _nsa_kernel(q_ref, g_ref, kc_ref, vtc_ref, tblc_ref, smat_ref, ka_ref, vst_ref, tbls_ref,
                kw_ref, vwt_ref, tblw_ref, o_ref, gt_ref, st_ref, rank_ref, qa_ref, m_ref, acc_ref):
    g = pl.program_id(0)
    a = pl.program_id(2)
    n_sel = smat_ref.shape[0]
    half = 2 * HEAD_DIM
    q = q_ref[...] * (HEAD_DIM ** -0.5)
    t0 = q[:, :half].T
    t1 = q[:, half:].T
    qt = jnp.concatenate([t0[:HEAD_DIM], t0[HEAD_DIM:], t1[:HEAD_DIM], t1[HEAD_DIM:]],
                         axis=1).astype(BF16)
    gt_ref[...] = g_ref[:, :Q_TILE].T

    def gate(branch):
        rows = [gt_ref[pl.ds((g * GQA_R + r) * 3 + branch, 1), :] for r in range(GQA_R)]
        return jax.nn.sigmoid(jnp.concatenate(rows, axis=1))

    o, sel = _cmp_branch(a, qt, gate(0), kc_ref, vtc_ref, tblc_ref, smat_ref, st_ref, rank_ref)
    for r in range(GQA_R):
        qa_ref[0:n_sel, r * Q_TILE:(r + 1) * Q_TILE] = sel
    qa_ref[n_sel:, :] = qt
    o = o + _sel_branch(a, gate(1), ka_ref, vst_ref, tbls_ref, qa_ref, m_ref, acc_ref)
    o = o + _win_branch(a, qt, gate(2), kw_ref, vwt_ref, tblw_ref)
    u0 = jnp.concatenate([o[:, 0:Q_TILE], o[:, Q_TILE:2 * Q_TILE]], axis=0).T
    u1 = jnp.concatenate([o[:, 2 * Q_TILE:3 * Q_TILE], o[:, 3 * Q_TILE:]], axis=0).T
    o_ref[...] = jnp.concatenate([u0, u1], axis=1).astype(o_ref.dtype)


def nsa_attention(proj, n_batch, seq, k_cmp, vt_cmp, ka, vst, kwp, vwtp, tables, smat):
    tbl_c, tbl_s, tbl_w = tables
    n_a = seq // Q_TILE
    nc = k_cmp.shape[2]
    n_sel = smat.shape[0]
    dh = HEAD_DIM
    gw = GQA_R * HEAD_DIM
    tps = FAR_DIST // Q_TILE
    per_bg = lambda gi, bi, ai: (bi, gi, 0, 0)
    return pl.pallas_call(
        _nsa_kernel,
        grid=(N_KV, n_batch, n_a),
        in_specs=[pl.BlockSpec((Q_TILE, gw), lambda gi, bi, ai: (bi * n_a + ai, COL_Q // gw + gi)),
                  pl.BlockSpec((Q_TILE, gw), lambda gi, bi, ai: (bi * n_a + ai, COL_GATE // gw)),
                  pl.BlockSpec((1, 1, nc, dh), per_bg),
                  pl.BlockSpec((1, 1, dh, nc), per_bg),
                  pl.BlockSpec((1, 1, 2 * CMP_LANE, ROWS), lambda gi, bi, ai: (gi, ai % tps, 0, 0)),
                  pl.BlockSpec(smat.shape, lambda gi, bi, ai: (0, 0)),
                  pl.BlockSpec((1, 1, seq, n_sel + dh), per_bg),
                  pl.BlockSpec((1, 1, V_ROWS, seq), per_bg),
                  pl.BlockSpec((1, SEL_NEAR_TILES, SEL_KEY_TILE, ROWS), lambda gi, bi, ai: (gi, 0, 0, 0)),
                  pl.BlockSpec((1, 1, seq + WINDOW, dh), per_bg),
                  pl.BlockSpec((1, 1, dh, seq + WINDOW), per_bg),
                  pl.BlockSpec((1, WINDOW + Q_TILE, ROWS), lambda gi, bi, ai: (gi, 0, 0))],
        out_specs=pl.BlockSpec((Q_TILE, gw), lambda gi, bi, ai: (bi * n_a + ai, gi)),
        out_shape=jax.ShapeDtypeStruct((n_batch * seq, N_HEADS * HEAD_DIM), BF16),
        scratch_shapes=[pltpu.VMEM((Q_TILE, Q_TILE), F32),
                        pltpu.VMEM((n_sel, Q_TILE), F32), pltpu.VMEM((n_sel, Q_TILE), F32),
                        pltpu.VMEM((n_sel + dh, ROWS), BF16), pltpu.VMEM((1, ROWS), F32),
                        pltpu.VMEM((V_ROWS, ROWS), F32)],
        compiler_params=_cparams(("parallel", "parallel", "arbitrary")),
        name="nsa_attention",
    )(proj, proj, k_cmp, vt_cmp, tbl_c, smat, ka, vst, tbl_s, kwp, vwtp, tbl_w)


def _rel_bucket(dist):
    n = jnp.maximum(dist, 0)
    nf = jnp.maximum(n, 1).astype(F32)
    large = RPB_MAX_EXACT + (jnp.log(nf / RPB_MAX_EXACT) / math.log(RPB_MAX_DIST / RPB_MAX_EXACT)
                             * (N_BUCKETS - RPB_MAX_EXACT)).astype(jnp.int32)
    large = jnp.minimum(large, N_BUCKETS - 1)
    return jnp.where(n < RPB_MAX_EXACT, n, large)


def _toeplitz(e, n_rows):
    w = e.shape[-1]
    lead = e.shape[:-1]
    flat = jnp.broadcast_to(e[..., None, :], lead + (n_rows, w)).reshape(lead + (n_rows * w,))
    return flat[..., :n_rows * (w - 1)].reshape(lead + (n_rows, w - 1))


def _bias_tables(rpb_table):
    tab = rpb_table.astype(F32)
    by_dist = tab[_rel_bucket(jnp.arange(FAR_DIST))].T
    rel = by_dist - tab[N_BUCKETS - 1][:, None]
    neg = lambda n: jnp.full((N_HEADS, n), NEG, F32)
    zero = lambda n: jnp.zeros((N_HEADS, n), F32)

    def to_rows(m):
        m = m.reshape((N_KV, GQA_R) + m.shape[1:])
        m = jnp.moveaxis(m, 1, -2)
        return m.reshape(m.shape[:-2] + (ROWS,))

    tk = SEL_KEY_TILE
    ext = jnp.concatenate([neg(tk), rel], axis=1)
    win = jnp.stack([ext[:, Q_TILE * k + 1:Q_TILE * k + 1 + tk + Q_TILE] for k in range(SEL_NEAR_TILES)],
                    axis=1)
    tbl_s = to_rows(_toeplitz(win, tk)[..., tk - 1:tk - 1 + Q_TILE])

    span = WINDOW + Q_TILE
    ext = jnp.concatenate([neg(Q_TILE - 1), by_dist[:, :WINDOW], neg(Q_TILE + 1)], axis=1)
    tbl_w = to_rows(_toeplitz(ext, span)[..., span - 1:span - 1 + Q_TILE])

    n_m = 2 * CMP_LANE
    n_tau = FAR_DIST // CMP_STRIDE
    base = CMP_STRIDE * (n_m - 1) - (FAR_DIST - CMP_BLOCK + 1)
    length = CMP_STRIDE * (n_m + n_tau)
    ext = jnp.concatenate([neg(base), rel, zero(length - base - FAR_DIST)], axis=1)
    ev = ext.reshape(N_HEADS, n_m + n_tau, CMP_STRIDE).transpose(0, 2, 1)
    tz = _toeplitz(ev, n_m)[..., n_m - 1:n_m - 1 + n_tau]
    tz = tz.transpose(0, 2, 3, 1).reshape(N_HEADS, n_m, FAR_DIST // Q_TILE, Q_TILE)
    tbl_c = to_rows(tz.transpose(0, 2, 1, 3))
    return tbl_c, tbl_s, tbl_w


def _slc_matrix(n_cmp_pad, n_sel):
    sel_ratio, cmp_ratio = SEL_BLOCK // CMP_STRIDE, CMP_BLOCK // CMP_STRIDE
    mat = np.zeros((n_sel, n_cmp_pad), np.float32)
    for j in range(n_sel):
        for mm in range(sel_ratio):
            for nn in range(cmp_ratio):
                idx = sel_ratio * j - mm - nn
                if 0 <= idx < n_cmp_pad - 1:
                    mat[j, idx] += 1.0
    return jnp.asarray(mat, BF16)


def nsa_mixer(proj, n_batch, seq, cmp_k_w, cmp_v_w, tables, smat):
    kv = N_KV * HEAD_DIM
    rows16 = lambda col: proj[:, col:col + kv].reshape(n_batch, seq // CMP_STRIDE, CMP_STRIDE * kv)
    k_cmp = compress(rows16(COL_KC), *cmp_k_w)
    v_cmp = compress(rows16(COL_VC), *cmp_v_w)
    k_cmp = k_cmp.reshape(n_batch, -1, N_KV, HEAD_DIM).transpose(0, 2, 1, 3)
    vt_cmp = v_cmp.reshape(n_batch, -1, N_KV, HEAD_DIM).transpose(0, 2, 3, 1)
    ka, vst, kwp, vwtp = kv_prep(proj, n_batch, seq)
    return nsa_attention(proj, n_batch, seq, k_cmp, vt_cmp, ka, vst, kwp, vwtp, tables, smat)


def kernel(x, rpb_table, attn_norm, ffn_norm, final_norm, w_in, ssm_a_re, ssm_a_im, ssm_log_dt,
           ssm_b_re, ssm_b_im, ssm_c_re, ssm_c_im, ssm_d, w_glu, b_glu, cmp_pos_k, cmp_w1_k, cmp_w2_k,
           cmp_pos_v, cmp_w1_v, cmp_w2_v, w_out, w_ffn_gate, w_ffn_up, w_ffn_down):
    n_batch, seq, d_model = x.shape
    depth = w_in.shape[0]
    tables = _bias_tables(rpb_table)
    smat = _slc_matrix(seq // CMP_STRIDE, seq // SEL_BLOCK)

    xf = x.reshape(n_batch * seq, d_model)
    for l in range(depth):
        w_in_l = jnp.pad(w_in[l], ((0, 0), (0, IN_COLS_PAD - IN_COLS))).astype(BF16)
        proj = norm_matmul(xf, attn_norm[l].reshape(1, -1), w_in_l)
        s5w = _s5_weights(ssm_a_re[l], ssm_a_im[l], ssm_log_dt[l], ssm_b_re[l], ssm_b_im[l],
                          ssm_c_re[l], ssm_c_im[l])
        y_ssm = s5_glu(proj, n_batch, *s5w, ssm_d[l].reshape(1, -1), w_glu[l].astype(BF16),
                       b_glu[l].reshape(1, -1))
        y_nsa = nsa_mixer(proj, n_batch, seq,
                          _compress_weights(cmp_pos_k[l], cmp_w1_k[l], cmp_w2_k[l]),
                          _compress_weights(cmp_pos_v[l], cmp_w1_v[l], cmp_w2_v[l]),
                          tables, smat)
        w_out_l = w_out[l].astype(BF16)
        xf = out_proj(xf, y_ssm, y_nsa, w_out_l[:D_SSM], w_out_l[D_SSM:])
        xf = ffn(xf, ffn_norm[l].reshape(1, -1), w_ffn_gate[l].astype(BF16),
                 w_ffn_up[l].astype(BF16), w_ffn_down[l].astype(BF16))
    out = final_rms_norm(xf, final_norm.reshape(1, -1))
    return out.reshape(n_batch, seq, d_model)
```

```python
import math

import jax
import jax.numpy as jnp
import numpy as np
from jax import lax
from jax.experimental import pallas as pl
from jax.experimental.pallas import tpu as pltpu

F32 = jnp.float32
BF16 = jnp.bfloat16

D_MODEL = 2048
DEPTH = 4
D_SSM = 1024
SSM_GROUP = 16
N_SSM_GROUPS = 64
SSM_STATE = 64
N_STATE = N_SSM_GROUPS * SSM_STATE
SSM_PACK = 16
N_PACKS = N_SSM_GROUPS // SSM_PACK
N_HEADS = 16
N_KV = 4
GQA_R = 4
HEAD_DIM = 64
CMP_BLOCK = 32
CMP_STRIDE = 16
CMP_HIDDEN = 128
SEL_BLOCK = 64
SEL_TOP = 16
WINDOW = 512
Q_TILE = 128
ROWS = GQA_R * Q_TILE
FORCED_SCORE = 1e4
N_BUCKETS = 32
RPB_MAX_EXACT = 16
RPB_MAX_DIST = 1024
D_FF = 5632
NORM_EPS = 1e-6
IN_COLS = 3632
IN_COLS_PAD = 3840
COL_U, COL_Q, COL_KC, COL_VC, COL_KS, COL_VS, COL_KW, COL_VW, COL_GATE = (
    0, 1024, 2048, 2304, 2560, 2816, 3072, 3328, 3584)
NEG = -1e30
ONEHOT_BIG = 1e30
FAR_DIST = 2048
SEL_KEY_TILE = 256
SEL_NEAR_TILES = 12
LOG2E = math.log2(math.e)
SAFE_LOG2_RANGE = 64.0
CMP_LANE = 128
V_ROWS = 80
VMEM_LIMIT = 56 * 1024 * 1024


def _cparams(sem):
    return pltpu.CompilerParams(dimension_semantics=sem, vmem_limit_bytes=VMEM_LIMIT)


def _rms(x, g):
    ms = jnp.mean(x * x, axis=-1, keepdims=True)
    return x * lax.rsqrt(ms + NORM_EPS) * g


def _norm_matmul_kernel(x_ref, g_ref, w_ref, o_ref, h_ref):
    @pl.when(pl.program_id(1) == 0)
    def _():
        h_ref[...] = _rms(x_ref[...], g_ref[...]).astype(BF16)

    o_ref[...] = jnp.dot(h_ref[...], w_ref[...], preferred_element_type=F32)


def norm_matmul(x, g, w, *, tm=512, tn=768):
    m, k = x.shape
    n = w.shape[1]
    return pl.pallas_call(
        _norm_matmul_kernel,
        grid=(m // tm, n // tn),
        in_specs=[pl.BlockSpec((tm, k), lambda i, j: (i, 0)),
                  pl.BlockSpec((1, k), lambda i, j: (0, 0)),
                  pl.BlockSpec((k, tn), lambda i, j: (0, j))],
        out_specs=pl.BlockSpec((tm, tn), lambda i, j: (i, j)),
        out_shape=jax.ShapeDtypeStruct((m, n), F32),
        scratch_shapes=[pltpu.VMEM((tm, k), BF16)],
        compiler_params=_cparams(("parallel", "arbitrary")),
        name="norm_matmul",
    )(x, g, w)


def _out_proj_kernel(x_ref, ys_ref, yn_ref, w1_ref, w2_ref, o_ref):
    o_ref[...] = (x_ref[...]
                  + jnp.dot(ys_ref[...], w1_ref[...], preferred_element_type=F32)
                  + jnp.dot(yn_ref[...], w2_ref[...], preferred_element_type=F32))


def out_proj(x, y_ssm, y_nsa, w_ssm, w_nsa, *, tm=512):
    m, d = x.shape
    k = y_ssm.shape[1]
    row = lambda i: (i, 0)
    full = lambda i: (0, 0)
    return pl.pallas_call(
        _out_proj_kernel,
        grid=(m // tm,),
        in_specs=[pl.BlockSpec((tm, d), row), pl.BlockSpec((tm, k), row), pl.BlockSpec((tm, k), row),
                  pl.BlockSpec((k, d), full), pl.BlockSpec((k, d), full)],
        out_specs=pl.BlockSpec((tm, d), row),
        out_shape=jax.ShapeDtypeStruct((m, d), F32),
        compiler_params=_cparams(("parallel",)),
        name="out_proj",
    )(x, y_ssm, y_nsa, w_ssm, w_nsa)


def _ffn_kernel(x_ref, g_ref, wg_ref, wu_ref, wd_ref, o_ref, h_ref):
    f = pl.program_id(1)

    @pl.when(f == 0)
    def _():
        x = x_ref[...]
        h_ref[...] = _rms(x, g_ref[...]).astype(BF16)
        o_ref[...] = x

    h = h_ref[...]
    gate = jnp.dot(h, wg_ref[...], preferred_element_type=F32)
    up = jnp.dot(h, wu_ref[...], preferred_element_type=F32)
    act = (jax.nn.silu(gate) * up).astype(BF16)
    o_ref[...] += jnp.dot(act, wd_ref[...], preferred_element_type=F32)


def ffn(x, g, w_gate, w_up, w_down, *, tm=512, tf=512):
    m, d = x.shape
    dff = w_gate.shape[1]
    return pl.pallas_call(
        _ffn_kernel,
        grid=(m // tm, dff // tf),
        in_specs=[pl.BlockSpec((tm, d), lambda i, f: (i, 0)),
                  pl.BlockSpec((1, d), lambda i, f: (0, 0)),
                  pl.BlockSpec((d, tf), lambda i, f: (0, f)),
                  pl.BlockSpec((d, tf), lambda i, f: (0, f)),
                  pl.BlockSpec((tf, d), lambda i, f: (f, 0))],
        out_specs=pl.BlockSpec((tm, d), lambda i, f: (i, 0)),
        out_shape=jax.ShapeDtypeStruct((m, d), F32),
        scratch_shapes=[pltpu.VMEM((tm, d), BF16)],
        compiler_params=_cparams(("parallel", "arbitrary")),
        name="ffn",
    )(x, g, w_gate, w_up, w_down)


def _final_norm_kernel(x_ref, g_ref, o_ref):
    o_ref[...] = _rms(x_ref[...], g_ref[...])


def final_rms_norm(x, g, *, tm=512):
    m, d = x.shape
    return pl.pallas_call(
        _final_norm_kernel,
        grid=(m // tm,),
        in_specs=[pl.BlockSpec((tm, d), lambda i: (i, 0)), pl.BlockSpec((1, d), lambda i: (0, 0))],
        out_specs=pl.BlockSpec((tm, d), lambda i: (i, 0)),
        out_shape=jax.ShapeDtypeStruct((m, d), F32),
        compiler_params=_cparams(("parallel",)),
        name="final_norm",
    )(x, g)


def _s5_discretize_kernel(are_ref, aim_ref, ldt_ref, bre_ref, bim_ref,
                          abre_ref, abim_ref, bbre_ref, bbim_ref):
    lam_re = jnp.minimum(are_ref[...], -1e-4)
    lam_im = aim_ref[...]
    dt = jnp.exp(ldt_ref[...])
    mag = jnp.exp(lam_re * dt)
    ab_re = mag * jnp.cos(lam_im * dt)
    ab_im = mag * jnp.sin(lam_im * dt)
    nr, ni = ab_re - 1.0, ab_im
    den = lam_re * lam_re + lam_im * lam_im
    f_re = (nr * lam_re + ni * lam_im) / den
    f_im = (ni * lam_re - nr * lam_im) / den
    br, bi = bre_ref[...], bim_ref[...]
    abre_ref[...] = ab_re
    abim_ref[...] = ab_im
    bbre_ref[...] = f_re * br - f_im * bi
    bbim_ref[...] = f_re * bi + f_im * br


def s5_discretize(a_re, a_im, log_dt, b_re_t, b_im_t):
    g, p = a_re.shape
    h = b_re_t.shape[1]
    outs = pl.pallas_call(
        _s5_discretize_kernel,
        out_shape=[jax.ShapeDtypeStruct((g, 1, p), F32), jax.ShapeDtypeStruct((g, 1, p), F32),
                   jax.ShapeDtypeStruct((g, h, p), F32), jax.ShapeDtypeStruct((g, h, p), F32)],
        name="s5_discretize",
    )(a_re.reshape(g, 1, p), a_im.reshape(g, 1, p), log_dt.reshape(g, 1, 1), b_re_t, b_im_t)
    return outs


def _s5_kernel(u_ref, bre_ref, bim_ref, cre_ref, cimn_ref, a_ref, d_ref, wglu_ref, bglu_ref,
               o_ref, xr_ref, xi_ref, st_ref):
    tc = u_ref.shape[0]
    pw = SSM_PACK * SSM_GROUP
    sw = SSM_PACK * SSM_STATE

    @pl.when(pl.program_id(1) == 0)
    def _():
        st_ref[...] = jnp.zeros_like(st_ref)

    u = u_ref[...]
    ub = u.astype(BF16)
    for k in range(N_PACKS):
        uk = ub[:, k * pw:(k + 1) * pw]
        xr_ref[:, k * sw:(k + 1) * sw] = jnp.dot(uk, bre_ref[k], preferred_element_type=F32)
        xi_ref[:, k * sw:(k + 1) * sw] = jnp.dot(uk, bim_ref[k], preferred_element_type=F32)

    def step(t, carry):
        a_re = a_ref[0:1, :]
        a_im = a_ref[1:2, :]
        s_re = st_ref[0:1, :]
        s_im = st_ref[1:2, :]
        n_re = a_re * s_re - a_im * s_im + xr_ref[pl.ds(t, 1), :]
        n_im = a_re * s_im + a_im * s_re + xi_ref[pl.ds(t, 1), :]
        xr_ref[pl.ds(t, 1), :] = n_re
        xi_ref[pl.ds(t, 1), :] = n_im
        st_ref[0:1, :] = n_re
        st_ref[1:2, :] = n_im
        return carry

    lax.fori_loop(0, tc, step, 0)

    ys = []
    for k in range(N_PACKS):
        xr = xr_ref[:, k * sw:(k + 1) * sw].astype(BF16)
        xi = xi_ref[:, k * sw:(k + 1) * sw].astype(BF16)
        ys.append(jnp.dot(xr, cre_ref[k], preferred_element_type=F32)
                  + jnp.dot(xi, cimn_ref[k], preferred_element_type=F32))
    y = jnp.concatenate(ys, axis=-1) + d_ref[...] * u
    y = jax.nn.gelu(y).astype(BF16)
    ab = jnp.dot(y, wglu_ref[...], preferred_element_type=F32) + bglu_ref[...]
    o_ref[...] = (ab[:, :D_SSM] * jax.nn.sigmoid(ab[:, D_SSM:])).astype(o_ref.dtype)


def s5_glu(proj, n_batch, bbd_re, bbd_im, cbd_re, cbd_imn, a_flat, d, w_glu, b_glu, *, tc=256):
    t = proj.shape[0]
    n_chunks = t // n_batch // tc
    c3 = lambda b, c: (0, 0, 0)
    c2 = lambda b, c: (0, 0)
    return pl.pallas_call(
        _s5_kernel,
        grid=(n_batch, n_chunks),
        in_specs=[pl.BlockSpec((tc, D_SSM), lambda b, c: (b * n_chunks + c, 0)),
                  pl.BlockSpec(bbd_re.shape, c3), pl.BlockSpec(bbd_im.shape, c3),
                  pl.BlockSpec(cbd_re.shape, c3), pl.BlockSpec(cbd_imn.shape, c3),
                  pl.BlockSpec(a_flat.shape, c2), pl.BlockSpec(d.shape, c2),
                  pl.BlockSpec(w_glu.shape, c2), pl.BlockSpec(b_glu.shape, c2)],
        out_specs=pl.BlockSpec((tc, D_SSM), lambda b, c: (b * n_chunks + c, 0)),
        out_shape=jax.ShapeDtypeStruct((t, D_SSM), BF16),
        scratch_shapes=[pltpu.VMEM((tc, N_STATE), F32), pltpu.VMEM((tc, N_STATE), F32),
                        pltpu.VMEM((2, N_STATE), F32)],
        compiler_params=_cparams(("arbitrary", "arbitrary")),
        name="s5_glu",
    )(proj, bbd_re, bbd_im, cbd_re, cbd_imn, a_flat, d, w_glu, b_glu)


def _s5_weights(a_re, a_im, log_dt, b_re, b_im, c_re, c_im):
    ab_re, ab_im, bb_re, bb_im = s5_discretize(a_re, a_im, log_dt,
                                               b_re.transpose(0, 2, 1), b_im.transpose(0, 2, 1))
    eye = jnp.eye(SSM_PACK, dtype=F32)

    def pack_b(bb):
        bb = bb.reshape(N_PACKS, SSM_PACK, SSM_GROUP, 1, SSM_STATE)
        m = bb * eye[None, :, None, :, None]
        return m.reshape(N_PACKS, SSM_PACK * SSM_GROUP, SSM_PACK * SSM_STATE).astype(BF16)

    def pack_c(c):
        c = c.reshape(N_PACKS, SSM_PACK, SSM_GROUP, SSM_STATE)
        m = c.transpose(0, 1, 3, 2)[:, :, :, None, :] * eye[None, :, None, :, None]
        return m.reshape(N_PACKS, SSM_PACK * SSM_STATE, SSM_PACK * SSM_GROUP).astype(BF16)

    a_flat = jnp.stack([ab_re.reshape(-1), ab_im.reshape(-1)])
    return pack_b(bb_re), pack_b(bb_im), pack_c(c_re), pack_c(-c_im), a_flat


def _compress_kernel(r_ref, n_ref, pos_ref, w1a_ref, w1b_ref, w2_ref, o_ref):
    r = r_ref[0]
    rows = r.shape[0]
    rowid = lax.broadcasted_iota(jnp.int32, (rows, 1), 0)
    nxt = jnp.where(rowid == rows - 1, n_ref[0, 0:1, :], pltpu.roll(r, rows - 1, 0))
    first = (r + pos_ref[0:1, :]).astype(BF16)
    second = (nxt + pos_ref[1:2, :]).astype(BF16)
    hid = (jnp.dot(first, w1a_ref[...], preferred_element_type=F32)
           + jnp.dot(second, w1b_ref[...], preferred_element_type=F32))
    hid = jax.nn.gelu(hid).astype(BF16)
    o_ref[0] = jnp.dot(hid, w2_ref[...], preferred_element_type=F32).astype(o_ref.dtype)


def compress(kc_rows, pos2, w1a, w1b, w2e):
    b, nc, w = kc_rows.shape
    tr = 128
    sub = 8
    full = lambda i, j: (0, 0)
    nxt_blk = lambda i, j: (i, jnp.minimum((j + 1) * (tr // sub), nc // sub - 1), 0)
    return pl.pallas_call(
        _compress_kernel,
        grid=(b, nc // tr),
        in_specs=[pl.BlockSpec((1, tr, w), lambda i, j: (i, j, 0)),
                  pl.BlockSpec((1, sub, w), nxt_blk),
                  pl.BlockSpec(pos2.shape, full), pl.BlockSpec(w1a.shape, full),
                  pl.BlockSpec(w1b.shape, full), pl.BlockSpec(w2e.shape, full)],
        out_specs=pl.BlockSpec((1, tr, N_KV * HEAD_DIM), lambda i, j: (i, j, 0)),
        out_shape=jax.ShapeDtypeStruct((b, nc, N_KV * HEAD_DIM), BF16),
        compiler_params=_cparams(("parallel", "parallel")),
        name="compress",
    )(kc_rows, kc_rows, pos2, w1a, w1b, w2e)


def _compress_weights(pos, w1, w2):
    eye = jnp.eye(N_KV, dtype=F32)
    w1 = w1.reshape(2, CMP_STRIDE, 1, HEAD_DIM, 1, CMP_HIDDEN)
    w1e = (w1 * eye[None, None, :, None, :, None]).reshape(
        2, CMP_STRIDE * N_KV * HEAD_DIM, N_KV * CMP_HIDDEN).astype(BF16)
    w2e = (w2[None, :, None, :] * eye[:, None, :, None]).reshape(
        N_KV * CMP_HIDDEN, N_KV * HEAD_DIM).astype(BF16)
    pos2 = jnp.broadcast_to(pos.reshape(2, CMP_STRIDE, 1, HEAD_DIM),
                            (2, CMP_STRIDE, N_KV, HEAD_DIM)).reshape(2, -1)
    return pos2, w1e[0], w1e[1], w2e


def _kv_prep_kernel(ks_ref, vs_ref, kw_ref, vw_ref, ka_ref, vst_ref, kwp_ref, vwtp_ref):
    j = pl.program_id(1)
    ts = ks_ref.shape[0]
    n_sel = ka_ref.shape[3] - HEAD_DIM
    blk0 = jnp.maximum(j - 1, 0) * (ts // SEL_BLOCK)
    row_blk = blk0 + lax.broadcasted_iota(jnp.int32, (ts, n_sel), 0) // SEL_BLOCK
    col = lax.broadcasted_iota(jnp.int32, (ts, n_sel), 1)
    onehot = jnp.where(row_blk == col, ONEHOT_BIG, 0.0).astype(BF16)
    ks = ks_ref[...]
    vs_t = vs_ref[...].T
    pad_rows = lax.broadcasted_iota(jnp.int32, (V_ROWS - HEAD_DIM, ts), 0)
    extra = jnp.where(pad_rows == 0, 1.0, 0.0).astype(BF16)
    for g in range(N_KV):
        lo, hi = g * HEAD_DIM, (g + 1) * HEAD_DIM
        ka_ref[0, g, :, 0:n_sel] = onehot
        ka_ref[0, g, :, n_sel:] = ks[:, lo:hi].astype(BF16)
        vst_ref[0, g, 0:HEAD_DIM, :] = vs_t[lo:hi].astype(BF16)
        vst_ref[0, g, HEAD_DIM:, :] = extra

    @pl.when(j == 0)
    def _():
        kwp_ref[...] = jnp.zeros_like(kwp_ref)
        vwtp_ref[...] = jnp.zeros_like(vwtp_ref)

    @pl.when(j > 0)
    def _():
        kw = kw_ref[...]
        vw_t = vw_ref[...].T
        for g in range(N_KV):
            lo, hi = g * HEAD_DIM, (g + 1) * HEAD_DIM
            kwp_ref[0, g] = kw[:, lo:hi].astype(BF16)
            vwtp_ref[0, g] = vw_t[lo:hi].astype(BF16)


def kv_prep(proj, n_batch, seq):
    ts = WINDOW
    nblk = seq // ts
    n_sel = seq // SEL_BLOCK
    kvw = N_KV * HEAD_DIM
    src = lambda col: pl.BlockSpec((ts, kvw), lambda b, j: (b * nblk + jnp.maximum(j - 1, 0), col // kvw))
    same = lambda b, j: (b, 0, jnp.maximum(j - 1, 0), 0)
    same_t = lambda b, j: (b, 0, 0, jnp.maximum(j - 1, 0))
    return pl.pallas_call(
        _kv_prep_kernel,
        grid=(n_batch, nblk + 1),
        in_specs=[src(COL_KS), src(COL_VS), src(COL_KW), src(COL_VW)],
        out_specs=[pl.BlockSpec((1, N_KV, ts, n_sel + HEAD_DIM), same),
                   pl.BlockSpec((1, N_KV, V_ROWS, ts), same_t),
                   pl.BlockSpec((1, N_KV, ts, HEAD_DIM), lambda b, j: (b, 0, j, 0)),
                   pl.BlockSpec((1, N_KV, HEAD_DIM, ts), lambda b, j: (b, 0, 0, j))],
        out_shape=[jax.ShapeDtypeStruct((n_batch, N_KV, seq, n_sel + HEAD_DIM), BF16),
                   jax.ShapeDtypeStruct((n_batch, N_KV, V_ROWS, seq), BF16),
                   jax.ShapeDtypeStruct((n_batch, N_KV, seq + WINDOW, HEAD_DIM), BF16),
                   jax.ShapeDtypeStruct((n_batch, N_KV, HEAD_DIM, seq + WINDOW), BF16)],
        compiler_params=_cparams(("parallel", "arbitrary")),
        name="kv_prep",
    )(proj, proj, proj, proj)


def _softmax_cols(s):
    m = jnp.max(s, axis=0, keepdims=True)
    m = jnp.where(m < 0.5 * NEG, 0.0, m)
    e = jnp.exp(s - m)
    return e * (1.0 / jnp.maximum(jnp.sum(e, axis=0, keepdims=True), 1e-30))


def _cmp_branch(a, qt, gate, k_ref, vt_ref, tbl_ref, smat_ref, st_ref, rank_ref):
    n_row_tiles = k_ref.shape[2] // CMP_LANE
    ap = a // (FAR_DIST // Q_TILE)
    s = jnp.dot(k_ref[0, 0], qt, preferred_element_type=F32)
    tbl = tbl_ref[0, 0]
    pieces = []
    for c in range(n_row_tiles):
        st = s[c * CMP_LANE:(c + 1) * CMP_LANE]
        cur = st + tbl[CMP_LANE:]
        prev = st + tbl[:CMP_LANE]
        pieces.append(jnp.where(c == ap, cur, jnp.where(c == ap - 1, prev, jnp.where(c < ap, st, NEG))))
    p = _softmax_cols(jnp.concatenate(pieces, axis=0))
    o = gate * jnp.dot(vt_ref[0, 0], p.astype(BF16), preferred_element_type=F32)

    imp = (p[:, 0:Q_TILE] + p[:, Q_TILE:2 * Q_TILE] + p[:, 2 * Q_TILE:3 * Q_TILE]
           + p[:, 3 * Q_TILE:4 * Q_TILE])
    hi = imp.astype(BF16)
    r1 = imp - hi.astype(F32)
    mid = r1.astype(BF16)
    lo = (r1 - mid.astype(F32)).astype(BF16)
    smat = smat_ref[...]
    st = (jnp.dot(smat, hi, preferred_element_type=F32) + jnp.dot(smat, mid, preferred_element_type=F32)
          + jnp.dot(smat, lo, preferred_element_type=F32))

    t = a * Q_TILE + lax.broadcasted_iota(jnp.int32, st.shape, 1)
    jj = lax.broadcasted_iota(jnp.int32, st.shape, 0)
    cur_blk = t // SEL_BLOCK
    valid = jj * SEL_BLOCK <= t
    forced = (jj == 0) | (jj == cur_blk) | (jj == cur_blk - 1)
    st = jnp.where(valid, jnp.where(forced, FORCED_SCORE, st), -1.0)
    st_ref[...] = st
    rank_ref[...] = jnp.zeros_like(rank_ref)
    sub = 8
    chunk = 16
    n_sel = st.shape[0]
    last_valid = (a * Q_TILE + Q_TILE - 1) // SEL_BLOCK
    for q0 in range(0, n_sel, chunk):
        @pl.when(q0 <= last_valid)
        def _(q0=q0):
            for v0 in range(0, n_sel, chunk):
                @pl.when(v0 <= last_valid)
                def _(v0=v0):
                    for v in range(v0 // sub, min(v0 + chunk, n_sel) // sub):
                        sv = st_ref[v * sub:(v + 1) * sub, :]
                        jv = jj[v * sub:(v + 1) * sub]
                        cnt = rank_ref[v * sub:(v + 1) * sub, :]
                        for jp in range(q0, min(q0 + chunk, n_sel)):
                            row = st_ref[jp:jp + 1, :]
                            if jp < v * sub:
                                ahead = row >= sv
                            elif jp >= (v + 1) * sub:
                                ahead = row > sv
                            else:
                                ahead = (row > sv) | ((row == sv) & (jv > jp))
                            cnt = cnt + jnp.where(ahead, 1.0, 0.0)
                        rank_ref[v * sub:(v + 1) * sub, :] = cnt

    keep = (rank_ref[...] < SEL_TOP) & (st >= 0.0)
    return o, (keep.astype(F32) - 1.0).astype(BF16)


def _sel_branch(a, gate, ka_ref, vt_ref, tbl_ref, qa_ref, m_ref, acc_ref):
    tk = SEL_KEY_TILE
    acc_ref[...] = jnp.zeros_like(acc_ref)
    n_tiles = a // (tk // Q_TILE) + 1
    n_pairs = (n_tiles + 1) // 2
    far_pairs = jnp.maximum(0, (a - (SEL_NEAR_TILES - 5)) // 4)

    def key_off(c):
        return pl.multiple_of(jnp.minimum(c, n_tiles - 1) * tk, tk)

    def attend(p, biased):
        m_old = m_ref[...]
        acc_old = acc_ref[...]

        def score(h):
            c = 2 * p + h
            s = jnp.dot(ka_ref[0, 0, pl.ds(key_off(c), tk), :], qa_ref[...], preferred_element_type=F32)
            if biased:
                s = s + tbl_ref[0, jnp.where(c < n_tiles, a - 2 * c, SEL_NEAR_TILES - 1)]
            return s

        def weighted(h, s, m):
            return jnp.dot(vt_ref[0, 0, :, pl.ds(key_off(2 * p + h), tk)], jnp.exp2(s - m).astype(BF16),
                           preferred_element_type=F32)

        ss = [score(h) for h in range(2)]
        tile_max = [jnp.max(s, axis=0, keepdims=True) for s in ss]
        pv = [weighted(h, ss[h], m_old) for h in range(2)]
        m_new = jnp.maximum(m_old, jnp.maximum(tile_max[0], tile_max[1]))
        acc_ref[...] = (acc_old + pv[0] + pv[1]) * jnp.exp2(m_old - m_new)
        m_ref[...] = m_new

        @pl.when(jnp.max(m_new - m_old) > SAFE_LOG2_RANGE)
        def _():
            m_mid = jnp.maximum(m_old, tile_max[0])
            pv0 = weighted(0, score(0), m_mid)
            pv1 = weighted(1, score(1), m_new)
            acc_ref[...] = (jnp.exp2(m_old - m_new) * acc_old + jnp.exp2(m_mid - m_new) * pv0) + pv1

    def body(biased):
        def step(p, carry):
            attend(p, biased)
            return carry
        return step

    sub = 8
    m_ref[...] = jnp.dot(ka_ref[0, 0, 0:sub, :], qa_ref[...], preferred_element_type=F32)[0:1, :]
    lax.fori_loop(0, far_pairs, body(False), 0)
    lax.fori_loop(far_pairs, n_pairs, body(True), 0)
    acc = acc_ref[...]
    return acc[0:HEAD_DIM] * (gate / acc[HEAD_DIM:HEAD_DIM + 1])


def _win_branch(a, qt, gate, k_ref, vt_ref, tbl_ref):
    span = WINDOW + Q_TILE
    off = pl.multiple_of(a * Q_TILE, Q_TILE)
    s = jnp.dot(k_ref[0, 0, pl.ds(off, span), :], qt, preferred_element_type=F32)
    s = s + tbl_ref[0]
    kj = lax.broadcasted_iota(jnp.int32, s.shape, 0)
    s = jnp.where(kj >= WINDOW - a * Q_TILE, s, NEG)
    p = _softmax_cols(s)
    return gate * jnp.dot(vt_ref[0, 0, :, pl.ds(off, span)], p.astype(BF16), preferred_element_type=F32)


def _nsa_kernel(q_ref, g_ref, kc_ref, vtc_ref, tblc_ref, smat_ref, ka_ref, vst_ref, tbls_ref,
                kw_ref, vwt_ref, tblw_ref, o_ref, gt_ref, st_ref, rank_ref, qa_ref, m_ref, acc_ref):
    g = pl.program_id(0)
    a = pl.program_id(2)
    n_sel = smat_ref.shape[0]
    half = 2 * HEAD_DIM
    q = q_ref[...] * (HEAD_DIM ** -0.5)
    t0 = q[:, :half].T
    t1 = q[:, half:].T
    qt32 = jnp.concatenate([t0[:HEAD_DIM], t0[HEAD_DIM:], t1[:HEAD_DIM], t1[HEAD_DIM:]], axis=1)
    qt = qt32.astype(BF16)
    gt_ref[...] = g_ref[:, :Q_TILE].T

    def gate(branch):
        rows = [gt_ref[pl.ds((g * GQA_R + r) * 3 + branch, 1), :] for r in range(GQA_R)]
        return jax.nn.sigmoid(jnp.concatenate(rows, axis=1))

    o, sel = _cmp_branch(a, qt, gate(0), kc_ref, vtc_ref, tblc_ref, smat_ref, st_ref, rank_ref)
    for r in range(GQA_R):
        qa_ref[0:n_sel, r * Q_TILE:(r + 1) * Q_TILE] = sel
    qa_ref[n_sel:, :] = (qt32 * LOG2E).astype(BF16)
    o = o + _sel_branch(a, gate(1), ka_ref, vst_ref, tbls_ref, qa_ref, m_ref, acc_ref)
    o = o + _win_branch(a, qt, gate(2), kw_ref, vwt_ref, tblw_ref)
    u0 = jnp.concatenate([o[:, 0:Q_TILE], o[:, Q_TILE:2 * Q_TILE]], axis=0).T
    u1 = jnp.concatenate([o[:, 2 * Q_TILE:3 * Q_TILE], o[:, 3 * Q_TILE:]], axis=0).T
    o_ref[...] = jnp.concatenate([u0, u1], axis=1).astype(o_ref.dtype)


def nsa_attention(proj, n_batch, seq, k_cmp, vt_cmp, ka, vst, kwp, vwtp, tables, smat):
    tbl_c, tbl_s, tbl_w = tables
    n_a = seq // Q_TILE
    nc = k_cmp.shape[2]
    n_sel = smat.shape[0]
    dh = HEAD_DIM
    gw = GQA_R * HEAD_DIM
    tps = FAR_DIST // Q_TILE
    per_bg = lambda gi, bi, ai: (bi, gi, 0, 0)
    return pl.pallas_call(
        _nsa_kernel,
        grid=(N_KV, n_batch, n_a),
        in_specs=[pl.BlockSpec((Q_TILE, gw), lambda gi, bi, ai: (bi * n_a + ai, COL_Q // gw + gi)),
                  pl.BlockSpec((Q_TILE, gw), lambda gi, bi, ai: (bi * n_a + ai, COL_GATE // gw)),
                  pl.BlockSpec((1, 1, nc, dh), per_bg),
                  pl.BlockSpec((1, 1, dh, nc), per_bg),
                  pl.BlockSpec((1, 1, 2 * CMP_LANE, ROWS), lambda gi, bi, ai: (gi, ai % tps, 0, 0)),
                  pl.BlockSpec(smat.shape, lambda gi, bi, ai: (0, 0)),
                  pl.BlockSpec((1, 1, seq, n_sel + dh), per_bg),
                  pl.BlockSpec((1, 1, V_ROWS, seq), per_bg),
                  pl.BlockSpec((1, SEL_NEAR_TILES, SEL_KEY_TILE, ROWS), lambda gi, bi, ai: (gi, 0, 0, 0)),
                  pl.BlockSpec((1, 1, seq + WINDOW, dh), per_bg),
                  pl.BlockSpec((1, 1, dh, seq + WINDOW), per_bg),
                  pl.BlockSpec((1, WINDOW + Q_TILE, ROWS), lambda gi, bi, ai: (gi, 0, 0))],
        out_specs=pl.BlockSpec((Q_TILE, gw), lambda gi, bi, ai: (bi * n_a + ai, gi)),
        out_shape=jax.ShapeDtypeStruct((n_batch * seq, N_HEADS * HEAD_DIM), BF16),
        scratch_shapes=[pltpu.VMEM((Q_TILE, Q_TILE), F32),
                        pltpu.VMEM((n_sel, Q_TILE), F32), pltpu.VMEM((n_sel, Q_TILE), F32),
                        pltpu.VMEM((n_sel + dh, ROWS), BF16), pltpu.VMEM((1, ROWS), F32),
                        pltpu.VMEM((V_ROWS, ROWS), F32)],
        compiler_params=_cparams(("parallel", "parallel", "arbitrary")),
        name="nsa_attention",
    )(proj, proj, k_cmp, vt_cmp, tbl_c, smat, ka, vst, tbl_s, kwp, vwtp, tbl_w)


def _rel_bucket(dist):
    n = jnp.maximum(dist, 0)
    nf = jnp.maximum(n, 1).astype(F32)
    large = RPB_MAX_EXACT + (jnp.log(nf / RPB_MAX_EXACT) / math.log(RPB_MAX_DIST / RPB_MAX_EXACT)
                             * (N_BUCKETS - RPB_MAX_EXACT)).astype(jnp.int32)
    large = jnp.minimum(large, N_BUCKETS - 1)
    return jnp.where(n < RPB_MAX_EXACT, n, large)


def _toeplitz(e, n_rows):
    w = e.shape[-1]
    lead = e.shape[:-1]
    flat = jnp.broadcast_to(e[..., None, :], lead + (n_rows, w)).reshape(lead + (n_rows * w,))
    return flat[..., :n_rows * (w - 1)].reshape(lead + (n_rows, w - 1))


def _bias_tables(rpb_table):
    tab = rpb_table.astype(F32)
    by_dist = tab[_rel_bucket(jnp.arange(FAR_DIST))].T
    rel = by_dist - tab[N_BUCKETS - 1][:, None]
    neg = lambda n: jnp.full((N_HEADS, n), NEG, F32)
    zero = lambda n: jnp.zeros((N_HEADS, n), F32)

    def to_rows(m):
        m = m.reshape((N_KV, GQA_R) + m.shape[1:])
        m = jnp.moveaxis(m, 1, -2)
        return m.reshape(m.shape[:-2] + (ROWS,))

    tk = SEL_KEY_TILE
    ext = jnp.concatenate([neg(tk), rel], axis=1)
    win = jnp.stack([ext[:, Q_TILE * k + 1:Q_TILE * k + 1 + tk + Q_TILE] for k in range(SEL_NEAR_TILES - 1)]
                    + [neg(tk + Q_TILE)], axis=1)
    tbl_s = to_rows(_toeplitz(win, tk)[..., tk - 1:tk - 1 + Q_TILE]) * LOG2E

    span = WINDOW + Q_TILE
    ext = jnp.concatenate([neg(Q_TILE - 1), by_dist[:, :WINDOW], neg(Q_TILE + 1)], axis=1)
    tbl_w = to_rows(_toeplitz(ext, span)[..., span - 1:span - 1 + Q_TILE])

    n_m = 2 * CMP_LANE
    n_tau = FAR_DIST // CMP_STRIDE
    base = CMP_STRIDE * (n_m - 1) - (FAR_DIST - CMP_BLOCK + 1)
    length = CMP_STRIDE * (n_m + n_tau)
    ext = jnp.concatenate([neg(base), rel, zero(length - base - FAR_DIST)], axis=1)
    ev = ext.reshape(N_HEADS, n_m + n_tau, CMP_STRIDE).transpose(0, 2, 1)
    tz = _toeplitz(ev, n_m)[..., n_m - 1:n_m - 1 + n_tau]
    tz = tz.transpose(0, 2, 3, 1).reshape(N_HEADS, n_m, FAR_DIST // Q_TILE, Q_TILE)
    tbl_c = to_rows(tz.transpose(0, 2, 1, 3))
    return tbl_c, tbl_s, tbl_w


def _slc_matrix(n_cmp_pad, n_sel):
    sel_ratio, cmp_ratio = SEL_BLOCK // CMP_STRIDE, CMP_BLOCK // CMP_STRIDE
    mat = np.zeros((n_sel, n_cmp_pad), np.float32)
    for j in range(n_sel):
        for mm in range(sel_ratio):
            for nn in range(cmp_ratio):
                idx = sel_ratio * j - mm - nn
                if 0 <= idx < n_cmp_pad - 1:
                    mat[j, idx] += 1.0
    return jnp.asarray(mat, BF16)


def nsa_mixer(proj, n_batch, seq, cmp_k_w, cmp_v_w, tables, smat):
    kv = N_KV * HEAD_DIM
    rows16 = lambda col: proj[:, col:col + kv].reshape(n_batch, seq // CMP_STRIDE, CMP_STRIDE * kv)
    k_cmp = compress(rows16(COL_KC), *cmp_k_w)
    v_cmp = compress(rows16(COL_VC), *cmp_v_w)
    k_cmp = k_cmp.reshape(n_batch, -1, N_KV, HEAD_DIM).transpose(0, 2, 1, 3)
    vt_cmp = v_cmp.reshape(n_batch, -1, N_KV, HEAD_DIM).transpose(0, 2, 3, 1)
    ka, vst, kwp, vwtp = kv_prep(proj, n_batch, seq)
    return nsa_attention(proj, n_batch, seq, k_cmp, vt_cmp, ka, vst, kwp, vwtp, tables, smat)


def kernel(x, rpb_table, attn_norm, ffn_norm, final_norm, w_in, ssm_a_re, ssm_a_im, ssm_log_dt,
           ssm_b_re, ssm_b_im, ssm_c_re, ssm_c_im, ssm_d, w_glu, b_glu, cmp_pos_k, cmp_w1_k, cmp_w2_k,
           cmp_pos_v, cmp_w1_v, cmp_w2_v, w_out, w_ffn_gate, w_ffn_up, w_ffn_down):
    n_batch, seq, d_model = x.shape
    depth = w_in.shape[0]
    tables = _bias_tables(rpb_table)
    smat = _slc_matrix(seq // CMP_STRIDE, seq // SEL_BLOCK)

    xf = x.reshape(n_batch * seq, d_model)
    for l in range(depth):
        w_in_l = jnp.pad(w_in[l], ((0, 0), (0, IN_COLS_PAD - IN_COLS))).astype(BF16)
        proj = norm_matmul(xf, attn_norm[l].reshape(1, -1), w_in_l)
        s5w = _s5_weights(ssm_a_re[l], ssm_a_im[l], ssm_log_dt[l], ssm_b_re[l], ssm_b_im[l],
                          ssm_c_re[l], ssm_c_im[l])
        y_ssm = s5_glu(proj, n_batch, *s5w, ssm_d[l].reshape(1, -1), w_glu[l].astype(BF16),
                       b_glu[l].reshape(1, -1))
        y_nsa = nsa_mixer(proj, n_batch, seq,
                          _compress_weights(cmp_pos_k[l], cmp_w1_k[l], cmp_w2_k[l]),
                          _compress_weights(cmp_pos_v[l], cmp_w1_v[l], cmp_w2_v[l]),
                          tables, smat)
        w_out_l = w_out[l].astype(BF16)
        xf = out_proj(xf, y_ssm, y_nsa, w_out_l[:D_SSM], w_out_l[D_SSM:])
        xf = ffn(xf, ffn_norm[l].reshape(1, -1), w_ffn_gate[l].astype(BF16),
                 w_ffn_up[l].astype(BF16), w_ffn_down[l].astype(BF16))
    out = final_rms_norm(xf, final_norm.reshape(1, -1))
    return out.reshape(n_batch, seq, d_model)
```

```python
import math

import jax
import jax.numpy as jnp
import numpy as np
from jax import lax
from jax.experimental import pallas as pl
from jax.experimental.pallas import tpu as pltpu

F32 = jnp.float32
BF16 = jnp.bfloat16

D_MODEL = 2048
DEPTH = 4
D_SSM = 1024
SSM_GROUP = 16
N_SSM_GROUPS = 64
SSM_STATE = 64
N_STATE = N_SSM_GROUPS * SSM_STATE
SSM_PACK = 16
N_PACKS = N_SSM_GROUPS // SSM_PACK
N_HEADS = 16
N_KV = 4
GQA_R = 4
HEAD_DIM = 64
CMP_BLOCK = 32
CMP_STRIDE = 16
CMP_HIDDEN = 128
SEL_BLOCK = 64
SEL_TOP = 16
WINDOW = 512
Q_TILE = 128
ROWS = GQA_R * Q_TILE
FORCED_SCORE = 1e4
N_BUCKETS = 32
RPB_MAX_EXACT = 16
RPB_MAX_DIST = 1024
D_FF = 5632
NORM_EPS = 1e-6
IN_COLS = 3632
IN_COLS_PAD = 3840
COL_U, COL_Q, COL_KC, COL_VC, COL_KS, COL_VS, COL_KW, COL_VW, COL_GATE = (
    0, 1024, 2048, 2304, 2560, 2816, 3072, 3328, 3584)
NEG = -1e30
ONEHOT_BIG = 1e30
FAR_DIST = 2048
SEL_KEY_TILE = 256
SEL_NEAR_TILES = 12
LOG2E = math.log2(math.e)
SAFE_LOG2_RANGE = 64.0
CMP_LANE = 128
V_ROWS = 80
VMEM_LIMIT = 56 * 1024 * 1024


def _cparams(sem):
    return pltpu.CompilerParams(dimension_semantics=sem, vmem_limit_bytes=VMEM_LIMIT)


def _rms(x, g):
    ms = jnp.mean(x * x, axis=-1, keepdims=True)
    return x * lax.rsqrt(ms + NORM_EPS) * g


def _norm_matmul_kernel(x_ref, g_ref, w_ref, o_ref, h_ref):
    @pl.when(pl.program_id(1) == 0)
    def _():
        h_ref[...] = _rms(x_ref[...], g_ref[...]).astype(BF16)

    o_ref[...] = jnp.dot(h_ref[...], w_ref[...], preferred_element_type=F32)


def norm_matmul(x, g, w, *, tm=1024, tn=768):
    m, k = x.shape
    n = w.shape[1]
    return pl.pallas_call(
        _norm_matmul_kernel,
        grid=(m // tm, n // tn),
        in_specs=[pl.BlockSpec((tm, k), lambda i, j: (i, 0)),
                  pl.BlockSpec((1, k), lambda i, j: (0, 0)),
                  pl.BlockSpec((k, tn), lambda i, j: (0, j))],
        out_specs=pl.BlockSpec((tm, tn), lambda i, j: (i, j)),
        out_shape=jax.ShapeDtypeStruct((m, n), F32),
        scratch_shapes=[pltpu.VMEM((tm, k), BF16)],
        compiler_params=_cparams(("parallel", "arbitrary")),
        name="norm_matmul",
    )(x, g, w)


def _out_proj_kernel(x_ref, ys_ref, yn_ref, w1_ref, w2_ref, o_ref):
    o_ref[...] = (x_ref[...]
                  + jnp.dot(ys_ref[...], w1_ref[...], preferred_element_type=F32)
                  + jnp.dot(yn_ref[...], w2_ref[...], preferred_element_type=F32))


def out_proj(x, y_ssm, y_nsa, w_ssm, w_nsa, *, tm=512):
    m, d = x.shape
    k = y_ssm.shape[1]
    row = lambda i: (i, 0)
    full = lambda i: (0, 0)
    return pl.pallas_call(
        _out_proj_kernel,
        grid=(m // tm,),
        in_specs=[pl.BlockSpec((tm, d), row), pl.BlockSpec((tm, k), row), pl.BlockSpec((tm, k), row),
                  pl.BlockSpec((k, d), full), pl.BlockSpec((k, d), full)],
        out_specs=pl.BlockSpec((tm, d), row),
        out_shape=jax.ShapeDtypeStruct((m, d), F32),
        compiler_params=_cparams(("parallel",)),
        name="out_proj",
    )(x, y_ssm, y_nsa, w_ssm, w_nsa)


def _ffn_kernel(x_ref, g_ref, wg_ref, wu_ref, wd_ref, o_ref, h_ref):
    f = pl.program_id(1)

    @pl.when(f == 0)
    def _():
        x = x_ref[...]
        h_ref[...] = _rms(x, g_ref[...]).astype(BF16)
        o_ref[...] = x

    h = h_ref[...]
    gate = jnp.dot(h, wg_ref[...], preferred_element_type=F32)
    up = jnp.dot(h, wu_ref[...], preferred_element_type=F32)
    act = (jax.nn.silu(gate) * up).astype(BF16)
    o_ref[...] += jnp.dot(act, wd_ref[...], preferred_element_type=F32)


def ffn(x, g, w_gate, w_up, w_down, *, tm=1024, tf=256):
    m, d = x.shape
    dff = w_gate.shape[1]
    return pl.pallas_call(
        _ffn_kernel,
        grid=(m // tm, dff // tf),
        in_specs=[pl.BlockSpec((tm, d), lambda i, f: (i, 0)),
                  pl.BlockSpec((1, d), lambda i, f: (0, 0)),
                  pl.BlockSpec((d, tf), lambda i, f: (0, f)),
                  pl.BlockSpec((d, tf), lambda i, f: (0, f)),
                  pl.BlockSpec((tf, d), lambda i, f: (f, 0))],
        out_specs=pl.BlockSpec((tm, d), lambda i, f: (i, 0)),
        out_shape=jax.ShapeDtypeStruct((m, d), F32),
        scratch_shapes=[pltpu.VMEM((tm, d), BF16)],
        compiler_params=_cparams(("parallel", "arbitrary")),
        name="ffn",
    )(x, g, w_gate, w_up, w_down)


def _final_norm_kernel(x_ref, g_ref, o_ref):
    o_ref[...] = _rms(x_ref[...], g_ref[...])


def final_rms_norm(x, g, *, tm=512):
    m, d = x.shape
    return pl.pallas_call(
        _final_norm_kernel,
        grid=(m // tm,),
        in_specs=[pl.BlockSpec((tm, d), lambda i: (i, 0)), pl.BlockSpec((1, d), lambda i: (0, 0))],
        out_specs=pl.BlockSpec((tm, d), lambda i: (i, 0)),
        out_shape=jax.ShapeDtypeStruct((m, d), F32),
        compiler_params=_cparams(("parallel",)),
        name="final_norm",
    )(x, g)


def _s5_discretize_kernel(are_ref, aim_ref, ldt_ref, bre_ref, bim_ref,
                          abre_ref, abim_ref, bbre_ref, bbim_ref):
    lam_re = jnp.minimum(are_ref[...], -1e-4)
    lam_im = aim_ref[...]
    dt = jnp.exp(ldt_ref[...])
    mag = jnp.exp(lam_re * dt)
    ab_re = mag * jnp.cos(lam_im * dt)
    ab_im = mag * jnp.sin(lam_im * dt)
    nr, ni = ab_re - 1.0, ab_im
    den = lam_re * lam_re + lam_im * lam_im
    f_re = (nr * lam_re + ni * lam_im) / den
    f_im = (ni * lam_re - nr * lam_im) / den
    br, bi = bre_ref[...], bim_ref[...]
    abre_ref[...] = ab_re
    abim_ref[...] = ab_im
    bbre_ref[...] = f_re * br - f_im * bi
    bbim_ref[...] = f_re * bi + f_im * br


def s5_discretize(a_re, a_im, log_dt, b_re_t, b_im_t):
    g, p = a_re.shape
    h = b_re_t.shape[1]
    outs = pl.pallas_call(
        _s5_discretize_kernel,
        out_shape=[jax.ShapeDtypeStruct((g, 1, p), F32), jax.ShapeDtypeStruct((g, 1, p), F32),
                   jax.ShapeDtypeStruct((g, h, p), F32), jax.ShapeDtypeStruct((g, h, p), F32)],
        name="s5_discretize",
    )(a_re.reshape(g, 1, p), a_im.reshape(g, 1, p), log_dt.reshape(g, 1, 1), b_re_t, b_im_t)
    return outs


def _s5_kernel(u_ref, bre_ref, bim_ref, cre_ref, cimn_ref, a_ref, d_ref, wglu_ref, bglu_ref,
               o_ref, xr_ref, xi_ref, st_ref):
    tc = u_ref.shape[0]
    pw = SSM_PACK * SSM_GROUP
    sw = SSM_PACK * SSM_STATE

    @pl.when(pl.program_id(1) == 0)
    def _():
        st_ref[...] = jnp.zeros_like(st_ref)

    u = u_ref[...]
    ub = u.astype(BF16)
    for k in range(N_PACKS):
        uk = ub[:, k * pw:(k + 1) * pw]
        xr_ref[:, k * sw:(k + 1) * sw] = jnp.dot(uk, bre_ref[k], preferred_element_type=F32)
        xi_ref[:, k * sw:(k + 1) * sw] = jnp.dot(uk, bim_ref[k], preferred_element_type=F32)

    a_re = a_ref[0:1, :]
    a_im = a_ref[1:2, :]

    def step(t, state):
        s_re, s_im = state
        n_re = a_re * s_re - a_im * s_im + xr_ref[pl.ds(t, 1), :]
        n_im = a_re * s_im + a_im * s_re + xi_ref[pl.ds(t, 1), :]
        xr_ref[pl.ds(t, 1), :] = n_re
        xi_ref[pl.ds(t, 1), :] = n_im
        return n_re, n_im

    s_re, s_im = lax.fori_loop(0, tc, step, (st_ref[0:1, :], st_ref[1:2, :]), unroll=4)
    st_ref[0:1, :] = s_re
    st_ref[1:2, :] = s_im

    ys = []
    for k in range(N_PACKS):
        xr = xr_ref[:, k * sw:(k + 1) * sw].astype(BF16)
        xi = xi_ref[:, k * sw:(k + 1) * sw].astype(BF16)
        ys.append(jnp.dot(xr, cre_ref[k], preferred_element_type=F32)
                  + jnp.dot(xi, cimn_ref[k], preferred_element_type=F32))
    y = jnp.concatenate(ys, axis=-1) + d_ref[...] * u
    y = jax.nn.gelu(y).astype(BF16)
    ab = jnp.dot(y, wglu_ref[...], preferred_element_type=F32) + bglu_ref[...]
    o_ref[...] = (ab[:, :D_SSM] * jax.nn.sigmoid(ab[:, D_SSM:])).astype(o_ref.dtype)


def s5_glu(proj, n_batch, bbd_re, bbd_im, cbd_re, cbd_imn, a_flat, d, w_glu, b_glu, *, tc=256):
    t = proj.shape[0]
    n_chunks = t // n_batch // tc
    c3 = lambda b, c: (0, 0, 0)
    c2 = lambda b, c: (0, 0)
    return pl.pallas_call(
        _s5_kernel,
        grid=(n_batch, n_chunks),
        in_specs=[pl.BlockSpec((tc, D_SSM), lambda b, c: (b * n_chunks + c, 0)),
                  pl.BlockSpec(bbd_re.shape, c3), pl.BlockSpec(bbd_im.shape, c3),
                  pl.BlockSpec(cbd_re.shape, c3), pl.BlockSpec(cbd_imn.shape, c3),
                  pl.BlockSpec(a_flat.shape, c2), pl.BlockSpec(d.shape, c2),
                  pl.BlockSpec(w_glu.shape, c2), pl.BlockSpec(b_glu.shape, c2)],
        out_specs=pl.BlockSpec((tc, D_SSM), lambda b, c: (b * n_chunks + c, 0)),
        out_shape=jax.ShapeDtypeStruct((t, D_SSM), BF16),
        scratch_shapes=[pltpu.VMEM((tc, N_STATE), F32), pltpu.VMEM((tc, N_STATE), F32),
                        pltpu.VMEM((2, N_STATE), F32)],
        compiler_params=_cparams(("arbitrary", "arbitrary")),
        name="s5_glu",
    )(proj, bbd_re, bbd_im, cbd_re, cbd_imn, a_flat, d, w_glu, b_glu)


def _s5_weights(a_re, a_im, log_dt, b_re, b_im, c_re, c_im):
    ab_re, ab_im, bb_re, bb_im = s5_discretize(a_re, a_im, log_dt,
                                               b_re.transpose(0, 2, 1), b_im.transpose(0, 2, 1))
    eye = jnp.eye(SSM_PACK, dtype=F32)

    def pack_b(bb):
        bb = bb.reshape(N_PACKS, SSM_PACK, SSM_GROUP, 1, SSM_STATE)
        m = bb * eye[None, :, None, :, None]
        return m.reshape(N_PACKS, SSM_PACK * SSM_GROUP, SSM_PACK * SSM_STATE).astype(BF16)

    def pack_c(c):
        c = c.reshape(N_PACKS, SSM_PACK, SSM_GROUP, SSM_STATE)
        m = c.transpose(0, 1, 3, 2)[:, :, :, None, :] * eye[None, :, None, :, None]
        return m.reshape(N_PACKS, SSM_PACK * SSM_STATE, SSM_PACK * SSM_GROUP).astype(BF16)

    a_flat = jnp.stack([ab_re.reshape(-1), ab_im.reshape(-1)])
    return pack_b(bb_re), pack_b(bb_im), pack_c(c_re), pack_c(-c_im), a_flat


def _compress_kernel(r_ref, n_ref, pos_ref, w1a_ref, w1b_ref, w2_ref, o_ref):
    r = r_ref[0]
    rows = r.shape[0]
    rowid = lax.broadcasted_iota(jnp.int32, (rows, 1), 0)
    nxt = jnp.where(rowid == rows - 1, n_ref[0, 0:1, :], pltpu.roll(r, rows - 1, 0))
    first = (r + pos_ref[0:1, :]).astype(BF16)
    second = (nxt + pos_ref[1:2, :]).astype(BF16)
    hid = (jnp.dot(first, w1a_ref[...], preferred_element_type=F32)
           + jnp.dot(second, w1b_ref[...], preferred_element_type=F32))
    hid = jax.nn.gelu(hid).astype(BF16)
    o_ref[0] = jnp.dot(hid, w2_ref[...], preferred_element_type=F32).astype(o_ref.dtype)


def compress(kc_rows, pos2, w1a, w1b, w2e):
    b, nc, w = kc_rows.shape
    tr = 128
    sub = 8
    full = lambda i, j: (0, 0)
    nxt_blk = lambda i, j: (i, jnp.minimum((j + 1) * (tr // sub), nc // sub - 1), 0)
    return pl.pallas_call(
        _compress_kernel,
        grid=(b, nc // tr),
        in_specs=[pl.BlockSpec((1, tr, w), lambda i, j: (i, j, 0)),
                  pl.BlockSpec((1, sub, w), nxt_blk),
                  pl.BlockSpec(pos2.shape, full), pl.BlockSpec(w1a.shape, full),
                  pl.BlockSpec(w1b.shape, full), pl.BlockSpec(w2e.shape, full)],
        out_specs=pl.BlockSpec((1, tr, N_KV * HEAD_DIM), lambda i, j: (i, j, 0)),
        out_shape=jax.ShapeDtypeStruct((b, nc, N_KV * HEAD_DIM), BF16),
        compiler_params=_cparams(("parallel", "parallel")),
        name="compress",
    )(kc_rows, kc_rows, pos2, w1a, w1b, w2e)


def _compress_weights(pos, w1, w2):
    eye = jnp.eye(N_KV, dtype=F32)
    w1 = w1.reshape(2, CMP_STRIDE, 1, HEAD_DIM, 1, CMP_HIDDEN)
    w1e = (w1 * eye[None, None, :, None, :, None]).reshape(
        2, CMP_STRIDE * N_KV * HEAD_DIM, N_KV * CMP_HIDDEN).astype(BF16)
    w2e = (w2[None, :, None, :] * eye[:, None, :, None]).reshape(
        N_KV * CMP_HIDDEN, N_KV * HEAD_DIM).astype(BF16)
    pos2 = jnp.broadcast_to(pos.reshape(2, CMP_STRIDE, 1, HEAD_DIM),
                            (2, CMP_STRIDE, N_KV, HEAD_DIM)).reshape(2, -1)
    return pos2, w1e[0], w1e[1], w2e


def _kv_prep_kernel(ks_ref, vs_ref, kw_ref, vw_ref, ka_ref, vst_ref, kwp_ref, vwtp_ref):
    j = pl.program_id(1)
    ts = ks_ref.shape[0]
    n_sel = ka_ref.shape[3] - HEAD_DIM
    blk0 = jnp.maximum(j - 1, 0) * (ts // SEL_BLOCK)
    row_blk = blk0 + lax.broadcasted_iota(jnp.int32, (ts, n_sel), 0) // SEL_BLOCK
    col = lax.broadcasted_iota(jnp.int32, (ts, n_sel), 1)
    onehot = jnp.where(row_blk == col, ONEHOT_BIG, 0.0).astype(BF16)
    ks = ks_ref[...]
    vs_t = vs_ref[...].T
    pad_rows = lax.broadcasted_iota(jnp.int32, (V_ROWS - HEAD_DIM, ts), 0)
    extra = jnp.where(pad_rows == 0, 1.0, 0.0).astype(BF16)
    for g in range(N_KV):
        lo, hi = g * HEAD_DIM, (g + 1) * HEAD_DIM
        ka_ref[0, g, :, 0:n_sel] = onehot
        ka_ref[0, g, :, n_sel:] = ks[:, lo:hi].astype(BF16)
        vst_ref[0, g, 0:HEAD_DIM, :] = vs_t[lo:hi].astype(BF16)
        vst_ref[0, g, HEAD_DIM:, :] = extra

    @pl.when(j == 0)
    def _():
        kwp_ref[...] = jnp.zeros_like(kwp_ref)
        vwtp_ref[...] = jnp.zeros_like(vwtp_ref)

    @pl.when(j > 0)
    def _():
        kw = kw_ref[...]
        vw_t = vw_ref[...].T
        for g in range(N_KV):
            lo, hi = g * HEAD_DIM, (g + 1) * HEAD_DIM
            kwp_ref[0, g] = kw[:, lo:hi].astype(BF16)
            vwtp_ref[0, g] = vw_t[lo:hi].astype(BF16)


def kv_prep(proj, n_batch, seq):
    ts = WINDOW
    nblk = seq // ts
    n_sel = seq // SEL_BLOCK
    kvw = N_KV * HEAD_DIM
    src = lambda col: pl.BlockSpec((ts, kvw), lambda b, j: (b * nblk + jnp.maximum(j - 1, 0), col // kvw))
    same = lambda b, j: (b, 0, jnp.maximum(j - 1, 0), 0)
    same_t = lambda b, j: (b, 0, 0, jnp.maximum(j - 1, 0))
    return pl.pallas_call(
        _kv_prep_kernel,
        grid=(n_batch, nblk + 1),
        in_specs=[src(COL_KS), src(COL_VS), src(COL_KW), src(COL_VW)],
        out_specs=[pl.BlockSpec((1, N_KV, ts, n_sel + HEAD_DIM), same),
                   pl.BlockSpec((1, N_KV, V_ROWS, ts), same_t),
                   pl.BlockSpec((1, N_KV, ts, HEAD_DIM), lambda b, j: (b, 0, j, 0)),
                   pl.BlockSpec((1, N_KV, HEAD_DIM, ts), lambda b, j: (b, 0, 0, j))],
        out_shape=[jax.ShapeDtypeStruct((n_batch, N_KV, seq, n_sel + HEAD_DIM), BF16),
                   jax.ShapeDtypeStruct((n_batch, N_KV, V_ROWS, seq), BF16),
                   jax.ShapeDtypeStruct((n_batch, N_KV, seq + WINDOW, HEAD_DIM), BF16),
                   jax.ShapeDtypeStruct((n_batch, N_KV, HEAD_DIM, seq + WINDOW), BF16)],
        compiler_params=_cparams(("parallel", "arbitrary")),
        name="kv_prep",
    )(proj, proj, proj, proj)


def _softmax_cols(s):
    m = jnp.max(s, axis=0, keepdims=True)
    m = jnp.where(m < 0.5 * NEG, 0.0, m)
    e = jnp.exp2(s - m)
    return e * (1.0 / jnp.maximum(jnp.sum(e, axis=0, keepdims=True), 1e-30))


def _cmp_branch(a, qt, gate, k_ref, vt_ref, tbl_ref, smat_ref, st_ref, rank_ref):
    n_row_tiles = k_ref.shape[2] // CMP_LANE
    ap = a // (FAR_DIST // Q_TILE)
    s = jnp.dot(k_ref[0, 0], qt, preferred_element_type=F32)
    tbl = tbl_ref[0, 0]
    pieces = []
    for c in range(n_row_tiles):
        st = s[c * CMP_LANE:(c + 1) * CMP_LANE]
        cur = st + tbl[CMP_LANE:]
        prev = st + tbl[:CMP_LANE]
        pieces.append(jnp.where(c == ap, cur, jnp.where(c == ap - 1, prev, jnp.where(c < ap, st, NEG))))
    p = _softmax_cols(jnp.concatenate(pieces, axis=0))
    o = gate * jnp.dot(vt_ref[0, 0], p.astype(BF16), preferred_element_type=F32)

    imp = (p[:, 0:Q_TILE] + p[:, Q_TILE:2 * Q_TILE] + p[:, 2 * Q_TILE:3 * Q_TILE]
           + p[:, 3 * Q_TILE:4 * Q_TILE])
    hi = imp.astype(BF16)
    r1 = imp - hi.astype(F32)
    mid = r1.astype(BF16)
    lo = (r1 - mid.astype(F32)).astype(BF16)
    smat = smat_ref[...]
    st = (jnp.dot(smat, hi, preferred_element_type=F32) + jnp.dot(smat, mid, preferred_element_type=F32)
          + jnp.dot(smat, lo, preferred_element_type=F32))

    t = a * Q_TILE + lax.broadcasted_iota(jnp.int32, st.shape, 1)
    jj = lax.broadcasted_iota(jnp.int32, st.shape, 0)
    cur_blk = t // SEL_BLOCK
    valid = jj * SEL_BLOCK <= t
    forced = (jj == 0) | (jj == cur_blk) | (jj == cur_blk - 1)
    st = jnp.where(valid, jnp.where(forced, FORCED_SCORE, st), -1.0)
    st_ref[...] = st
    rank_ref[...] = jnp.zeros_like(rank_ref)
    sub = 8
    chunk = 16
    n_sel = st.shape[0]
    last_valid = (a * Q_TILE + Q_TILE - 1) // SEL_BLOCK
    for q0 in range(0, n_sel, chunk):
        @pl.when(q0 <= last_valid)
        def _(q0=q0):
            for v0 in range(0, n_sel, chunk):
                @pl.when(v0 <= last_valid)
                def _(v0=v0):
                    for v in range(v0 // sub, min(v0 + chunk, n_sel) // sub):
                        sv = st_ref[v * sub:(v + 1) * sub, :]
                        jv = jj[v * sub:(v + 1) * sub]
                        cnt = rank_ref[v * sub:(v + 1) * sub, :]
                        for jp in range(q0, min(q0 + chunk, n_sel)):
                            row = st_ref[jp:jp + 1, :]
                            if jp < v * sub:
                                ahead = row >= sv
                            elif jp >= (v + 1) * sub:
                                ahead = row > sv
                            else:
                                ahead = (row > sv) | ((row == sv) & (jv > jp))
                            cnt = cnt + jnp.where(ahead, 1.0, 0.0)
                        rank_ref[v * sub:(v + 1) * sub, :] = cnt

    keep = (rank_ref[...] < SEL_TOP) & (st >= 0.0)
    return o, (keep.astype(F32) - 1.0).astype(BF16)


def _sel_branch(a, gate, ka_ref, vt_ref, tbl_ref, qa_ref, m_ref, acc_ref):
    tk = SEL_KEY_TILE
    acc_ref[...] = jnp.zeros_like(acc_ref)
    n_tiles = a // (tk // Q_TILE) + 1
    far_pairs = jnp.maximum(0, (a - (SEL_NEAR_TILES - 5)) // 4)
    far_quads = far_pairs // 2
    first_near = 2 * far_pairs
    near_quads = (n_tiles - first_near) // 4
    near_pairs = (n_tiles - first_near - 4 * near_quads + 1) // 2

    def key_off(c):
        return pl.multiple_of(jnp.minimum(c, n_tiles - 1) * tk, tk)

    def attend(c0, n, biased):
        m_old = m_ref[...]
        acc_old = acc_ref[...]

        def score(h):
            c = c0 + h
            s = jnp.dot(ka_ref[0, 0, pl.ds(key_off(c), tk), :], qa_ref[...], preferred_element_type=F32)
            if biased:
                s = s + tbl_ref[0, jnp.where(c < n_tiles, a - 2 * c, SEL_NEAR_TILES - 1)]
            return s

        def weighted(h, s, m):
            return jnp.dot(vt_ref[0, 0, :, pl.ds(key_off(c0 + h), tk)], jnp.exp2(s - m).astype(BF16),
                           preferred_element_type=F32)

        ss = [score(h) for h in range(n)]
        m_new = m_old
        for s in ss:
            m_new = jnp.maximum(m_new, jnp.max(s, axis=0, keepdims=True))
        acc = acc_old
        for h in range(n):
            acc = acc + weighted(h, ss[h], m_old)
        acc_ref[...] = acc * jnp.exp2(m_old - m_new)
        m_ref[...] = m_new

        @pl.when(jnp.max(m_new - m_old) > SAFE_LOG2_RANGE)
        def _():
            acc = jnp.exp2(m_old - m_new) * acc_old
            for h in range(n):
                acc = acc + weighted(h, score(h), m_new)
            acc_ref[...] = acc

    def body(first, n, biased):
        def step(i, carry):
            attend(first + n * i, n, biased)
            return carry
        return step

    sub = 8
    m_ref[...] = jnp.dot(ka_ref[0, 0, 0:sub, :], qa_ref[...], preferred_element_type=F32)[0:1, :]
    lax.fori_loop(0, far_quads, body(0, 4, False), 0)
    lax.fori_loop(0, far_pairs - 2 * far_quads, body(4 * far_quads, 2, False), 0)
    lax.fori_loop(0, near_quads, body(first_near, 4, True), 0)
    lax.fori_loop(0, near_pairs, body(first_near + 4 * near_quads, 2, True), 0)
    acc = acc_ref[...]
    return acc[0:HEAD_DIM] * (gate / acc[HEAD_DIM:HEAD_DIM + 1])


def _win_branch(a, qt, gate, k_ref, vt_ref, tbl_ref):
    span = WINDOW + Q_TILE
    off = pl.multiple_of(a * Q_TILE, Q_TILE)
    s = jnp.dot(k_ref[0, 0, pl.ds(off, span), :], qt, preferred_element_type=F32)
    s = s + tbl_ref[0]
    kj = lax.broadcasted_iota(jnp.int32, s.shape, 0)
    s = jnp.where(kj >= WINDOW - a * Q_TILE, s, NEG)
    p = _softmax_cols(s)
    return gate * jnp.dot(vt_ref[0, 0, :, pl.ds(off, span)], p.astype(BF16), preferred_element_type=F32)


def _nsa_kernel(q_ref, g_ref, kc_ref, vtc_ref, tblc_ref, smat_ref, ka_ref, vst_ref, tbls_ref,
                kw_ref, vwt_ref, tblw_ref, o_ref, gt_ref, st_ref, rank_ref, qa_ref, m_ref, acc_ref):
    g = pl.program_id(0)
    a = pl.program_id(2)
    n_sel = smat_ref.shape[0]
    half = 2 * HEAD_DIM
    q = q_ref[...] * (HEAD_DIM ** -0.5) * LOG2E
    t0 = q[:, :half].T
    t1 = q[:, half:].T
    qt = jnp.concatenate([t0[:HEAD_DIM], t0[HEAD_DIM:], t1[:HEAD_DIM], t1[HEAD_DIM:]],
                         axis=1).astype(BF16)
    gt_ref[...] = g_ref[:, :Q_TILE].T

    def gate(branch):
        rows = [gt_ref[pl.ds((g * GQA_R + r) * 3 + branch, 1), :] for r in range(GQA_R)]
        return jax.nn.sigmoid(jnp.concatenate(rows, axis=1))

    o, sel = _cmp_branch(a, qt, gate(0), kc_ref, vtc_ref, tblc_ref, smat_ref, st_ref, rank_ref)
    for r in range(GQA_R):
        qa_ref[0:n_sel, r * Q_TILE:(r + 1) * Q_TILE] = sel
    qa_ref[n_sel:, :] = qt
    o = o + _sel_branch(a, gate(1), ka_ref, vst_ref, tbls_ref, qa_ref, m_ref, acc_ref)
    o = o + _win_branch(a, qt, gate(2), kw_ref, vwt_ref, tblw_ref)
    u0 = jnp.concatenate([o[:, 0:Q_TILE], o[:, Q_TILE:2 * Q_TILE]], axis=0).T
    u1 = jnp.concatenate([o[:, 2 * Q_TILE:3 * Q_TILE], o[:, 3 * Q_TILE:]], axis=0).T
    o_ref[...] = jnp.concatenate([u0, u1], axis=1).astype(o_ref.dtype)


def nsa_attention(proj, n_batch, seq, k_cmp, vt_cmp, ka, vst, kwp, vwtp, tables, smat):
    tbl_c, tbl_s, tbl_w = tables
    n_a = seq // Q_TILE
    nc = k_cmp.shape[2]
    n_sel = smat.shape[0]
    dh = HEAD_DIM
    gw = GQA_R * HEAD_DIM
    tps = FAR_DIST // Q_TILE
    per_bg = lambda gi, bi, ai: (bi, gi, 0, 0)
    return pl.pallas_call(
        _nsa_kernel,
        grid=(N_KV, n_batch, n_a),
        in_specs=[pl.BlockSpec((Q_TILE, gw), lambda gi, bi, ai: (bi * n_a + ai, COL_Q // gw + gi)),
                  pl.BlockSpec((Q_TILE, gw), lambda gi, bi, ai: (bi * n_a + ai, COL_GATE // gw)),
                  pl.BlockSpec((1, 1, nc, dh), per_bg),
                  pl.BlockSpec((1, 1, dh, nc), per_bg),
                  pl.BlockSpec((1, 1, 2 * CMP_LANE, ROWS), lambda gi, bi, ai: (gi, ai % tps, 0, 0)),
                  pl.BlockSpec(smat.shape, lambda gi, bi, ai: (0, 0)),
                  pl.BlockSpec((1, 1, seq, n_sel + dh), per_bg),
                  pl.BlockSpec((1, 1, V_ROWS, seq), per_bg),
                  pl.BlockSpec((1, SEL_NEAR_TILES, SEL_KEY_TILE, ROWS), lambda gi, bi, ai: (gi, 0, 0, 0)),
                  pl.BlockSpec((1, 1, seq + WINDOW, dh), per_bg),
                  pl.BlockSpec((1, 1, dh, seq + WINDOW), per_bg),
                  pl.BlockSpec((1, WINDOW + Q_TILE, ROWS), lambda gi, bi, ai: (gi, 0, 0))],
        out_specs=pl.BlockSpec((Q_TILE, gw), lambda gi, bi, ai: (bi * n_a + ai, gi)),
        out_shape=jax.ShapeDtypeStruct((n_batch * seq, N_HEADS * HEAD_DIM), BF16),
        scratch_shapes=[pltpu.VMEM((Q_TILE, Q_TILE), F32),
                        pltpu.VMEM((n_sel, Q_TILE), F32), pltpu.VMEM((n_sel, Q_TILE), F32),
                        pltpu.VMEM((n_sel + dh, ROWS), BF16), pltpu.VMEM((1, ROWS), F32),
                        pltpu.VMEM((V_ROWS, ROWS), F32)],
        compiler_params=_cparams(("parallel", "parallel", "arbitrary")),
        name="nsa_attention",
    )(proj, proj, k_cmp, vt_cmp, tbl_c, smat, ka, vst, tbl_s, kwp, vwtp, tbl_w)


def _rel_bucket(dist):
    n = jnp.maximum(dist, 0)
    nf = jnp.maximum(n, 1).astype(F32)
    large = RPB_MAX_EXACT + (jnp.log(nf / RPB_MAX_EXACT) / math.log(RPB_MAX_DIST / RPB_MAX_EXACT)
                             * (N_BUCKETS - RPB_MAX_EXACT)).astype(jnp.int32)
    large = jnp.minimum(large, N_BUCKETS - 1)
    return jnp.where(n < RPB_MAX_EXACT, n, large)


def _toeplitz(e, n_rows):
    w = e.shape[-1]
    lead = e.shape[:-1]
    flat = jnp.broadcast_to(e[..., None, :], lead + (n_rows, w)).reshape(lead + (n_rows * w,))
    return flat[..., :n_rows * (w - 1)].reshape(lead + (n_rows, w - 1))


def _bias_tables(rpb_table):
    tab = rpb_table.astype(F32)
    by_dist = tab[_rel_bucket(jnp.arange(FAR_DIST))].T
    rel = by_dist - tab[N_BUCKETS - 1][:, None]
    neg = lambda n: jnp.full((N_HEADS, n), NEG, F32)
    zero = lambda n: jnp.zeros((N_HEADS, n), F32)

    def to_rows(m):
        m = m.reshape((N_KV, GQA_R) + m.shape[1:])
        m = jnp.moveaxis(m, 1, -2)
        return m.reshape(m.shape[:-2] + (ROWS,))

    tk = SEL_KEY_TILE
    ext = jnp.concatenate([neg(tk), rel], axis=1)
    win = jnp.stack([ext[:, Q_TILE * k + 1:Q_TILE * k + 1 + tk + Q_TILE] for k in range(SEL_NEAR_TILES - 1)]
                    + [neg(tk + Q_TILE)], axis=1)
    tbl_s = to_rows(_toeplitz(win, tk)[..., tk - 1:tk - 1 + Q_TILE]) * LOG2E

    span = WINDOW + Q_TILE
    ext = jnp.concatenate([neg(Q_TILE - 1), by_dist[:, :WINDOW], neg(Q_TILE + 1)], axis=1)
    tbl_w = to_rows(_toeplitz(ext, span)[..., span - 1:span - 1 + Q_TILE]) * LOG2E

    n_m = 2 * CMP_LANE
    n_tau = FAR_DIST // CMP_STRIDE
    base = CMP_STRIDE * (n_m - 1) - (FAR_DIST - CMP_BLOCK + 1)
    length = CMP_STRIDE * (n_m + n_tau)
    ext = jnp.concatenate([neg(base), rel, zero(length - base - FAR_DIST)], axis=1)
    ev = ext.reshape(N_HEADS, n_m + n_tau, CMP_STRIDE).transpose(0, 2, 1)
    tz = _toeplitz(ev, n_m)[..., n_m - 1:n_m - 1 + n_tau]
    tz = tz.transpose(0, 2, 3, 1).reshape(N_HEADS, n_m, FAR_DIST // Q_TILE, Q_TILE)
    tbl_c = to_rows(tz.transpose(0, 2, 1, 3)) * LOG2E
    return tbl_c, tbl_s, tbl_w


def _slc_matrix(n_cmp_pad, n_sel):
    sel_ratio, cmp_ratio = SEL_BLOCK // CMP_STRIDE, CMP_BLOCK // CMP_STRIDE
    mat = np.zeros((n_sel, n_cmp_pad), np.float32)
    for j in range(n_sel):
        for mm in range(sel_ratio):
            for nn in range(cmp_ratio):
                idx = sel_ratio * j - mm - nn
                if 0 <= idx < n_cmp_pad - 1:
                    mat[j, idx] += 1.0
    return jnp.asarray(mat, BF16)


def nsa_mixer(proj, n_batch, seq, cmp_k_w, cmp_v_w, tables, smat):
    kv = N_KV * HEAD_DIM
    rows16 = lambda col: proj[:, col:col + kv].reshape(n_batch, seq // CMP_STRIDE, CMP_STRIDE * kv)
    k_cmp = compress(rows16(COL_KC), *cmp_k_w)
    v_cmp = compress(rows16(COL_VC), *cmp_v_w)
    k_cmp = k_cmp.reshape(n_batch, -1, N_KV, HEAD_DIM).transpose(0, 2, 1, 3)
    vt_cmp = v_cmp.reshape(n_batch, -1, N_KV, HEAD_DIM).transpose(0, 2, 3, 1)
    ka, vst, kwp, vwtp = kv_prep(proj, n_batch, seq)
    return nsa_attention(proj, n_batch, seq, k_cmp, vt_cmp, ka, vst, kwp, vwtp, tables, smat)


def kernel(x, rpb_table, attn_norm, ffn_norm, final_norm, w_in, ssm_a_re, ssm_a_im, ssm_log_dt,
           ssm_b_re, ssm_b_im, ssm_c_re, ssm_c_im, ssm_d, w_glu, b_glu, cmp_pos_k, cmp_w1_k, cmp_w2_k,
           cmp_pos_v, cmp_w1_v, cmp_w2_v, w_out, w_ffn_gate, w_ffn_up, w_ffn_down):
    n_batch, seq, d_model = x.shape
    depth = w_in.shape[0]
    tables = _bias_tables(rpb_table)
    smat = _slc_matrix(seq // CMP_STRIDE, seq // SEL_BLOCK)

    xf = x.reshape(n_batch * seq, d_model)
    for l in range(depth):
        w_in_l = jnp.pad(w_in[l], ((0, 0), (0, IN_COLS_PAD - IN_COLS))).astype(BF16)
        proj = norm_matmul(xf, attn_norm[l].reshape(1, -1), w_in_l)
        s5w = _s5_weights(ssm_a_re[l], ssm_a_im[l], ssm_log_dt[l], ssm_b_re[l], ssm_b_im[l],
                          ssm_c_re[l], ssm_c_im[l])
        y_ssm = s5_glu(proj, n_batch, *s5w, ssm_d[l].reshape(1, -1), w_glu[l].astype(BF16),
                       b_glu[l].reshape(1, -1))
        y_nsa = nsa_mixer(proj, n_batch, seq,
                          _compress_weights(cmp_pos_k[l], cmp_w1_k[l], cmp_w2_k[l]),
                          _compress_weights(cmp_pos_v[l], cmp_w1_v[l], cmp_w2_v[l]),
                          tables, smat)
        w_out_l = w_out[l].astype(BF16)
        xf = out_proj(xf, y_ssm, y_nsa, w_out_l[:D_SSM], w_out_l[D_SSM:])
        xf = ffn(xf, ffn_norm[l].reshape(1, -1), w_ffn_gate[l].astype(BF16),
                 w_ffn_up[l].astype(BF16), w_ffn_down[l].astype(BF16))
    out = final_rms_norm(xf, final_norm.reshape(1, -1))
    return out.reshape(n_batch, seq, d_model)
```

```python
import math

import jax
import jax.numpy as jnp
import numpy as np
from jax import lax
from jax.experimental import pallas as pl
from jax.experimental.pallas import tpu as pltpu

F32 = jnp.float32
BF16 = jnp.bfloat16

D_MODEL = 2048
DEPTH = 4
D_SSM = 1024
SSM_GROUP = 16
N_SSM_GROUPS = 64
SSM_STATE = 64
N_STATE = N_SSM_GROUPS * SSM_STATE
SSM_PACK = 16
N_PACKS = N_SSM_GROUPS // SSM_PACK
N_HEADS = 16
N_KV = 4
GQA_R = 4
HEAD_DIM = 64
CMP_BLOCK = 32
CMP_STRIDE = 16
CMP_HIDDEN = 128
SEL_BLOCK = 64
SEL_TOP = 16
WINDOW = 512
Q_TILE = 128
ROWS = GQA_R * Q_TILE
FORCED_SCORE = 1e4
N_BUCKETS = 32
RPB_MAX_EXACT = 16
RPB_MAX_DIST = 1024
D_FF = 5632
NORM_EPS = 1e-6
IN_COLS = 3632
IN_COLS_PAD = 3840
COL_U, COL_Q, COL_KC, COL_VC, COL_KS, COL_VS, COL_KW, COL_VW, COL_GATE = (
    0, 1024, 2048, 2304, 2560, 2816, 3072, 3328, 3584)
NEG = -1e30
ONEHOT_BIG = 1e30
FAR_DIST = 2048
SEL_KEY_TILE = 256
SEL_NEAR_TILES = 12
LOG2E = math.log2(math.e)
SAFE_LOG2_RANGE = 64.0
CMP_LANE = 128
V_ROWS = 80
VMEM_LIMIT = 56 * 1024 * 1024


def _cparams(sem):
    return pltpu.CompilerParams(dimension_semantics=sem, vmem_limit_bytes=VMEM_LIMIT)


def _rms(x, g):
    ms = jnp.mean(x * x, axis=-1, keepdims=True)
    return x * lax.rsqrt(ms + NORM_EPS) * g


def _norm_matmul_kernel(x_ref, g_ref, w_ref, o_ref, h_ref):
    @pl.when(pl.program_id(1) == 0)
    def _():
        h_ref[...] = _rms(x_ref[...], g_ref[...]).astype(BF16)

    o_ref[...] = jnp.dot(h_ref[...], w_ref[...], preferred_element_type=F32)


def norm_matmul(x, g, w, *, tm=1024, tn=768):
    m, k = x.shape
    n = w.shape[1]
    return pl.pallas_call(
        _norm_matmul_kernel,
        grid=(m // tm, n // tn),
        in_specs=[pl.BlockSpec((tm, k), lambda i, j: (i, 0)),
                  pl.BlockSpec((1, k), lambda i, j: (0, 0)),
                  pl.BlockSpec((k, tn), lambda i, j: (0, j))],
        out_specs=pl.BlockSpec((tm, tn), lambda i, j: (i, j)),
        out_shape=jax.ShapeDtypeStruct((m, n), F32),
        scratch_shapes=[pltpu.VMEM((tm, k), BF16)],
        compiler_params=_cparams(("parallel", "arbitrary")),
        name="norm_matmul",
    )(x, g, w)


def _out_proj_kernel(x_ref, ys_ref, yn_ref, w1_ref, w2_ref, o_ref):
    o_ref[...] = (x_ref[...]
                  + jnp.dot(ys_ref[...], w1_ref[...], preferred_element_type=F32)
                  + jnp.dot(yn_ref[...], w2_ref[...], preferred_element_type=F32))


def out_proj(x, y_ssm, y_nsa, w_ssm, w_nsa, *, tm=512):
    m, d = x.shape
    k = y_ssm.shape[1]
    row = lambda i: (i, 0)
    full = lambda i: (0, 0)
    return pl.pallas_call(
        _out_proj_kernel,
        grid=(m // tm,),
        in_specs=[pl.BlockSpec((tm, d), row), pl.BlockSpec((tm, k), row), pl.BlockSpec((tm, k), row),
                  pl.BlockSpec((k, d), full), pl.BlockSpec((k, d), full)],
        out_specs=pl.BlockSpec((tm, d), row),
        out_shape=jax.ShapeDtypeStruct((m, d), F32),
        compiler_params=_cparams(("parallel",)),
        name="out_proj",
    )(x, y_ssm, y_nsa, w_ssm, w_nsa)


def _ffn_kernel(x_ref, g_ref, wg_ref, wu_ref, wd_ref, o_ref, h_ref):
    f = pl.program_id(1)

    @pl.when(f == 0)
    def _():
        x = x_ref[...]
        h_ref[...] = _rms(x, g_ref[...]).astype(BF16)
        o_ref[...] = x

    h = h_ref[...]
    gate = jnp.dot(h, wg_ref[...], preferred_element_type=F32)
    up = jnp.dot(h, wu_ref[...], preferred_element_type=F32)
    act = (jax.nn.silu(gate) * up).astype(BF16)
    o_ref[...] += jnp.dot(act, wd_ref[...], preferred_element_type=F32)


def ffn(x, g, w_gate, w_up, w_down, *, tm=1024, tf=256):
    m, d = x.shape
    dff = w_gate.shape[1]
    return pl.pallas_call(
        _ffn_kernel,
        grid=(m // tm, dff // tf),
        in_specs=[pl.BlockSpec((tm, d), lambda i, f: (i, 0)),
                  pl.BlockSpec((1, d), lambda i, f: (0, 0)),
                  pl.BlockSpec((d, tf), lambda i, f: (0, f)),
                  pl.BlockSpec((d, tf), lambda i, f: (0, f)),
                  pl.BlockSpec((tf, d), lambda i, f: (f, 0))],
        out_specs=pl.BlockSpec((tm, d), lambda i, f: (i, 0)),
        out_shape=jax.ShapeDtypeStruct((m, d), F32),
        scratch_shapes=[pltpu.VMEM((tm, d), BF16)],
        compiler_params=_cparams(("parallel", "arbitrary")),
        name="ffn",
    )(x, g, w_gate, w_up, w_down)


def _final_norm_kernel(x_ref, g_ref, o_ref):
    o_ref[...] = _rms(x_ref[...], g_ref[...])


def final_rms_norm(x, g, *, tm=512):
    m, d = x.shape
    return pl.pallas_call(
        _final_norm_kernel,
        grid=(m // tm,),
        in_specs=[pl.BlockSpec((tm, d), lambda i: (i, 0)), pl.BlockSpec((1, d), lambda i: (0, 0))],
        out_specs=pl.BlockSpec((tm, d), lambda i: (i, 0)),
        out_shape=jax.ShapeDtypeStruct((m, d), F32),
        compiler_params=_cparams(("parallel",)),
        name="final_norm",
    )(x, g)


def _s5_discretize_kernel(are_ref, aim_ref, ldt_ref, bre_ref, bim_ref,
                          abre_ref, abim_ref, bbre_ref, bbim_ref):
    lam_re = jnp.minimum(are_ref[...], -1e-4)
    lam_im = aim_ref[...]
    dt = jnp.exp(ldt_ref[...])
    mag = jnp.exp(lam_re * dt)
    ab_re = mag * jnp.cos(lam_im * dt)
    ab_im = mag * jnp.sin(lam_im * dt)
    nr, ni = ab_re - 1.0, ab_im
    den = lam_re * lam_re + lam_im * lam_im
    f_re = (nr * lam_re + ni * lam_im) / den
    f_im = (ni * lam_re - nr * lam_im) / den
    br, bi = bre_ref[...], bim_ref[...]
    abre_ref[...] = ab_re
    abim_ref[...] = ab_im
    bbre_ref[...] = f_re * br - f_im * bi
    bbim_ref[...] = f_re * bi + f_im * br


def s5_discretize(a_re, a_im, log_dt, b_re_t, b_im_t):
    g, p = a_re.shape
    h = b_re_t.shape[1]
    outs = pl.pallas_call(
        _s5_discretize_kernel,
        out_shape=[jax.ShapeDtypeStruct((g, 1, p), F32), jax.ShapeDtypeStruct((g, 1, p), F32),
                   jax.ShapeDtypeStruct((g, h, p), F32), jax.ShapeDtypeStruct((g, h, p), F32)],
        name="s5_discretize",
    )(a_re.reshape(g, 1, p), a_im.reshape(g, 1, p), log_dt.reshape(g, 1, 1), b_re_t, b_im_t)
    return outs


def _s5_kernel(u_ref, bre_ref, bim_ref, cre_ref, cimn_ref, a_ref, d_ref, wglu_ref, bglu_ref,
               o_ref, xr_ref, xi_ref, st_ref):
    tc = u_ref.shape[0]
    pw = SSM_PACK * SSM_GROUP
    sw = SSM_PACK * SSM_STATE

    @pl.when(pl.program_id(1) == 0)
    def _():
        st_ref[...] = jnp.zeros_like(st_ref)

    u = u_ref[...]
    ub = u.astype(BF16)
    for k in range(N_PACKS):
        uk = ub[:, k * pw:(k + 1) * pw]
        xr_ref[:, k * sw:(k + 1) * sw] = jnp.dot(uk, bre_ref[k], preferred_element_type=F32)
        xi_ref[:, k * sw:(k + 1) * sw] = jnp.dot(uk, bim_ref[k], preferred_element_type=F32)

    a_re = a_ref[0:1, :]
    a_im = a_ref[1:2, :]

    def step(t, state):
        s_re, s_im = state
        n_re = a_re * s_re - a_im * s_im + xr_ref[pl.ds(t, 1), :]
        n_im = a_re * s_im + a_im * s_re + xi_ref[pl.ds(t, 1), :]
        xr_ref[pl.ds(t, 1), :] = n_re
        xi_ref[pl.ds(t, 1), :] = n_im
        return n_re, n_im

    s_re, s_im = lax.fori_loop(0, tc, step, (st_ref[0:1, :], st_ref[1:2, :]), unroll=4)
    st_ref[0:1, :] = s_re
    st_ref[1:2, :] = s_im

    ys = []
    for k in range(N_PACKS):
        xr = xr_ref[:, k * sw:(k + 1) * sw].astype(BF16)
        xi = xi_ref[:, k * sw:(k + 1) * sw].astype(BF16)
        ys.append(jnp.dot(xr, cre_ref[k], preferred_element_type=F32)
                  + jnp.dot(xi, cimn_ref[k], preferred_element_type=F32))
    y = jnp.concatenate(ys, axis=-1) + d_ref[...] * u
    y = jax.nn.gelu(y).astype(BF16)
    ab = jnp.dot(y, wglu_ref[...], preferred_element_type=F32) + bglu_ref[...]
    o_ref[...] = (ab[:, :D_SSM] * jax.nn.sigmoid(ab[:, D_SSM:])).astype(o_ref.dtype)


def s5_glu(proj, n_batch, bbd_re, bbd_im, cbd_re, cbd_imn, a_flat, d, w_glu, b_glu, *, tc=256):
    t = proj.shape[0]
    n_chunks = t // n_batch // tc
    c3 = lambda b, c: (0, 0, 0)
    c2 = lambda b, c: (0, 0)
    return pl.pallas_call(
        _s5_kernel,
        grid=(n_batch, n_chunks),
        in_specs=[pl.BlockSpec((tc, D_SSM), lambda b, c: (b * n_chunks + c, 0)),
                  pl.BlockSpec(bbd_re.shape, c3), pl.BlockSpec(bbd_im.shape, c3),
                  pl.BlockSpec(cbd_re.shape, c3), pl.BlockSpec(cbd_imn.shape, c3),
                  pl.BlockSpec(a_flat.shape, c2), pl.BlockSpec(d.shape, c2),
                  pl.BlockSpec(w_glu.shape, c2), pl.BlockSpec(b_glu.shape, c2)],
        out_specs=pl.BlockSpec((tc, D_SSM), lambda b, c: (b * n_chunks + c, 0)),
        out_shape=jax.ShapeDtypeStruct((t, D_SSM), BF16),
        scratch_shapes=[pltpu.VMEM((tc, N_STATE), F32), pltpu.VMEM((tc, N_STATE), F32),
                        pltpu.VMEM((2, N_STATE), F32)],
        compiler_params=_cparams(("arbitrary", "arbitrary")),
        name="s5_glu",
    )(proj, bbd_re, bbd_im, cbd_re, cbd_imn, a_flat, d, w_glu, b_glu)


def _s5_weights(a_re, a_im, log_dt, b_re, b_im, c_re, c_im):
    ab_re, ab_im, bb_re, bb_im = s5_discretize(a_re, a_im, log_dt,
                                               b_re.transpose(0, 2, 1), b_im.transpose(0, 2, 1))
    eye = jnp.eye(SSM_PACK, dtype=F32)

    def pack_b(bb):
        bb = bb.reshape(N_PACKS, SSM_PACK, SSM_GROUP, 1, SSM_STATE)
        m = bb * eye[None, :, None, :, None]
        return m.reshape(N_PACKS, SSM_PACK * SSM_GROUP, SSM_PACK * SSM_STATE).astype(BF16)

    def pack_c(c):
        c = c.reshape(N_PACKS, SSM_PACK, SSM_GROUP, SSM_STATE)
        m = c.transpose(0, 1, 3, 2)[:, :, :, None, :] * eye[None, :, None, :, None]
        return m.reshape(N_PACKS, SSM_PACK * SSM_STATE, SSM_PACK * SSM_GROUP).astype(BF16)

    a_flat = jnp.stack([ab_re.reshape(-1), ab_im.reshape(-1)])
    return pack_b(bb_re), pack_b(bb_im), pack_c(c_re), pack_c(-c_im), a_flat


def _compress_kernel(r_ref, n_ref, pos_ref, w1a_ref, w1b_ref, w2_ref, o_ref):
    r = r_ref[0]
    rows = r.shape[0]
    rowid = lax.broadcasted_iota(jnp.int32, (rows, 1), 0)
    nxt = jnp.where(rowid == rows - 1, n_ref[0, 0:1, :], pltpu.roll(r, rows - 1, 0))
    first = (r + pos_ref[0:1, :]).astype(BF16)
    second = (nxt + pos_ref[1:2, :]).astype(BF16)
    hid = (jnp.dot(first, w1a_ref[...], preferred_element_type=F32)
           + jnp.dot(second, w1b_ref[...], preferred_element_type=F32))
    hid = jax.nn.gelu(hid).astype(BF16)
    o_ref[0] = jnp.dot(hid, w2_ref[...], preferred_element_type=F32).astype(o_ref.dtype)


def compress(kc_rows, pos2, w1a, w1b, w2e):
    b, nc, w = kc_rows.shape
    tr = 128
    sub = 8
    full = lambda i, j: (0, 0)
    nxt_blk = lambda i, j: (i, jnp.minimum((j + 1) * (tr // sub), nc // sub - 1), 0)
    return pl.pallas_call(
        _compress_kernel,
        grid=(b, nc // tr),
        in_specs=[pl.BlockSpec((1, tr, w), lambda i, j: (i, j, 0)),
                  pl.BlockSpec((1, sub, w), nxt_blk),
                  pl.BlockSpec(pos2.shape, full), pl.BlockSpec(w1a.shape, full),
                  pl.BlockSpec(w1b.shape, full), pl.BlockSpec(w2e.shape, full)],
        out_specs=pl.BlockSpec((1, tr, N_KV * HEAD_DIM), lambda i, j: (i, j, 0)),
        out_shape=jax.ShapeDtypeStruct((b, nc, N_KV * HEAD_DIM), BF16),
        compiler_params=_cparams(("parallel", "parallel")),
        name="compress",
    )(kc_rows, kc_rows, pos2, w1a, w1b, w2e)


def _compress_weights(pos, w1, w2):
    eye = jnp.eye(N_KV, dtype=F32)
    w1 = w1.reshape(2, CMP_STRIDE, 1, HEAD_DIM, 1, CMP_HIDDEN)
    w1e = (w1 * eye[None, None, :, None, :, None]).reshape(
        2, CMP_STRIDE * N_KV * HEAD_DIM, N_KV * CMP_HIDDEN).astype(BF16)
    w2e = (w2[None, :, None, :] * eye[:, None, :, None]).reshape(
        N_KV * CMP_HIDDEN, N_KV * HEAD_DIM).astype(BF16)
    pos2 = jnp.broadcast_to(pos.reshape(2, CMP_STRIDE, 1, HEAD_DIM),
                            (2, CMP_STRIDE, N_KV, HEAD_DIM)).reshape(2, -1)
    return pos2, w1e[0], w1e[1], w2e


def _kv_prep_kernel(ks_ref, vs_ref, kw_ref, vw_ref, ka_ref, vst_ref, kwp_ref, vwtp_ref):
    j = pl.program_id(1)
    ts = ks_ref.shape[0]
    n_sel = ka_ref.shape[3] - HEAD_DIM
    blk0 = jnp.maximum(j - 1, 0) * (ts // SEL_BLOCK)
    row_blk = blk0 + lax.broadcasted_iota(jnp.int32, (ts, n_sel), 0) // SEL_BLOCK
    col = lax.broadcasted_iota(jnp.int32, (ts, n_sel), 1)
    onehot = jnp.where(row_blk == col, ONEHOT_BIG, 0.0).astype(BF16)
    ks = ks_ref[...]
    vs_t = vs_ref[...].T
    pad_rows = lax.broadcasted_iota(jnp.int32, (V_ROWS - HEAD_DIM, ts), 0)
    extra = jnp.where(pad_rows == 0, 1.0, 0.0).astype(BF16)
    for g in range(N_KV):
        lo, hi = g * HEAD_DIM, (g + 1) * HEAD_DIM
        ka_ref[0, g, :, 0:n_sel] = onehot
        ka_ref[0, g, :, n_sel:] = ks[:, lo:hi].astype(BF16)
        vst_ref[0, g, 0:HEAD_DIM, :] = vs_t[lo:hi].astype(BF16)
        vst_ref[0, g, HEAD_DIM:, :] = extra

    @pl.when(j == 0)
    def _():
        kwp_ref[...] = jnp.zeros_like(kwp_ref)
        vwtp_ref[...] = jnp.zeros_like(vwtp_ref)

    @pl.when(j > 0)
    def _():
        kw = kw_ref[...]
        vw_t = vw_ref[...].T
        for g in range(N_KV):
            lo, hi = g * HEAD_DIM, (g + 1) * HEAD_DIM
            kwp_ref[0, g] = kw[:, lo:hi].astype(BF16)
            vwtp_ref[0, g] = vw_t[lo:hi].astype(BF16)


def kv_prep(proj, n_batch, seq):
    ts = WINDOW
    nblk = seq // ts
    n_sel = seq // SEL_BLOCK
    kvw = N_KV * HEAD_DIM
    src = lambda col: pl.BlockSpec((ts, kvw), lambda b, j: (b * nblk + jnp.maximum(j - 1, 0), col // kvw))
    same = lambda b, j: (b, 0, jnp.maximum(j - 1, 0), 0)
    same_t = lambda b, j: (b, 0, 0, jnp.maximum(j - 1, 0))
    return pl.pallas_call(
        _kv_prep_kernel,
        grid=(n_batch, nblk + 1),
        in_specs=[src(COL_KS), src(COL_VS), src(COL_KW), src(COL_VW)],
        out_specs=[pl.BlockSpec((1, N_KV, ts, n_sel + HEAD_DIM), same),
                   pl.BlockSpec((1, N_KV, V_ROWS, ts), same_t),
                   pl.BlockSpec((1, N_KV, ts, HEAD_DIM), lambda b, j: (b, 0, j, 0)),
                   pl.BlockSpec((1, N_KV, HEAD_DIM, ts), lambda b, j: (b, 0, 0, j))],
        out_shape=[jax.ShapeDtypeStruct((n_batch, N_KV, seq, n_sel + HEAD_DIM), BF16),
                   jax.ShapeDtypeStruct((n_batch, N_KV, V_ROWS, seq), BF16),
                   jax.ShapeDtypeStruct((n_batch, N_KV, seq + WINDOW, HEAD_DIM), BF16),
                   jax.ShapeDtypeStruct((n_batch, N_KV, HEAD_DIM, seq + WINDOW), BF16)],
        compiler_params=_cparams(("parallel", "arbitrary")),
        name="kv_prep",
    )(proj, proj, proj, proj)


def _softmax_cols(s):
    m = jnp.max(s, axis=0, keepdims=True)
    m = jnp.where(m < 0.5 * NEG, 0.0, m)
    e = jnp.exp2(s - m)
    return e * (1.0 / jnp.maximum(jnp.sum(e, axis=0, keepdims=True), 1e-30))


def _cmp_branch(a, qt, gate, k_ref, vt_ref, tbl_ref, smat_ref, o_ref, st_ref, rank_ref):
    n_row_tiles = k_ref.shape[2] // CMP_LANE
    ap = a // (FAR_DIST // Q_TILE)

    def attend(n):
        rows = n * CMP_LANE
        s = jnp.dot(k_ref[0, 0, 0:rows, :], qt, preferred_element_type=F32)
        tbl = tbl_ref[0, 0]
        pieces = [s[c * CMP_LANE:(c + 1) * CMP_LANE] for c in range(n)]
        pieces[-1] = pieces[-1] + tbl[CMP_LANE:]
        if n > 1:
            pieces[-2] = pieces[-2] + tbl[:CMP_LANE]
        p = _softmax_cols(jnp.concatenate(pieces, axis=0))
        o_ref[...] = gate * jnp.dot(vt_ref[0, 0, :, 0:rows], p.astype(BF16), preferred_element_type=F32)
        imp = (p[:, 0:Q_TILE] + p[:, Q_TILE:2 * Q_TILE] + p[:, 2 * Q_TILE:3 * Q_TILE]
               + p[:, 3 * Q_TILE:4 * Q_TILE])
        hi = imp.astype(BF16)
        r1 = imp - hi.astype(F32)
        mid = r1.astype(BF16)
        lo = (r1 - mid.astype(F32)).astype(BF16)
        smat = smat_ref[:, 0:rows]
        st_ref[...] = (jnp.dot(smat, hi, preferred_element_type=F32)
                       + jnp.dot(smat, mid, preferred_element_type=F32)
                       + jnp.dot(smat, lo, preferred_element_type=F32))

    for n in range(1, n_row_tiles + 1):
        pl.when(ap == n - 1)(lambda n=n: attend(n))

    o = o_ref[...]
    st = st_ref[...]
    t = a * Q_TILE + lax.broadcasted_iota(jnp.int32, st.shape, 1)
    jj = lax.broadcasted_iota(jnp.int32, st.shape, 0)
    cur_blk = t // SEL_BLOCK
    valid = jj * SEL_BLOCK <= t
    forced = (jj == 0) | (jj == cur_blk) | (jj == cur_blk - 1)
    st = jnp.where(valid, jnp.where(forced, FORCED_SCORE, st), -1.0)
    st_ref[...] = st
    rank_ref[...] = jnp.zeros_like(rank_ref)
    sub = 8
    chunk = 16
    n_sel = st.shape[0]
    last_valid = (a * Q_TILE + Q_TILE - 1) // SEL_BLOCK
    for q0 in range(0, n_sel, chunk):
        @pl.when(q0 <= last_valid)
        def _(q0=q0):
            for v0 in range(0, n_sel, chunk):
                @pl.when(v0 <= last_valid)
                def _(v0=v0):
                    for v in range(v0 // sub, min(v0 + chunk, n_sel) // sub):
                        sv = st_ref[v * sub:(v + 1) * sub, :]
                        jv = jj[v * sub:(v + 1) * sub]
                        cnt = rank_ref[v * sub:(v + 1) * sub, :]
                        for jp in range(q0, min(q0 + chunk, n_sel)):
                            row = st_ref[jp:jp + 1, :]
                            if jp < v * sub:
                                ahead = row >= sv
                            elif jp >= (v + 1) * sub:
                                ahead = row > sv
                            else:
                                ahead = (row > sv) | ((row == sv) & (jv > jp))
                            cnt = cnt + jnp.where(ahead, 1.0, 0.0)
                        rank_ref[v * sub:(v + 1) * sub, :] = cnt

    keep = (rank_ref[...] < SEL_TOP) & (st >= 0.0)
    return o, (keep.astype(F32) - 1.0).astype(BF16)


def _sel_branch(a, gate, ka_ref, vt_ref, tbl_ref, qa_ref, m_ref, acc_ref):
    tk = SEL_KEY_TILE
    acc_ref[...] = jnp.zeros_like(acc_ref)
    n_tiles = a // (tk // Q_TILE) + 1
    far_pairs = jnp.maximum(0, (a - (SEL_NEAR_TILES - 5)) // 4)
    far_quads = far_pairs // 2
    first_near = 2 * far_pairs
    near_quads = (n_tiles - first_near) // 4
    near_pairs = (n_tiles - first_near - 4 * near_quads + 1) // 2

    def key_off(c):
        return pl.multiple_of(jnp.minimum(c, n_tiles - 1) * tk, tk)

    def attend(c0, n, biased):
        m_old = m_ref[...]
        acc_old = acc_ref[...]

        def score(h):
            c = c0 + h
            s = jnp.dot(ka_ref[0, 0, pl.ds(key_off(c), tk), :], qa_ref[...], preferred_element_type=F32)
            if biased:
                s = s + tbl_ref[0, jnp.where(c < n_tiles, a - 2 * c, SEL_NEAR_TILES - 1)]
            return s

        def weighted(h, s, m):
            return jnp.dot(vt_ref[0, 0, :, pl.ds(key_off(c0 + h), tk)], jnp.exp2(s - m).astype(BF16),
                           preferred_element_type=F32)

        ss = [score(h) for h in range(n)]
        m_new = m_old
        for s in ss:
            m_new = jnp.maximum(m_new, jnp.max(s, axis=0, keepdims=True))
        acc = acc_old
        for h in range(n):
            acc = acc + weighted(h, ss[h], m_old)
        acc_ref[...] = acc * jnp.exp2(m_old - m_new)
        m_ref[...] = m_new

        @pl.when(jnp.max(m_new - m_old) > SAFE_LOG2_RANGE)
        def _():
            acc = jnp.exp2(m_old - m_new) * acc_old
            for h in range(n):
                acc = acc + weighted(h, score(h), m_new)
            acc_ref[...] = acc

    def body(first, n, biased):
        def step(i, carry):
            attend(first + n * i, n, biased)
            return carry
        return step

    sub = 8
    m_ref[...] = jnp.dot(ka_ref[0, 0, 0:sub, :], qa_ref[...], preferred_element_type=F32)[0:1, :]
    lax.fori_loop(0, far_quads, body(0, 4, False), 0)
    lax.fori_loop(0, far_pairs - 2 * far_quads, body(4 * far_quads, 2, False), 0)
    lax.fori_loop(0, near_quads, body(first_near, 4, True), 0)
    lax.fori_loop(0, near_pairs, body(first_near + 4 * near_quads, 2, True), 0)
    acc = acc_ref[...]
    return acc[0:HEAD_DIM] * (gate / acc[HEAD_DIM:HEAD_DIM + 1])


def _win_branch(a, qt, gate, k_ref, vt_ref, tbl_ref):
    span = WINDOW + Q_TILE
    off = pl.multiple_of(a * Q_TILE, Q_TILE)
    s = jnp.dot(k_ref[0, 0, pl.ds(off, span), :], qt, preferred_element_type=F32)
    s = s + tbl_ref[0, 0]
    e = jnp.exp2(s - jnp.max(s, axis=0, keepdims=True))
    ev = jnp.dot(vt_ref[0, 0, :, pl.ds(off, span)], e.astype(BF16), preferred_element_type=F32)
    return ev * (gate / jnp.sum(e, axis=0, keepdims=True))


def _nsa_kernel(q_ref, g_ref, kc_ref, vtc_ref, tblc_ref, smat_ref, ka_ref, vst_ref, tbls_ref,
                kw_ref, vwt_ref, tblw_ref, o_ref, gt_ref, oc_ref, st_ref, rank_ref, qa_ref, m_ref, acc_ref):
    g = pl.program_id(0)
    a = pl.program_id(2)
    n_sel = smat_ref.shape[0]
    half = 2 * HEAD_DIM
    q = q_ref[...] * (HEAD_DIM ** -0.5) * LOG2E
    t0 = q[:, :half].T
    t1 = q[:, half:].T
    qt = jnp.concatenate([t0[:HEAD_DIM], t0[HEAD_DIM:], t1[:HEAD_DIM], t1[HEAD_DIM:]],
                         axis=1).astype(BF16)
    gt_ref[...] = g_ref[:, :Q_TILE].T

    def gate(branch):
        rows = [gt_ref[pl.ds((g * GQA_R + r) * 3 + branch, 1), :] for r in range(GQA_R)]
        return jax.nn.sigmoid(jnp.concatenate(rows, axis=1))

    o, sel = _cmp_branch(a, qt, gate(0), kc_ref, vtc_ref, tblc_ref, smat_ref, oc_ref, st_ref, rank_ref)
    for r in range(GQA_R):
        qa_ref[0:n_sel, r * Q_TILE:(r + 1) * Q_TILE] = sel
    qa_ref[n_sel:, :] = qt
    o = o + _sel_branch(a, gate(1), ka_ref, vst_ref, tbls_ref, qa_ref, m_ref, acc_ref)
    o = o + _win_branch(a, qt, gate(2), kw_ref, vwt_ref, tblw_ref)
    u0 = jnp.concatenate([o[:, 0:Q_TILE], o[:, Q_TILE:2 * Q_TILE]], axis=0).T
    u1 = jnp.concatenate([o[:, 2 * Q_TILE:3 * Q_TILE], o[:, 3 * Q_TILE:]], axis=0).T
    o_ref[...] = jnp.concatenate([u0, u1], axis=1).astype(o_ref.dtype)


def nsa_attention(proj, n_batch, seq, k_cmp, vt_cmp, ka, vst, kwp, vwtp, tables, smat):
    tbl_c, tbl_s, tbl_w = tables
    n_a = seq // Q_TILE
    nc = k_cmp.shape[2]
    n_sel = smat.shape[0]
    dh = HEAD_DIM
    gw = GQA_R * HEAD_DIM
    tps = FAR_DIST // Q_TILE
    per_bg = lambda gi, bi, ai: (bi, gi, 0, 0)
    return pl.pallas_call(
        _nsa_kernel,
        grid=(N_KV, n_batch, n_a),
        in_specs=[pl.BlockSpec((Q_TILE, gw), lambda gi, bi, ai: (bi * n_a + ai, COL_Q // gw + gi)),
                  pl.BlockSpec((Q_TILE, gw), lambda gi, bi, ai: (bi * n_a + ai, COL_GATE // gw)),
                  pl.BlockSpec((1, 1, nc, dh), per_bg),
                  pl.BlockSpec((1, 1, dh, nc), per_bg),
                  pl.BlockSpec((1, 1, 2 * CMP_LANE, ROWS), lambda gi, bi, ai: (gi, ai % tps, 0, 0)),
                  pl.BlockSpec(smat.shape, lambda gi, bi, ai: (0, 0)),
                  pl.BlockSpec((1, 1, seq, n_sel + dh), per_bg),
                  pl.BlockSpec((1, 1, V_ROWS, seq), per_bg),
                  pl.BlockSpec((1, SEL_NEAR_TILES, SEL_KEY_TILE, ROWS), lambda gi, bi, ai: (gi, 0, 0, 0)),
                  pl.BlockSpec((1, 1, seq + WINDOW, dh), per_bg),
                  pl.BlockSpec((1, 1, dh, seq + WINDOW), per_bg),
                  pl.BlockSpec((1, 1, WINDOW + Q_TILE, ROWS),
                               lambda gi, bi, ai: (gi, jnp.minimum(ai, WINDOW // Q_TILE), 0, 0))],
        out_specs=pl.BlockSpec((Q_TILE, gw), lambda gi, bi, ai: (bi * n_a + ai, gi)),
        out_shape=jax.ShapeDtypeStruct((n_batch * seq, N_HEADS * HEAD_DIM), BF16),
        scratch_shapes=[pltpu.VMEM((Q_TILE, Q_TILE), F32),
                        pltpu.VMEM((dh, ROWS), F32),
                        pltpu.VMEM((n_sel, Q_TILE), F32), pltpu.VMEM((n_sel, Q_TILE), F32),
                        pltpu.VMEM((n_sel + dh, ROWS), BF16), pltpu.VMEM((1, ROWS), F32),
                        pltpu.VMEM((V_ROWS, ROWS), F32)],
        compiler_params=_cparams(("parallel", "parallel", "arbitrary")),
        name="nsa_attention",
    )(proj, proj, k_cmp, vt_cmp, tbl_c, smat, ka, vst, tbl_s, kwp, vwtp, tbl_w)


def _rel_bucket(dist):
    n = jnp.maximum(dist, 0)
    nf = jnp.maximum(n, 1).astype(F32)
    large = RPB_MAX_EXACT + (jnp.log(nf / RPB_MAX_EXACT) / math.log(RPB_MAX_DIST / RPB_MAX_EXACT)
                             * (N_BUCKETS - RPB_MAX_EXACT)).astype(jnp.int32)
    large = jnp.minimum(large, N_BUCKETS - 1)
    return jnp.where(n < RPB_MAX_EXACT, n, large)


def _toeplitz(e, n_rows):
    w = e.shape[-1]
    lead = e.shape[:-1]
    flat = jnp.broadcast_to(e[..., None, :], lead + (n_rows, w)).reshape(lead + (n_rows * w,))
    return flat[..., :n_rows * (w - 1)].reshape(lead + (n_rows, w - 1))


def _bias_tables(rpb_table):
    tab = rpb_table.astype(F32)
    by_dist = tab[_rel_bucket(jnp.arange(FAR_DIST))].T
    rel = by_dist - tab[N_BUCKETS - 1][:, None]
    neg = lambda n: jnp.full((N_HEADS, n), NEG, F32)
    zero = lambda n: jnp.zeros((N_HEADS, n), F32)

    def to_rows(m):
        m = m.reshape((N_KV, GQA_R) + m.shape[1:])
        m = jnp.moveaxis(m, 1, -2)
        return m.reshape(m.shape[:-2] + (ROWS,))

    tk = SEL_KEY_TILE
    ext = jnp.concatenate([neg(tk), rel], axis=1)
    win = jnp.stack([ext[:, Q_TILE * k + 1:Q_TILE * k + 1 + tk + Q_TILE] for k in range(SEL_NEAR_TILES - 1)]
                    + [neg(tk + Q_TILE)], axis=1)
    tbl_s = to_rows(_toeplitz(win, tk)[..., tk - 1:tk - 1 + Q_TILE]) * LOG2E

    span = WINDOW + Q_TILE
    ext = jnp.concatenate([neg(Q_TILE - 1), by_dist[:, :WINDOW], neg(Q_TILE + 1)], axis=1)
    tbl_w = to_rows(_toeplitz(ext, span)[..., span - 1:span - 1 + Q_TILE]) * LOG2E
    first_key = WINDOW - Q_TILE * jnp.arange(WINDOW // Q_TILE + 1)
    in_seq = jnp.arange(span)[None, :] >= first_key[:, None]
    tbl_w = jnp.where(in_seq[None, :, :, None], tbl_w[:, None], NEG)

    n_m = 2 * CMP_LANE
    n_tau = FAR_DIST // CMP_STRIDE
    base = CMP_STRIDE * (n_m - 1) - (FAR_DIST - CMP_BLOCK + 1)
    length = CMP_STRIDE * (n_m + n_tau)
    ext = jnp.concatenate([neg(base), rel, zero(length - base - FAR_DIST)], axis=1)
    ev = ext.reshape(N_HEADS, n_m + n_tau, CMP_STRIDE).transpose(0, 2, 1)
    tz = _toeplitz(ev, n_m)[..., n_m - 1:n_m - 1 + n_tau]
    tz = tz.transpose(0, 2, 3, 1).reshape(N_HEADS, n_m, FAR_DIST // Q_TILE, Q_TILE)
    tbl_c = to_rows(tz.transpose(0, 2, 1, 3)) * LOG2E
    return tbl_c, tbl_s, tbl_w


def _slc_matrix(n_cmp_pad, n_sel):
    sel_ratio, cmp_ratio = SEL_BLOCK // CMP_STRIDE, CMP_BLOCK // CMP_STRIDE
    mat = np.zeros((n_sel, n_cmp_pad), np.float32)
    for j in range(n_sel):
        for mm in range(sel_ratio):
            for nn in range(cmp_ratio):
                idx = sel_ratio * j - mm - nn
                if 0 <= idx < n_cmp_pad - 1:
                    mat[j, idx] += 1.0
    return jnp.asarray(mat, BF16)


def nsa_mixer(proj, n_batch, seq, cmp_k_w, cmp_v_w, tables, smat):
    kv = N_KV * HEAD_DIM
    rows16 = lambda col: proj[:, col:col + kv].reshape(n_batch, seq // CMP_STRIDE, CMP_STRIDE * kv)
    k_cmp = compress(rows16(COL_KC), *cmp_k_w)
    v_cmp = compress(rows16(COL_VC), *cmp_v_w)
    k_cmp = k_cmp.reshape(n_batch, -1, N_KV, HEAD_DIM).transpose(0, 2, 1, 3)
    vt_cmp = v_cmp.reshape(n_batch, -1, N_KV, HEAD_DIM).transpose(0, 2, 3, 1)
    ka, vst, kwp, vwtp = kv_prep(proj, n_batch, seq)
    return nsa_attention(proj, n_batch, seq, k_cmp, vt_cmp, ka, vst, kwp, vwtp, tables, smat)


def kernel(x, rpb_table, attn_norm, ffn_norm, final_norm, w_in, ssm_a_re, ssm_a_im, ssm_log_dt,
           ssm_b_re, ssm_b_im, ssm_c_re, ssm_c_im, ssm_d, w_glu, b_glu, cmp_pos_k, cmp_w1_k, cmp_w2_k,
           cmp_pos_v, cmp_w1_v, cmp_w2_v, w_out, w_ffn_gate, w_ffn_up, w_ffn_down):
    n_batch, seq, d_model = x.shape
    depth = w_in.shape[0]
    tables = _bias_tables(rpb_table)
    smat = _slc_matrix(seq // CMP_STRIDE, seq // SEL_BLOCK)

    xf = x.reshape(n_batch * seq, d_model)
    for l in range(depth):
        w_in_l = jnp.pad(w_in[l], ((0, 0), (0, IN_COLS_PAD - IN_COLS))).astype(BF16)
        proj = norm_matmul(xf, attn_norm[l].reshape(1, -1), w_in_l)
        s5w = _s5_weights(ssm_a_re[l], ssm_a_im[l], ssm_log_dt[l], ssm_b_re[l], ssm_b_im[l],
                          ssm_c_re[l], ssm_c_im[l])
        y_ssm = s5_glu(proj, n_batch, *s5w, ssm_d[l].reshape(1, -1), w_glu[l].astype(BF16),
                       b_glu[l].reshape(1, -1))
        y_nsa = nsa_mixer(proj, n_batch, seq,
                          _compress_weights(cmp_pos_k[l], cmp_w1_k[l], cmp_w2_k[l]),
                          _compress_weights(cmp_pos_v[l], cmp_w1_v[l], cmp_w2_v[l]),
                          tables, smat)
        w_out_l = w_out[l].astype(BF16)
        xf = out_proj(xf, y_ssm, y_nsa, w_out_l[:D_SSM], w_out_l[D_SSM:])
        xf = ffn(xf, ffn_norm[l].reshape(1, -1), w_ffn_gate[l].astype(BF16),
                 w_ffn_up[l].astype(BF16), w_ffn_down[l].astype(BF16))
    out = final_rms_norm(xf, final_norm.reshape(1, -1))
    return out.reshape(n_batch, seq, d_model)
```

```python
import math

import jax
import jax.numpy as jnp
import numpy as np
from jax import lax
from jax.experimental import pallas as pl
from jax.experimental.pallas import tpu as pltpu

F32 = jnp.float32
BF16 = jnp.bfloat16

D_MODEL = 2048
DEPTH = 4
D_SSM = 1024
SSM_GROUP = 16
N_SSM_GROUPS = 64
SSM_STATE = 64
N_STATE = N_SSM_GROUPS * SSM_STATE
SSM_PACK = 16
N_PACKS = N_SSM_GROUPS // SSM_PACK
N_HEADS = 16
N_KV = 4
GQA_R = 4
HEAD_DIM = 64
CMP_BLOCK = 32
CMP_STRIDE = 16
CMP_HIDDEN = 128
SEL_BLOCK = 64
SEL_TOP = 16
WINDOW = 512
Q_TILE = 128
ROWS = GQA_R * Q_TILE
FORCED_SCORE = 1e4
N_BUCKETS = 32
RPB_MAX_EXACT = 16
RPB_MAX_DIST = 1024
D_FF = 5632
NORM_EPS = 1e-6
IN_COLS = 3632
IN_COLS_PAD = 3840
COL_U, COL_Q, COL_KC, COL_VC, COL_KS, COL_VS, COL_KW, COL_VW, COL_GATE = (
    0, 1024, 2048, 2304, 2560, 2816, 3072, 3328, 3584)
NEG = -1e30
ONEHOT_BIG = 1e30
FAR_DIST = 2048
SEL_KEY_TILE = 256
SEL_NEAR_TILES = 12
LOG2E = math.log2(math.e)
SAFE_LOG2_RANGE = 64.0
CMP_LANE = 128
V_ROWS = 80
VMEM_LIMIT = 56 * 1024 * 1024


def _cparams(sem):
    return pltpu.CompilerParams(dimension_semantics=sem, vmem_limit_bytes=VMEM_LIMIT)


def _rms(x, g):
    ms = jnp.mean(x * x, axis=-1, keepdims=True)
    return x * lax.rsqrt(ms + NORM_EPS) * g


def _norm_matmul_kernel(x_ref, g_ref, w_ref, o_ref, h_ref):
    @pl.when(pl.program_id(1) == 0)
    def _():
        h_ref[...] = _rms(x_ref[...], g_ref[...]).astype(BF16)

    o_ref[...] = jnp.dot(h_ref[...], w_ref[...], preferred_element_type=F32)


def norm_matmul(x, g, w, *, tm=1024, tn=768):
    m, k = x.shape
    n = w.shape[1]
    return pl.pallas_call(
        _norm_matmul_kernel,
        grid=(m // tm, n // tn),
        in_specs=[pl.BlockSpec((tm, k), lambda i, j: (i, 0)),
                  pl.BlockSpec((1, k), lambda i, j: (0, 0)),
                  pl.BlockSpec((k, tn), lambda i, j: (0, j))],
        out_specs=pl.BlockSpec((tm, tn), lambda i, j: (i, j)),
        out_shape=jax.ShapeDtypeStruct((m, n), F32),
        scratch_shapes=[pltpu.VMEM((tm, k), BF16)],
        compiler_params=_cparams(("parallel", "arbitrary")),
        name="norm_matmul",
    )(x, g, w)


def _out_proj_kernel(x_ref, ys_ref, yn_ref, w1_ref, w2_ref, o_ref):
    o_ref[...] = (x_ref[...]
                  + jnp.dot(ys_ref[...], w1_ref[...], preferred_element_type=F32)
                  + jnp.dot(yn_ref[...], w2_ref[...], preferred_element_type=F32))


def out_proj(x, y_ssm, y_nsa, w_ssm, w_nsa, *, tm=512):
    m, d = x.shape
    k = y_ssm.shape[1]
    row = lambda i: (i, 0)
    full = lambda i: (0, 0)
    return pl.pallas_call(
        _out_proj_kernel,
        grid=(m // tm,),
        in_specs=[pl.BlockSpec((tm, d), row), pl.BlockSpec((tm, k), row), pl.BlockSpec((tm, k), row),
                  pl.BlockSpec((k, d), full), pl.BlockSpec((k, d), full)],
        out_specs=pl.BlockSpec((tm, d), row),
        out_shape=jax.ShapeDtypeStruct((m, d), F32),
        compiler_params=_cparams(("parallel",)),
        name="out_proj",
    )(x, y_ssm, y_nsa, w_ssm, w_nsa)


def _ffn_kernel(x_ref, g_ref, wg_ref, wu_ref, wd_ref, o_ref, h_ref):
    f = pl.program_id(1)

    @pl.when(f == 0)
    def _():
        x = x_ref[...]
        h_ref[...] = _rms(x, g_ref[...]).astype(BF16)
        o_ref[...] = x

    h = h_ref[...]
    gate = jnp.dot(h, wg_ref[...], preferred_element_type=F32)
    up = jnp.dot(h, wu_ref[...], preferred_element_type=F32)
    act = (jax.nn.silu(gate) * up).astype(BF16)
    o_ref[...] += jnp.dot(act, wd_ref[...], preferred_element_type=F32)


def ffn(x, g, w_gate, w_up, w_down, *, tm=1024, tf=256):
    m, d = x.shape
    dff = w_gate.shape[1]
    return pl.pallas_call(
        _ffn_kernel,
        grid=(m // tm, dff // tf),
        in_specs=[pl.BlockSpec((tm, d), lambda i, f: (i, 0)),
                  pl.BlockSpec((1, d), lambda i, f: (0, 0)),
                  pl.BlockSpec((d, tf), lambda i, f: (0, f)),
                  pl.BlockSpec((d, tf), lambda i, f: (0, f)),
                  pl.BlockSpec((tf, d), lambda i, f: (f, 0))],
        out_specs=pl.BlockSpec((tm, d), lambda i, f: (i, 0)),
        out_shape=jax.ShapeDtypeStruct((m, d), F32),
        scratch_shapes=[pltpu.VMEM((tm, d), BF16)],
        compiler_params=_cparams(("parallel", "arbitrary")),
        name="ffn",
    )(x, g, w_gate, w_up, w_down)


def _final_norm_kernel(x_ref, g_ref, o_ref):
    o_ref[...] = _rms(x_ref[...], g_ref[...])


def final_rms_norm(x, g, *, tm=512):
    m, d = x.shape
    return pl.pallas_call(
        _final_norm_kernel,
        grid=(m // tm,),
        in_specs=[pl.BlockSpec((tm, d), lambda i: (i, 0)), pl.BlockSpec((1, d), lambda i: (0, 0))],
        out_specs=pl.BlockSpec((tm, d), lambda i: (i, 0)),
        out_shape=jax.ShapeDtypeStruct((m, d), F32),
        compiler_params=_cparams(("parallel",)),
        name="final_norm",
    )(x, g)


def _s5_discretize_kernel(are_ref, aim_ref, ldt_ref, bre_ref, bim_ref,
                          abre_ref, abim_ref, bbre_ref, bbim_ref):
    lam_re = jnp.minimum(are_ref[...], -1e-4)
    lam_im = aim_ref[...]
    dt = jnp.exp(ldt_ref[...])
    mag = jnp.exp(lam_re * dt)
    ab_re = mag * jnp.cos(lam_im * dt)
    ab_im = mag * jnp.sin(lam_im * dt)
    nr, ni = ab_re - 1.0, ab_im
    den = lam_re * lam_re + lam_im * lam_im
    f_re = (nr * lam_re + ni * lam_im) / den
    f_im = (ni * lam_re - nr * lam_im) / den
    br, bi = bre_ref[...], bim_ref[...]
    abre_ref[...] = ab_re
    abim_ref[...] = ab_im
    bbre_ref[...] = f_re * br - f_im * bi
    bbim_ref[...] = f_re * bi + f_im * br


def s5_discretize(a_re, a_im, log_dt, b_re_t, b_im_t):
    g, p = a_re.shape
    h = b_re_t.shape[1]
    outs = pl.pallas_call(
        _s5_discretize_kernel,
        out_shape=[jax.ShapeDtypeStruct((g, 1, p), F32), jax.ShapeDtypeStruct((g, 1, p), F32),
                   jax.ShapeDtypeStruct((g, h, p), F32), jax.ShapeDtypeStruct((g, h, p), F32)],
        name="s5_discretize",
    )(a_re.reshape(g, 1, p), a_im.reshape(g, 1, p), log_dt.reshape(g, 1, 1), b_re_t, b_im_t)
    return outs


def _s5_kernel(u_ref, bre_ref, bim_ref, cre_ref, cimn_ref, a_ref, d_ref, wglu_ref, bglu_ref,
               o_ref, xr_ref, xi_ref, st_ref):
    tc = u_ref.shape[0]
    pw = SSM_PACK * SSM_GROUP
    sw = SSM_PACK * SSM_STATE

    @pl.when(pl.program_id(1) == 0)
    def _():
        st_ref[...] = jnp.zeros_like(st_ref)

    u = u_ref[...]
    ub = u.astype(BF16)
    for k in range(N_PACKS):
        uk = ub[:, k * pw:(k + 1) * pw]
        xr_ref[:, k * sw:(k + 1) * sw] = jnp.dot(uk, bre_ref[k], preferred_element_type=F32)
        xi_ref[:, k * sw:(k + 1) * sw] = jnp.dot(uk, bim_ref[k], preferred_element_type=F32)

    a_re = a_ref[0:1, :]
    a_im = a_ref[1:2, :]

    def step(t, state):
        s_re, s_im = state
        n_re = a_re * s_re - a_im * s_im + xr_ref[pl.ds(t, 1), :]
        n_im = a_re * s_im + a_im * s_re + xi_ref[pl.ds(t, 1), :]
        xr_ref[pl.ds(t, 1), :] = n_re
        xi_ref[pl.ds(t, 1), :] = n_im
        return n_re, n_im

    s_re, s_im = lax.fori_loop(0, tc, step, (st_ref[0:1, :], st_ref[1:2, :]), unroll=4)
    st_ref[0:1, :] = s_re
    st_ref[1:2, :] = s_im

    ys = []
    for k in range(N_PACKS):
        xr = xr_ref[:, k * sw:(k + 1) * sw].astype(BF16)
        xi = xi_ref[:, k * sw:(k + 1) * sw].astype(BF16)
        ys.append(jnp.dot(xr, cre_ref[k], preferred_element_type=F32)
                  + jnp.dot(xi, cimn_ref[k], preferred_element_type=F32))
    y = jnp.concatenate(ys, axis=-1) + d_ref[...] * u
    y = jax.nn.gelu(y).astype(BF16)
    ab = jnp.dot(y, wglu_ref[...], preferred_element_type=F32) + bglu_ref[...]
    o_ref[...] = (ab[:, :D_SSM] * jax.nn.sigmoid(ab[:, D_SSM:])).astype(o_ref.dtype)


def s5_glu(proj, n_batch, bbd_re, bbd_im, cbd_re, cbd_imn, a_flat, d, w_glu, b_glu, *, tc=256):
    t = proj.shape[0]
    n_chunks = t // n_batch // tc
    c3 = lambda b, c: (0, 0, 0)
    c2 = lambda b, c: (0, 0)
    return pl.pallas_call(
        _s5_kernel,
        grid=(n_batch, n_chunks),
        in_specs=[pl.BlockSpec((tc, D_SSM), lambda b, c: (b * n_chunks + c, 0)),
                  pl.BlockSpec(bbd_re.shape, c3), pl.BlockSpec(bbd_im.shape, c3),
                  pl.BlockSpec(cbd_re.shape, c3), pl.BlockSpec(cbd_imn.shape, c3),
                  pl.BlockSpec(a_flat.shape, c2), pl.BlockSpec(d.shape, c2),
                  pl.BlockSpec(w_glu.shape, c2), pl.BlockSpec(b_glu.shape, c2)],
        out_specs=pl.BlockSpec((tc, D_SSM), lambda b, c: (b * n_chunks + c, 0)),
        out_shape=jax.ShapeDtypeStruct((t, D_SSM), BF16),
        scratch_shapes=[pltpu.VMEM((tc, N_STATE), F32), pltpu.VMEM((tc, N_STATE), F32),
                        pltpu.VMEM((2, N_STATE), F32)],
        compiler_params=_cparams(("arbitrary", "arbitrary")),
        name="s5_glu",
    )(proj, bbd_re, bbd_im, cbd_re, cbd_imn, a_flat, d, w_glu, b_glu)


def _s5_weights(a_re, a_im, log_dt, b_re, b_im, c_re, c_im):
    ab_re, ab_im, bb_re, bb_im = s5_discretize(a_re, a_im, log_dt,
                                               b_re.transpose(0, 2, 1), b_im.transpose(0, 2, 1))
    eye = jnp.eye(SSM_PACK, dtype=F32)

    def pack_b(bb):
        bb = bb.reshape(N_PACKS, SSM_PACK, SSM_GROUP, 1, SSM_STATE)
        m = bb * eye[None, :, None, :, None]
        return m.reshape(N_PACKS, SSM_PACK * SSM_GROUP, SSM_PACK * SSM_STATE).astype(BF16)

    def pack_c(c):
        c = c.reshape(N_PACKS, SSM_PACK, SSM_GROUP, SSM_STATE)
        m = c.transpose(0, 1, 3, 2)[:, :, :, None, :] * eye[None, :, None, :, None]
        return m.reshape(N_PACKS, SSM_PACK * SSM_STATE, SSM_PACK * SSM_GROUP).astype(BF16)

    a_flat = jnp.stack([ab_re.reshape(-1), ab_im.reshape(-1)])
    return pack_b(bb_re), pack_b(bb_im), pack_c(c_re), pack_c(-c_im), a_flat


def _compress_kernel(r_ref, n_ref, pos_ref, w1a_ref, w1b_ref, w2_ref, o_ref):
    r = r_ref[0]
    rows = r.shape[0]
    rowid = lax.broadcasted_iota(jnp.int32, (rows, 1), 0)
    nxt = jnp.where(rowid == rows - 1, n_ref[0, 0:1, :], pltpu.roll(r, rows - 1, 0))
    first = (r + pos_ref[0:1, :]).astype(BF16)
    second = (nxt + pos_ref[1:2, :]).astype(BF16)
    hid = (jnp.dot(first, w1a_ref[...], preferred_element_type=F32)
           + jnp.dot(second, w1b_ref[...], preferred_element_type=F32))
    hid = jax.nn.gelu(hid).astype(BF16)
    o_ref[0] = jnp.dot(hid, w2_ref[...], preferred_element_type=F32).astype(o_ref.dtype)


def compress(kc_rows, pos2, w1a, w1b, w2e):
    b, nc, w = kc_rows.shape
    tr = 128
    sub = 8
    full = lambda i, j: (0, 0)
    nxt_blk = lambda i, j: (i, jnp.minimum((j + 1) * (tr // sub), nc // sub - 1), 0)
    return pl.pallas_call(
        _compress_kernel,
        grid=(b, nc // tr),
        in_specs=[pl.BlockSpec((1, tr, w), lambda i, j: (i, j, 0)),
                  pl.BlockSpec((1, sub, w), nxt_blk),
                  pl.BlockSpec(pos2.shape, full), pl.BlockSpec(w1a.shape, full),
                  pl.BlockSpec(w1b.shape, full), pl.BlockSpec(w2e.shape, full)],
        out_specs=pl.BlockSpec((1, tr, N_KV * HEAD_DIM), lambda i, j: (i, j, 0)),
        out_shape=jax.ShapeDtypeStruct((b, nc, N_KV * HEAD_DIM), BF16),
        compiler_params=_cparams(("parallel", "parallel")),
        name="compress",
    )(kc_rows, kc_rows, pos2, w1a, w1b, w2e)


def _compress_weights(pos, w1, w2):
    eye = jnp.eye(N_KV, dtype=F32)
    w1 = w1.reshape(2, CMP_STRIDE, 1, HEAD_DIM, 1, CMP_HIDDEN)
    w1e = (w1 * eye[None, None, :, None, :, None]).reshape(
        2, CMP_STRIDE * N_KV * HEAD_DIM, N_KV * CMP_HIDDEN).astype(BF16)
    w2e = (w2[None, :, None, :] * eye[:, None, :, None]).reshape(
        N_KV * CMP_HIDDEN, N_KV * HEAD_DIM).astype(BF16)
    pos2 = jnp.broadcast_to(pos.reshape(2, CMP_STRIDE, 1, HEAD_DIM),
                            (2, CMP_STRIDE, N_KV, HEAD_DIM)).reshape(2, -1)
    return pos2, w1e[0], w1e[1], w2e


def _kv_prep_kernel(ks_ref, vs_ref, kw_ref, vw_ref, ka_ref, vst_ref, kwp_ref, vwtp_ref):
    j = pl.program_id(1)
    ts = ks_ref.shape[0]
    n_sel = ka_ref.shape[3] - HEAD_DIM
    blk0 = jnp.maximum(j - 1, 0) * (ts // SEL_BLOCK)
    row_blk = blk0 + lax.broadcasted_iota(jnp.int32, (ts, n_sel), 0) // SEL_BLOCK
    col = lax.broadcasted_iota(jnp.int32, (ts, n_sel), 1)
    onehot = jnp.where(row_blk == col, ONEHOT_BIG, 0.0).astype(BF16)
    ks = ks_ref[...]
    vs_t = vs_ref[...].T
    pad_rows = lax.broadcasted_iota(jnp.int32, (V_ROWS - HEAD_DIM, ts), 0)
    extra = jnp.where(pad_rows == 0, 1.0, 0.0).astype(BF16)
    for g in range(N_KV):
        lo, hi = g * HEAD_DIM, (g + 1) * HEAD_DIM
        ka_ref[0, g, :, 0:n_sel] = onehot
        ka_ref[0, g, :, n_sel:] = ks[:, lo:hi].astype(BF16)
        vst_ref[0, g, 0:HEAD_DIM, :] = vs_t[lo:hi].astype(BF16)
        vst_ref[0, g, HEAD_DIM:, :] = extra

    @pl.when(j == 0)
    def _():
        kwp_ref[...] = jnp.zeros_like(kwp_ref)
        vwtp_ref[...] = jnp.zeros_like(vwtp_ref)

    @pl.when(j > 0)
    def _():
        kw = kw_ref[...]
        vw_t = vw_ref[...].T
        for g in range(N_KV):
            lo, hi = g * HEAD_DIM, (g + 1) * HEAD_DIM
            kwp_ref[0, g] = kw[:, lo:hi].astype(BF16)
            vwtp_ref[0, g] = vw_t[lo:hi].astype(BF16)


def kv_prep(proj, n_batch, seq):
    ts = WINDOW
    nblk = seq // ts
    n_sel = seq // SEL_BLOCK
    kvw = N_KV * HEAD_DIM
    src = lambda col: pl.BlockSpec((ts, kvw), lambda b, j: (b * nblk + jnp.maximum(j - 1, 0), col // kvw))
    same = lambda b, j: (b, 0, jnp.maximum(j - 1, 0), 0)
    same_t = lambda b, j: (b, 0, 0, jnp.maximum(j - 1, 0))
    return pl.pallas_call(
        _kv_prep_kernel,
        grid=(n_batch, nblk + 1),
        in_specs=[src(COL_KS), src(COL_VS), src(COL_KW), src(COL_VW)],
        out_specs=[pl.BlockSpec((1, N_KV, ts, n_sel + HEAD_DIM), same),
                   pl.BlockSpec((1, N_KV, V_ROWS, ts), same_t),
                   pl.BlockSpec((1, N_KV, ts, HEAD_DIM), lambda b, j: (b, 0, j, 0)),
                   pl.BlockSpec((1, N_KV, HEAD_DIM, ts), lambda b, j: (b, 0, 0, j))],
        out_shape=[jax.ShapeDtypeStruct((n_batch, N_KV, seq, n_sel + HEAD_DIM), BF16),
                   jax.ShapeDtypeStruct((n_batch, N_KV, V_ROWS, seq), BF16),
                   jax.ShapeDtypeStruct((n_batch, N_KV, seq + WINDOW, HEAD_DIM), BF16),
                   jax.ShapeDtypeStruct((n_batch, N_KV, HEAD_DIM, seq + WINDOW), BF16)],
        compiler_params=_cparams(("parallel", "arbitrary")),
        name="kv_prep",
    )(proj, proj, proj, proj)


def _softmax_cols(s):
    m = jnp.max(s, axis=0, keepdims=True)
    m = jnp.where(m < 0.5 * NEG, 0.0, m)
    e = jnp.exp2(s - m)
    return e * (1.0 / jnp.maximum(jnp.sum(e, axis=0, keepdims=True), 1e-30))


def _cmp_branch(a, qt, gate, k_ref, vt_ref, tbl_ref, smat_ref, o_ref, st_ref, rank_ref):
    n_row_tiles = k_ref.shape[2] // CMP_LANE
    ap = a // (FAR_DIST // Q_TILE)

    def attend(n):
        rows = n * CMP_LANE
        s = jnp.dot(k_ref[0, 0, 0:rows, :], qt, preferred_element_type=F32)
        tbl = tbl_ref[0, 0]
        pieces = [s[c * CMP_LANE:(c + 1) * CMP_LANE] for c in range(n)]
        pieces[-1] = pieces[-1] + tbl[CMP_LANE:]
        if n > 1:
            pieces[-2] = pieces[-2] + tbl[:CMP_LANE]
        p = _softmax_cols(jnp.concatenate(pieces, axis=0))
        o_ref[...] = gate * jnp.dot(vt_ref[0, 0, :, 0:rows], p.astype(BF16), preferred_element_type=F32)
        imp = (p[:, 0:Q_TILE] + p[:, Q_TILE:2 * Q_TILE] + p[:, 2 * Q_TILE:3 * Q_TILE]
               + p[:, 3 * Q_TILE:4 * Q_TILE])
        hi = imp.astype(BF16)
        r1 = imp - hi.astype(F32)
        mid = r1.astype(BF16)
        lo = (r1 - mid.astype(F32)).astype(BF16)
        smat = smat_ref[:, 0:rows]
        st_ref[...] = (jnp.dot(smat, hi, preferred_element_type=F32)
                       + jnp.dot(smat, mid, preferred_element_type=F32)
                       + jnp.dot(smat, lo, preferred_element_type=F32))

    for n in range(1, n_row_tiles + 1):
        pl.when(ap == n - 1)(lambda n=n: attend(n))

    o = o_ref[...]
    st = st_ref[...]
    t = a * Q_TILE + lax.broadcasted_iota(jnp.int32, st.shape, 1)
    jj = lax.broadcasted_iota(jnp.int32, st.shape, 0)
    cur_blk = t // SEL_BLOCK
    valid = jj * SEL_BLOCK <= t
    forced = (jj == 0) | (jj == cur_blk) | (jj == cur_blk - 1)
    st = jnp.where(valid, jnp.where(forced, FORCED_SCORE, st), -1.0)
    st_ref[...] = st
    rank_ref[...] = jnp.zeros_like(rank_ref)
    sub = 8
    chunk = 16
    n_sel = st.shape[0]
    last_valid = (a * Q_TILE + Q_TILE - 1) // SEL_BLOCK
    for q0 in range(0, n_sel, chunk):
        @pl.when(q0 <= last_valid)
        def _(q0=q0):
            for v0 in range(0, n_sel, chunk):
                @pl.when(v0 <= last_valid)
                def _(v0=v0):
                    for v in range(v0 // sub, min(v0 + chunk, n_sel) // sub):
                        sv = st_ref[v * sub:(v + 1) * sub, :]
                        jv = jj[v * sub:(v + 1) * sub]
                        cnt = rank_ref[v * sub:(v + 1) * sub, :]
                        for jp in range(q0, min(q0 + chunk, n_sel)):
                            row = st_ref[jp:jp + 1, :]
                            if jp < v * sub:
                                ahead = row >= sv
                            elif jp >= (v + 1) * sub:
                                ahead = row > sv
                            else:
                                ahead = (row > sv) | ((row == sv) & (jv > jp))
                            cnt = cnt + jnp.where(ahead, 1.0, 0.0)
                        rank_ref[v * sub:(v + 1) * sub, :] = cnt

    keep = (rank_ref[...] < SEL_TOP) & (st >= 0.0)
    return o, (keep.astype(F32) - 1.0).astype(BF16)


def _sel_branch(a, gate, ka_ref, vt_ref, tbl_ref, qa_ref, m_ref, acc_ref):
    tk = SEL_KEY_TILE
    acc_ref[...] = jnp.zeros_like(acc_ref)
    n_tiles = a // (tk // Q_TILE) + 1
    far_pairs = jnp.maximum(0, (a - (SEL_NEAR_TILES - 5)) // 4)
    far_octs = far_pairs // 4
    far_quads = (far_pairs - 4 * far_octs) // 2
    first_near = 2 * far_pairs
    near_quads = (n_tiles - first_near) // 4
    near_pairs = (n_tiles - first_near - 4 * near_quads + 1) // 2

    def key_off(c):
        return pl.multiple_of(jnp.minimum(c, n_tiles - 1) * tk, tk)

    def attend(c0, n, biased):
        m_old = m_ref[...]
        acc_old = acc_ref[...]

        whole = n >= 4
        span = pl.ds(pl.multiple_of(c0 * tk, tk), n * tk)

        def scores():
            if whole:
                s_all = jnp.dot(ka_ref[0, 0, span, :], qa_ref[...], preferred_element_type=F32)
                ss = [s_all[h * tk:(h + 1) * tk] for h in range(n)]
            else:
                ss = [jnp.dot(ka_ref[0, 0, pl.ds(key_off(c0 + h), tk), :], qa_ref[...],
                              preferred_element_type=F32) for h in range(n)]
            if biased:
                ss = [s + tbl_ref[0, jnp.where(c0 + h < n_tiles, a - 2 * (c0 + h), SEL_NEAR_TILES - 1)]
                      for h, s in enumerate(ss)]
            return ss

        def weighted(ss, m):
            ps = [jnp.exp2(s - m).astype(BF16) for s in ss]
            if whole:
                return jnp.dot(vt_ref[0, 0, :, span], jnp.concatenate(ps, axis=0), preferred_element_type=F32)
            return sum(jnp.dot(vt_ref[0, 0, :, pl.ds(key_off(c0 + h), tk)], p, preferred_element_type=F32)
                       for h, p in enumerate(ps))

        ss = scores()
        m_new = m_old
        for s in ss:
            m_new = jnp.maximum(m_new, jnp.max(s, axis=0, keepdims=True))
        acc_ref[...] = (acc_old + weighted(ss, m_old)) * jnp.exp2(m_old - m_new)
        m_ref[...] = m_new

        @pl.when(jnp.max(m_new - m_old) > SAFE_LOG2_RANGE)
        def _():
            acc_ref[...] = jnp.exp2(m_old - m_new) * acc_old + weighted(scores(), m_new)

    def body(first, n, biased):
        def step(i, carry):
            attend(first + n * i, n, biased)
            return carry
        return step

    sub = 8
    m_ref[...] = jnp.dot(ka_ref[0, 0, 0:sub, :], qa_ref[...], preferred_element_type=F32)[0:1, :]
    lax.fori_loop(0, far_octs, body(0, 8, False), 0)
    lax.fori_loop(0, far_quads, body(8 * far_octs, 4, False), 0)
    lax.fori_loop(0, far_pairs - 4 * far_octs - 2 * far_quads, body(8 * far_octs + 4 * far_quads, 2, False), 0)
    lax.fori_loop(0, near_quads, body(first_near, 4, True), 0)
    lax.fori_loop(0, near_pairs, body(first_near + 4 * near_quads, 2, True), 0)
    acc = acc_ref[...]
    return acc[0:HEAD_DIM] * (gate / acc[HEAD_DIM:HEAD_DIM + 1])


def _win_branch(a, qt, gate, k_ref, vt_ref, tbl_ref):
    span = WINDOW + Q_TILE
    off = pl.multiple_of(a * Q_TILE, Q_TILE)
    s = jnp.dot(k_ref[0, 0, pl.ds(off, span), :], qt, preferred_element_type=F32)
    s = s + tbl_ref[0, 0]
    e = jnp.exp2(s - jnp.max(s, axis=0, keepdims=True))
    ev = jnp.dot(vt_ref[0, 0, :, pl.ds(off, span)], e.astype(BF16), preferred_element_type=F32)
    return ev * (gate / jnp.sum(e, axis=0, keepdims=True))


def _nsa_kernel(q_ref, g_ref, kc_ref, vtc_ref, tblc_ref, smat_ref, ka_ref, vst_ref, tbls_ref,
                kw_ref, vwt_ref, tblw_ref, o_ref, gt_ref, oc_ref, st_ref, rank_ref, qa_ref, m_ref, acc_ref):
    g = pl.program_id(0)
    a = pl.program_id(2)
    n_sel = smat_ref.shape[0]
    half = 2 * HEAD_DIM
    q = q_ref[...] * (HEAD_DIM ** -0.5) * LOG2E
    t0 = q[:, :half].T
    t1 = q[:, half:].T
    qt = jnp.concatenate([t0[:HEAD_DIM], t0[HEAD_DIM:], t1[:HEAD_DIM], t1[HEAD_DIM:]],
                         axis=1).astype(BF16)
    gt_ref[...] = g_ref[:, :Q_TILE].T

    def gate(branch):
        rows = [gt_ref[pl.ds((g * GQA_R + r) * 3 + branch, 1), :] for r in range(GQA_R)]
        return jax.nn.sigmoid(jnp.concatenate(rows, axis=1))

    o, sel = _cmp_branch(a, qt, gate(0), kc_ref, vtc_ref, tblc_ref, smat_ref, oc_ref, st_ref, rank_ref)
    for r in range(GQA_R):
        qa_ref[0:n_sel, r * Q_TILE:(r + 1) * Q_TILE] = sel
    qa_ref[n_sel:, :] = qt
    o = o + _sel_branch(a, gate(1), ka_ref, vst_ref, tbls_ref, qa_ref, m_ref, acc_ref)
    o = o + _win_branch(a, qt, gate(2), kw_ref, vwt_ref, tblw_ref)
    u0 = jnp.concatenate([o[:, 0:Q_TILE], o[:, Q_TILE:2 * Q_TILE]], axis=0).T
    u1 = jnp.concatenate([o[:, 2 * Q_TILE:3 * Q_TILE], o[:, 3 * Q_TILE:]], axis=0).T
    o_ref[...] = jnp.concatenate([u0, u1], axis=1).astype(o_ref.dtype)


def nsa_attention(proj, n_batch, seq, k_cmp, vt_cmp, ka, vst, kwp, vwtp, tables, smat):
    tbl_c, tbl_s, tbl_w = tables
    n_a = seq // Q_TILE
    nc = k_cmp.shape[2]
    n_sel = smat.shape[0]
    dh = HEAD_DIM
    gw = GQA_R * HEAD_DIM
    tps = FAR_DIST // Q_TILE
    per_bg = lambda gi, bi, ai: (bi, gi, 0, 0)
    return pl.pallas_call(
        _nsa_kernel,
        grid=(N_KV, n_batch, n_a),
        in_specs=[pl.BlockSpec((Q_TILE, gw), lambda gi, bi, ai: (bi * n_a + ai, COL_Q // gw + gi)),
                  pl.BlockSpec((Q_TILE, gw), lambda gi, bi, ai: (bi * n_a + ai, COL_GATE // gw)),
                  pl.BlockSpec((1, 1, nc, dh), per_bg),
                  pl.BlockSpec((1, 1, dh, nc), per_bg),
                  pl.BlockSpec((1, 1, 2 * CMP_LANE, ROWS), lambda gi, bi, ai: (gi, ai % tps, 0, 0)),
                  pl.BlockSpec(smat.shape, lambda gi, bi, ai: (0, 0)),
                  pl.BlockSpec((1, 1, seq, n_sel + dh), per_bg),
                  pl.BlockSpec((1, 1, V_ROWS, seq), per_bg),
                  pl.BlockSpec((1, SEL_NEAR_TILES, SEL_KEY_TILE, ROWS), lambda gi, bi, ai: (gi, 0, 0, 0)),
                  pl.BlockSpec((1, 1, seq + WINDOW, dh), per_bg),
                  pl.BlockSpec((1, 1, dh, seq + WINDOW), per_bg),
                  pl.BlockSpec((1, 1, WINDOW + Q_TILE, ROWS),
                               lambda gi, bi, ai: (gi, jnp.minimum(ai, WINDOW // Q_TILE), 0, 0))],
        out_specs=pl.BlockSpec((Q_TILE, gw), lambda gi, bi, ai: (bi * n_a + ai, gi)),
        out_shape=jax.ShapeDtypeStruct((n_batch * seq, N_HEADS * HEAD_DIM), BF16),
        scratch_shapes=[pltpu.VMEM((Q_TILE, Q_TILE), F32),
                        pltpu.VMEM((dh, ROWS), F32),
                        pltpu.VMEM((n_sel, Q_TILE), F32), pltpu.VMEM((n_sel, Q_TILE), F32),
                        pltpu.VMEM((n_sel + dh, ROWS), BF16), pltpu.VMEM((1, ROWS), F32),
                        pltpu.VMEM((V_ROWS, ROWS), F32)],
        compiler_params=_cparams(("parallel", "parallel", "arbitrary")),
        name="nsa_attention",
    )(proj, proj, k_cmp, vt_cmp, tbl_c, smat, ka, vst, tbl_s, kwp, vwtp, tbl_w)


def _rel_bucket(dist):
    n = jnp.maximum(dist, 0)
    nf = jnp.maximum(n, 1).astype(F32)
    large = RPB_MAX_EXACT + (jnp.log(nf / RPB_MAX_EXACT) / math.log(RPB_MAX_DIST / RPB_MAX_EXACT)
                             * (N_BUCKETS - RPB_MAX_EXACT)).astype(jnp.int32)
    large = jnp.minimum(large, N_BUCKETS - 1)
    return jnp.where(n < RPB_MAX_EXACT, n, large)


def _toeplitz(e, n_rows):
    w = e.shape[-1]
    lead = e.shape[:-1]
    flat = jnp.broadcast_to(e[..., None, :], lead + (n_rows, w)).reshape(lead + (n_rows * w,))
    return flat[..., :n_rows * (w - 1)].reshape(lead + (n_rows, w - 1))


def _bias_tables(rpb_table):
    tab = rpb_table.astype(F32)
    by_dist = tab[_rel_bucket(jnp.arange(FAR_DIST))].T
    rel = by_dist - tab[N_BUCKETS - 1][:, None]
    neg = lambda n: jnp.full((N_HEADS, n), NEG, F32)
    zero = lambda n: jnp.zeros((N_HEADS, n), F32)

    def to_rows(m):
        m = m.reshape((N_KV, GQA_R) + m.shape[1:])
        m = jnp.moveaxis(m, 1, -2)
        return m.reshape(m.shape[:-2] + (ROWS,))

    tk = SEL_KEY_TILE
    ext = jnp.concatenate([neg(tk), rel], axis=1)
    win = jnp.stack([ext[:, Q_TILE * k + 1:Q_TILE * k + 1 + tk + Q_TILE] for k in range(SEL_NEAR_TILES - 1)]
                    + [neg(tk + Q_TILE)], axis=1)
    tbl_s = to_rows(_toeplitz(win, tk)[..., tk - 1:tk - 1 + Q_TILE]) * LOG2E

    span = WINDOW + Q_TILE
    ext = jnp.concatenate([neg(Q_TILE - 1), by_dist[:, :WINDOW], neg(Q_TILE + 1)], axis=1)
    tbl_w = to_rows(_toeplitz(ext, span)[..., span - 1:span - 1 + Q_TILE]) * LOG2E
    first_key = WINDOW - Q_TILE * jnp.arange(WINDOW // Q_TILE + 1)
    in_seq = jnp.arange(span)[None, :] >= first_key[:, None]
    tbl_w = jnp.where(in_seq[None, :, :, None], tbl_w[:, None], NEG)

    n_m = 2 * CMP_LANE
    n_tau = FAR_DIST // CMP_STRIDE
    base = CMP_STRIDE * (n_m - 1) - (FAR_DIST - CMP_BLOCK + 1)
    length = CMP_STRIDE * (n_m + n_tau)
    ext = jnp.concatenate([neg(base), rel, zero(length - base - FAR_DIST)], axis=1)
    ev = ext.reshape(N_HEADS, n_m + n_tau, CMP_STRIDE).transpose(0, 2, 1)
    tz = _toeplitz(ev, n_m)[..., n_m - 1:n_m - 1 + n_tau]
    tz = tz.transpose(0, 2, 3, 1).reshape(N_HEADS, n_m, FAR_DIST // Q_TILE, Q_TILE)
    tbl_c = to_rows(tz.transpose(0, 2, 1, 3)) * LOG2E
    return tbl_c, tbl_s, tbl_w


def _slc_matrix(n_cmp_pad, n_sel):
    sel_ratio, cmp_ratio = SEL_BLOCK // CMP_STRIDE, CMP_BLOCK // CMP_STRIDE
    mat = np.zeros((n_sel, n_cmp_pad), np.float32)
    for j in range(n_sel):
        for mm in range(sel_ratio):
            for nn in range(cmp_ratio):
                idx = sel_ratio * j - mm - nn
                if 0 <= idx < n_cmp_pad - 1:
                    mat[j, idx] += 1.0
    return jnp.asarray(mat, BF16)


def nsa_mixer(proj, n_batch, seq, cmp_k_w, cmp_v_w, tables, smat):
    kv = N_KV * HEAD_DIM
    rows16 = lambda col: proj[:, col:col + kv].reshape(n_batch, seq // CMP_STRIDE, CMP_STRIDE * kv)
    k_cmp = compress(rows16(COL_KC), *cmp_k_w)
    v_cmp = compress(rows16(COL_VC), *cmp_v_w)
    k_cmp = k_cmp.reshape(n_batch, -1, N_KV, HEAD_DIM).transpose(0, 2, 1, 3)
    vt_cmp = v_cmp.reshape(n_batch, -1, N_KV, HEAD_DIM).transpose(0, 2, 3, 1)
    ka, vst, kwp, vwtp = kv_prep(proj, n_batch, seq)
    return nsa_attention(proj, n_batch, seq, k_cmp, vt_cmp, ka, vst, kwp, vwtp, tables, smat)


def kernel(x, rpb_table, attn_norm, ffn_norm, final_norm, w_in, ssm_a_re, ssm_a_im, ssm_log_dt,
           ssm_b_re, ssm_b_im, ssm_c_re, ssm_c_im, ssm_d, w_glu, b_glu, cmp_pos_k, cmp_w1_k, cmp_w2_k,
           cmp_pos_v, cmp_w1_v, cmp_w2_v, w_out, w_ffn_gate, w_ffn_up, w_ffn_down):
    n_batch, seq, d_model = x.shape
    depth = w_in.shape[0]
    tables = _bias_tables(rpb_table)
    smat = _slc_matrix(seq // CMP_STRIDE, seq // SEL_BLOCK)

    xf = x.reshape(n_batch * seq, d_model)
    for l in range(depth):
        w_in_l = jnp.pad(w_in[l], ((0, 0), (0, IN_COLS_PAD - IN_COLS))).astype(BF16)
        proj = norm_matmul(xf, attn_norm[l].reshape(1, -1), w_in_l)
        s5w = _s5_weights(ssm_a_re[l], ssm_a_im[l], ssm_log_dt[l], ssm_b_re[l], ssm_b_im[l],
                          ssm_c_re[l], ssm_c_im[l])
        y_ssm = s5_glu(proj, n_batch, *s5w, ssm_d[l].reshape(1, -1), w_glu[l].astype(BF16),
                       b_glu[l].reshape(1, -1))
        y_nsa = nsa_mixer(proj, n_batch, seq,
                          _compress_weights(cmp_pos_k[l], cmp_w1_k[l], cmp_w2_k[l]),
                          _compress_weights(cmp_pos_v[l], cmp_w1_v[l], cmp_w2_v[l]),
                          tables, smat)
        w_out_l = w_out[l].astype(BF16)
        xf = out_proj(xf, y_ssm, y_nsa, w_out_l[:D_SSM], w_out_l[D_SSM:])
        xf = ffn(xf, ffn_norm[l].reshape(1, -1), w_ffn_gate[l].astype(BF16),
                 w_ffn_up[l].astype(BF16), w_ffn_down[l].astype(BF16))
    out = final_rms_norm(xf, final_norm.reshape(1, -1))
    return out.reshape(n_batch, seq, d_model)
```

```python
import math

import jax
import jax.numpy as jnp
import numpy as np
from jax import lax
from jax.experimental import pallas as pl
from jax.experimental.pallas import tpu as pltpu

F32 = jnp.float32
BF16 = jnp.bfloat16

D_MODEL = 2048
DEPTH = 4
D_SSM = 1024
SSM_GROUP = 16
N_SSM_GROUPS = 64
SSM_STATE = 64
N_STATE = N_SSM_GROUPS * SSM_STATE
SSM_PACK = 16
N_PACKS = N_SSM_GROUPS // SSM_PACK
N_HEADS = 16
N_KV = 4
GQA_R = 4
HEAD_DIM = 64
CMP_BLOCK = 32
CMP_STRIDE = 16
CMP_HIDDEN = 128
SEL_BLOCK = 64
SEL_TOP = 16
WINDOW = 512
Q_TILE = 128
ROWS = GQA_R * Q_TILE
FORCED_SCORE = 1e4
N_BUCKETS = 32
RPB_MAX_EXACT = 16
RPB_MAX_DIST = 1024
D_FF = 5632
NORM_EPS = 1e-6
IN_COLS = 3632
IN_COLS_PAD = 3840
COL_U, COL_Q, COL_KC, COL_VC, COL_KS, COL_VS, COL_KW, COL_VW, COL_GATE = (
    0, 1024, 2048, 2304, 2560, 2816, 3072, 3328, 3584)
NEG = -1e30
ONEHOT_BIG = 1e30
FAR_DIST = 2048
SEL_KEY_TILE = 256
SEL_NEAR_TILES = 12
LOG2E = math.log2(math.e)
SAFE_LOG2_RANGE = 64.0
CMP_LANE = 128
V_ROWS = 80
VMEM_LIMIT = 56 * 1024 * 1024


def _cparams(sem):
    return pltpu.CompilerParams(dimension_semantics=sem, vmem_limit_bytes=VMEM_LIMIT)


def _rms(x, g):
    ms = jnp.mean(x * x, axis=-1, keepdims=True)
    return x * lax.rsqrt(ms + NORM_EPS) * g


def _norm_matmul_kernel(x_ref, g_ref, w_ref, o_ref, h_ref):
    @pl.when(pl.program_id(1) == 0)
    def _():
        h_ref[...] = _rms(x_ref[...], g_ref[...]).astype(BF16)

    o_ref[...] = jnp.dot(h_ref[...], w_ref[...], preferred_element_type=F32)


def norm_matmul(x, g, w, layer, *, tm=1024, tn=768):
    m, k = x.shape
    n = w.shape[2]
    return pl.pallas_call(
        _norm_matmul_kernel,
        grid=(m // tm, n // tn),
        in_specs=[pl.BlockSpec((tm, k), lambda i, j: (i, 0)),
                  pl.BlockSpec((1, k), lambda i, j: (0, 0)),
                  pl.BlockSpec((None, k, tn), lambda i, j: (layer, 0, j))],
        out_specs=pl.BlockSpec((tm, tn), lambda i, j: (i, j)),
        out_shape=jax.ShapeDtypeStruct((m, n), F32),
        scratch_shapes=[pltpu.VMEM((tm, k), BF16)],
        compiler_params=_cparams(("parallel", "arbitrary")),
        name="norm_matmul",
    )(x, g, w)


def _out_proj_kernel(x_ref, ys_ref, yn_ref, w1_ref, w2_ref, o_ref):
    o_ref[...] = (x_ref[...]
                  + jnp.dot(ys_ref[...], w1_ref[...], preferred_element_type=F32)
                  + jnp.dot(yn_ref[...], w2_ref[...], preferred_element_type=F32))


def out_proj(x, y_ssm, y_nsa, w_out, layer, *, tm=512):
    m, d = x.shape
    k = y_ssm.shape[1]
    row = lambda i: (i, 0)
    return pl.pallas_call(
        _out_proj_kernel,
        grid=(m // tm,),
        in_specs=[pl.BlockSpec((tm, d), row), pl.BlockSpec((tm, k), row), pl.BlockSpec((tm, k), row),
                  pl.BlockSpec((None, k, d), lambda i: (layer, 0, 0)),
                  pl.BlockSpec((None, k, d), lambda i: (layer, 1, 0))],
        out_specs=pl.BlockSpec((tm, d), row),
        out_shape=jax.ShapeDtypeStruct((m, d), F32),
        compiler_params=_cparams(("parallel",)),
        name="out_proj",
    )(x, y_ssm, y_nsa, w_out, w_out)


def _ffn_kernel(x_ref, g_ref, wg_ref, wu_ref, wd_ref, o_ref, h_ref):
    f = pl.program_id(1)

    @pl.when(f == 0)
    def _():
        x = x_ref[...]
        h_ref[...] = _rms(x, g_ref[...]).astype(BF16)
        o_ref[...] = x

    h = h_ref[...]
    gate = jnp.dot(h, wg_ref[...], preferred_element_type=F32)
    up = jnp.dot(h, wu_ref[...], preferred_element_type=F32)
    act = (jax.nn.silu(gate) * up).astype(BF16)
    o_ref[...] += jnp.dot(act, wd_ref[...], preferred_element_type=F32)


def ffn(x, g, w_gate, w_up, w_down, layer, *, tm=1024, tf=256):
    m, d = x.shape
    dff = w_gate.shape[2]
    return pl.pallas_call(
        _ffn_kernel,
        grid=(m // tm, dff // tf),
        in_specs=[pl.BlockSpec((tm, d), lambda i, f: (i, 0)),
                  pl.BlockSpec((1, d), lambda i, f: (0, 0)),
                  pl.BlockSpec((None, d, tf), lambda i, f: (layer, 0, f)),
                  pl.BlockSpec((None, d, tf), lambda i, f: (layer, 0, f)),
                  pl.BlockSpec((None, tf, d), lambda i, f: (layer, f, 0))],
        out_specs=pl.BlockSpec((tm, d), lambda i, f: (i, 0)),
        out_shape=jax.ShapeDtypeStruct((m, d), F32),
        scratch_shapes=[pltpu.VMEM((tm, d), BF16)],
        compiler_params=_cparams(("parallel", "arbitrary")),
        name="ffn",
    )(x, g, w_gate, w_up, w_down)


def _final_norm_kernel(x_ref, g_ref, o_ref):
    o_ref[...] = _rms(x_ref[...], g_ref[...])


def final_rms_norm(x, g, *, tm=512):
    m, d = x.shape
    return pl.pallas_call(
        _final_norm_kernel,
        grid=(m // tm,),
        in_specs=[pl.BlockSpec((tm, d), lambda i: (i, 0)), pl.BlockSpec((1, d), lambda i: (0, 0))],
        out_specs=pl.BlockSpec((tm, d), lambda i: (i, 0)),
        out_shape=jax.ShapeDtypeStruct((m, d), F32),
        compiler_params=_cparams(("parallel",)),
        name="final_norm",
    )(x, g)


def _s5_discretize_kernel(are_ref, aim_ref, ldt_ref, bre_ref, bim_ref,
                          abre_ref, abim_ref, bbre_ref, bbim_ref):
    lam_re = jnp.minimum(are_ref[...], -1e-4)
    lam_im = aim_ref[...]
    dt = jnp.exp(ldt_ref[...])
    mag = jnp.exp(lam_re * dt)
    ab_re = mag * jnp.cos(lam_im * dt)
    ab_im = mag * jnp.sin(lam_im * dt)
    nr, ni = ab_re - 1.0, ab_im
    den = lam_re * lam_re + lam_im * lam_im
    f_re = (nr * lam_re + ni * lam_im) / den
    f_im = (ni * lam_re - nr * lam_im) / den
    br, bi = bre_ref[...], bim_ref[...]
    abre_ref[...] = ab_re
    abim_ref[...] = ab_im
    bbre_ref[...] = f_re * br - f_im * bi
    bbim_ref[...] = f_re * bi + f_im * br


def s5_discretize(a_re, a_im, log_dt, b_re_t, b_im_t):
    g, p = a_re.shape
    h = b_re_t.shape[1]
    outs = pl.pallas_call(
        _s5_discretize_kernel,
        out_shape=[jax.ShapeDtypeStruct((g, 1, p), F32), jax.ShapeDtypeStruct((g, 1, p), F32),
                   jax.ShapeDtypeStruct((g, h, p), F32), jax.ShapeDtypeStruct((g, h, p), F32)],
        name="s5_discretize",
    )(a_re.reshape(g, 1, p), a_im.reshape(g, 1, p), log_dt.reshape(g, 1, 1), b_re_t, b_im_t)
    return outs


def _s5_kernel(u_ref, bre_ref, bim_ref, cre_ref, cimn_ref, a_ref, d_ref, wglu_ref, bglu_ref,
               o_ref, xr_ref, xi_ref, st_ref):
    tc = u_ref.shape[0]
    pw = SSM_PACK * SSM_GROUP
    sw = SSM_PACK * SSM_STATE

    @pl.when(pl.program_id(1) == 0)
    def _():
        st_ref[...] = jnp.zeros_like(st_ref)

    u = u_ref[...]
    ub = u.astype(BF16)
    for k in range(N_PACKS):
        uk = ub[:, k * pw:(k + 1) * pw]
        xr_ref[:, k * sw:(k + 1) * sw] = jnp.dot(uk, bre_ref[k], preferred_element_type=F32)
        xi_ref[:, k * sw:(k + 1) * sw] = jnp.dot(uk, bim_ref[k], preferred_element_type=F32)

    a_re = a_ref[0:1, :]
    a_im = a_ref[1:2, :]

    def step(t, state):
        s_re, s_im = state
        n_re = a_re * s_re - a_im * s_im + xr_ref[pl.ds(t, 1), :]
        n_im = a_re * s_im + a_im * s_re + xi_ref[pl.ds(t, 1), :]
        xr_ref[pl.ds(t, 1), :] = n_re
        xi_ref[pl.ds(t, 1), :] = n_im
        return n_re, n_im

    s_re, s_im = lax.fori_loop(0, tc, step, (st_ref[0:1, :], st_ref[1:2, :]), unroll=4)
    st_ref[0:1, :] = s_re
    st_ref[1:2, :] = s_im

    ys = []
    for k in range(N_PACKS):
        xr = xr_ref[:, k * sw:(k + 1) * sw].astype(BF16)
        xi = xi_ref[:, k * sw:(k + 1) * sw].astype(BF16)
        ys.append(jnp.dot(xr, cre_ref[k], preferred_element_type=F32)
                  + jnp.dot(xi, cimn_ref[k], preferred_element_type=F32))
    y = jnp.concatenate(ys, axis=-1) + d_ref[...] * u
    y = jax.nn.gelu(y).astype(BF16)
    ab = jnp.dot(y, wglu_ref[...], preferred_element_type=F32) + bglu_ref[...]
    o_ref[...] = (ab[:, :D_SSM] * jax.nn.sigmoid(ab[:, D_SSM:])).astype(o_ref.dtype)


def s5_glu(proj, n_batch, bbd_re, bbd_im, cbd_re, cbd_imn, a_flat, d, w_glu, b_glu, layer, *, tc=256):
    t = proj.shape[0]
    n_chunks = t // n_batch // tc
    c3 = lambda b, c: (0, 0, 0)
    c2 = lambda b, c: (0, 0)
    return pl.pallas_call(
        _s5_kernel,
        grid=(n_batch, n_chunks),
        in_specs=[pl.BlockSpec((tc, D_SSM), lambda b, c: (b * n_chunks + c, 0)),
                  pl.BlockSpec(bbd_re.shape, c3), pl.BlockSpec(bbd_im.shape, c3),
                  pl.BlockSpec(cbd_re.shape, c3), pl.BlockSpec(cbd_imn.shape, c3),
                  pl.BlockSpec(a_flat.shape, c2), pl.BlockSpec(d.shape, c2),
                  pl.BlockSpec((None,) + w_glu.shape[1:], lambda b, c: (layer, 0, 0)),
                  pl.BlockSpec(b_glu.shape, c2)],
        out_specs=pl.BlockSpec((tc, D_SSM), lambda b, c: (b * n_chunks + c, 0)),
        out_shape=jax.ShapeDtypeStruct((t, D_SSM), BF16),
        scratch_shapes=[pltpu.VMEM((tc, N_STATE), F32), pltpu.VMEM((tc, N_STATE), F32),
                        pltpu.VMEM((2, N_STATE), F32)],
        compiler_params=_cparams(("arbitrary", "arbitrary")),
        name="s5_glu",
    )(proj, bbd_re, bbd_im, cbd_re, cbd_imn, a_flat, d, w_glu, b_glu)


def _s5_weights(a_re, a_im, log_dt, b_re, b_im, c_re, c_im):
    ab_re, ab_im, bb_re, bb_im = s5_discretize(a_re, a_im, log_dt,
                                               b_re.transpose(0, 2, 1), b_im.transpose(0, 2, 1))
    eye = jnp.eye(SSM_PACK, dtype=F32)

    def pack_b(bb):
        bb = bb.reshape(N_PACKS, SSM_PACK, SSM_GROUP, 1, SSM_STATE)
        m = bb * eye[None, :, None, :, None]
        return m.reshape(N_PACKS, SSM_PACK * SSM_GROUP, SSM_PACK * SSM_STATE).astype(BF16)

    def pack_c(c):
        c = c.reshape(N_PACKS, SSM_PACK, SSM_GROUP, SSM_STATE)
        m = c.transpose(0, 1, 3, 2)[:, :, :, None, :] * eye[None, :, None, :, None]
        return m.reshape(N_PACKS, SSM_PACK * SSM_STATE, SSM_PACK * SSM_GROUP).astype(BF16)

    a_flat = jnp.stack([ab_re.reshape(-1), ab_im.reshape(-1)])
    return pack_b(bb_re), pack_b(bb_im), pack_c(c_re), pack_c(-c_im), a_flat


def _compress_kernel(r_ref, n_ref, pos_ref, w1a_ref, w1b_ref, w2_ref, o_ref):
    r = r_ref[0]
    rows = r.shape[0]
    rowid = lax.broadcasted_iota(jnp.int32, (rows, 1), 0)
    nxt = jnp.where(rowid == rows - 1, n_ref[0, 0:1, :], pltpu.roll(r, rows - 1, 0))
    first = (r + pos_ref[0:1, :]).astype(BF16)
    second = (nxt + pos_ref[1:2, :]).astype(BF16)
    hid = (jnp.dot(first, w1a_ref[...], preferred_element_type=F32)
           + jnp.dot(second, w1b_ref[...], preferred_element_type=F32))
    hid = jax.nn.gelu(hid).astype(BF16)
    o_ref[0] = jnp.dot(hid, w2_ref[...], preferred_element_type=F32).astype(o_ref.dtype)


def compress(kc_rows, pos2, w1a, w1b, w2e):
    b, nc, w = kc_rows.shape
    tr = 128
    sub = 8
    full = lambda i, j: (0, 0)
    nxt_blk = lambda i, j: (i, jnp.minimum((j + 1) * (tr // sub), nc // sub - 1), 0)
    return pl.pallas_call(
        _compress_kernel,
        grid=(b, nc // tr),
        in_specs=[pl.BlockSpec((1, tr, w), lambda i, j: (i, j, 0)),
                  pl.BlockSpec((1, sub, w), nxt_blk),
                  pl.BlockSpec(pos2.shape, full), pl.BlockSpec(w1a.shape, full),
                  pl.BlockSpec(w1b.shape, full), pl.BlockSpec(w2e.shape, full)],
        out_specs=pl.BlockSpec((1, tr, N_KV * HEAD_DIM), lambda i, j: (i, j, 0)),
        out_shape=jax.ShapeDtypeStruct((b, nc, N_KV * HEAD_DIM), BF16),
        compiler_params=_cparams(("parallel", "parallel")),
        name="compress",
    )(kc_rows, kc_rows, pos2, w1a, w1b, w2e)


def _compress_weights(pos, w1, w2):
    eye = jnp.eye(N_KV, dtype=F32)
    w1 = w1.reshape(2, CMP_STRIDE, 1, HEAD_DIM, 1, CMP_HIDDEN)
    w1e = (w1 * eye[None, None, :, None, :, None]).reshape(
        2, CMP_STRIDE * N_KV * HEAD_DIM, N_KV * CMP_HIDDEN).astype(BF16)
    w2e = (w2[None, :, None, :] * eye[:, None, :, None]).reshape(
        N_KV * CMP_HIDDEN, N_KV * HEAD_DIM).astype(BF16)
    pos2 = jnp.broadcast_to(pos.reshape(2, CMP_STRIDE, 1, HEAD_DIM),
                            (2, CMP_STRIDE, N_KV, HEAD_DIM)).reshape(2, -1)
    return pos2, w1e[0], w1e[1], w2e


def _kv_prep_kernel(ks_ref, vs_ref, kw_ref, vw_ref, ka_ref, vst_ref, kwp_ref, vwtp_ref):
    j = pl.program_id(1)
    ts = ks_ref.shape[0]
    n_sel = ka_ref.shape[3] - HEAD_DIM
    blk0 = jnp.maximum(j - 1, 0) * (ts // SEL_BLOCK)
    row_blk = blk0 + lax.broadcasted_iota(jnp.int32, (ts, n_sel), 0) // SEL_BLOCK
    col = lax.broadcasted_iota(jnp.int32, (ts, n_sel), 1)
    onehot = jnp.where(row_blk == col, ONEHOT_BIG, 0.0).astype(BF16)
    ks = ks_ref[...]
    vs_t = vs_ref[...].T
    pad_rows = lax.broadcasted_iota(jnp.int32, (V_ROWS - HEAD_DIM, ts), 0)
    extra = jnp.where(pad_rows == 0, 1.0, 0.0).astype(BF16)
    for g in range(N_KV):
        lo, hi = g * HEAD_DIM, (g + 1) * HEAD_DIM
        ka_ref[0, g, :, 0:n_sel] = onehot
        ka_ref[0, g, :, n_sel:] = ks[:, lo:hi].astype(BF16)
        vst_ref[0, g, 0:HEAD_DIM, :] = vs_t[lo:hi].astype(BF16)
        vst_ref[0, g, HEAD_DIM:, :] = extra

    @pl.when(j == 0)
    def _():
        kwp_ref[...] = jnp.zeros_like(kwp_ref)
        vwtp_ref[...] = jnp.zeros_like(vwtp_ref)

    @pl.when(j > 0)
    def _():
        kw = kw_ref[...]
        vw_t = vw_ref[...].T
        for g in range(N_KV):
            lo, hi = g * HEAD_DIM, (g + 1) * HEAD_DIM
            kwp_ref[0, g] = kw[:, lo:hi].astype(BF16)
            vwtp_ref[0, g] = vw_t[lo:hi].astype(BF16)


def kv_prep(proj, n_batch, seq):
    ts = WINDOW
    nblk = seq // ts
    n_sel = seq // SEL_BLOCK
    kvw = N_KV * HEAD_DIM
    src = lambda col: pl.BlockSpec((ts, kvw), lambda b, j: (b * nblk + jnp.maximum(j - 1, 0), col // kvw))
    same = lambda b, j: (b, 0, jnp.maximum(j - 1, 0), 0)
    same_t = lambda b, j: (b, 0, 0, jnp.maximum(j - 1, 0))
    return pl.pallas_call(
        _kv_prep_kernel,
        grid=(n_batch, nblk + 1),
        in_specs=[src(COL_KS), src(COL_VS), src(COL_KW), src(COL_VW)],
        out_specs=[pl.BlockSpec((1, N_KV, ts, n_sel + HEAD_DIM), same),
                   pl.BlockSpec((1, N_KV, V_ROWS, ts), same_t),
                   pl.BlockSpec((1, N_KV, ts, HEAD_DIM), lambda b, j: (b, 0, j, 0)),
                   pl.BlockSpec((1, N_KV, HEAD_DIM, ts), lambda b, j: (b, 0, 0, j))],
        out_shape=[jax.ShapeDtypeStruct((n_batch, N_KV, seq, n_sel + HEAD_DIM), BF16),
                   jax.ShapeDtypeStruct((n_batch, N_KV, V_ROWS, seq), BF16),
                   jax.ShapeDtypeStruct((n_batch, N_KV, seq + WINDOW, HEAD_DIM), BF16),
                   jax.ShapeDtypeStruct((n_batch, N_KV, HEAD_DIM, seq + WINDOW), BF16)],
        compiler_params=_cparams(("parallel", "arbitrary")),
        name="kv_prep",
    )(proj, proj, proj, proj)


def _softmax_cols(s):
    m = jnp.max(s, axis=0, keepdims=True)
    m = jnp.where(m < 0.5 * NEG, 0.0, m)
    e = jnp.exp2(s - m)
    return e * (1.0 / jnp.maximum(jnp.sum(e, axis=0, keepdims=True), 1e-30))


def _cmp_branch(a, qt, gate, k_ref, vt_ref, tbl_ref, smat_ref, o_ref, st_ref, rank_ref):
    n_row_tiles = k_ref.shape[2] // CMP_LANE
    ap = a // (FAR_DIST // Q_TILE)

    def attend(n):
        rows = n * CMP_LANE
        s = jnp.dot(k_ref[0, 0, 0:rows, :], qt, preferred_element_type=F32)
        tbl = tbl_ref[0, 0]
        pieces = [s[c * CMP_LANE:(c + 1) * CMP_LANE] for c in range(n)]
        pieces[-1] = pieces[-1] + tbl[CMP_LANE:]
        if n > 1:
            pieces[-2] = pieces[-2] + tbl[:CMP_LANE]
        p = _softmax_cols(jnp.concatenate(pieces, axis=0))
        o_ref[...] = gate * jnp.dot(vt_ref[0, 0, :, 0:rows], p.astype(BF16), preferred_element_type=F32)
        imp = (p[:, 0:Q_TILE] + p[:, Q_TILE:2 * Q_TILE] + p[:, 2 * Q_TILE:3 * Q_TILE]
               + p[:, 3 * Q_TILE:4 * Q_TILE])
        hi = imp.astype(BF16)
        r1 = imp - hi.astype(F32)
        mid = r1.astype(BF16)
        lo = (r1 - mid.astype(F32)).astype(BF16)
        smat = smat_ref[:, 0:rows]
        st_ref[...] = (jnp.dot(smat, hi, preferred_element_type=F32)
                       + jnp.dot(smat, mid, preferred_element_type=F32)
                       + jnp.dot(smat, lo, preferred_element_type=F32))

    for n in range(1, n_row_tiles + 1):
        pl.when(ap == n - 1)(lambda n=n: attend(n))

    o = o_ref[...]
    st = st_ref[...]
    t = a * Q_TILE + lax.broadcasted_iota(jnp.int32, st.shape, 1)
    jj = lax.broadcasted_iota(jnp.int32, st.shape, 0)
    cur_blk = t // SEL_BLOCK
    valid = jj * SEL_BLOCK <= t
    forced = (jj == 0) | (jj == cur_blk) | (jj == cur_blk - 1)
    st = jnp.where(valid, jnp.where(forced, FORCED_SCORE, st), -1.0)
    st_ref[...] = st
    rank_ref[...] = jnp.zeros_like(rank_ref)
    sub = 8
    chunk = 16
    n_sel = st.shape[0]
    last_valid = (a * Q_TILE + Q_TILE - 1) // SEL_BLOCK
    for q0 in range(0, n_sel, chunk):
        @pl.when(q0 <= last_valid)
        def _(q0=q0):
            for v0 in range(0, n_sel, chunk):
                @pl.when(v0 <= last_valid)
                def _(v0=v0):
                    for v in range(v0 // sub, min(v0 + chunk, n_sel) // sub):
                        sv = st_ref[v * sub:(v + 1) * sub, :]
                        jv = jj[v * sub:(v + 1) * sub]
                        cnt = rank_ref[v * sub:(v + 1) * sub, :]
                        for jp in range(q0, min(q0 + chunk, n_sel)):
                            row = st_ref[jp:jp + 1, :]
                            if jp < v * sub:
                                ahead = row >= sv
                            elif jp >= (v + 1) * sub:
                                ahead = row > sv
                            else:
                                ahead = (row > sv) | ((row == sv) & (jv > jp))
                            cnt = cnt + jnp.where(ahead, 1.0, 0.0)
                        rank_ref[v * sub:(v + 1) * sub, :] = cnt

    keep = (rank_ref[...] < SEL_TOP) & (st >= 0.0)
    return o, (keep.astype(F32) - 1.0).astype(BF16)


def _sel_branch(a, gate, ka_ref, vt_ref, tbl_ref, qa_ref, m_ref, acc_ref):
    tk = SEL_KEY_TILE
    acc_ref[...] = jnp.zeros_like(acc_ref)
    n_tiles = a // (tk // Q_TILE) + 1
    far_pairs = jnp.maximum(0, (a - (SEL_NEAR_TILES - 5)) // 4)
    far_octs = far_pairs // 4
    far_quads = (far_pairs - 4 * far_octs) // 2
    first_near = 2 * far_pairs
    near_quads = (n_tiles - first_near) // 4
    near_pairs = (n_tiles - first_near - 4 * near_quads + 1) // 2

    def key_off(c):
        return pl.multiple_of(jnp.minimum(c, n_tiles - 1) * tk, tk)

    def attend(c0, n, biased):
        m_old = m_ref[...]
        acc_old = acc_ref[...]

        whole = n >= 4
        span = pl.ds(pl.multiple_of(c0 * tk, tk), n * tk)

        def scores():
            if whole:
                s_all = jnp.dot(ka_ref[0, 0, span, :], qa_ref[...], preferred_element_type=F32)
                ss = [s_all[h * tk:(h + 1) * tk] for h in range(n)]
            else:
                ss = [jnp.dot(ka_ref[0, 0, pl.ds(key_off(c0 + h), tk), :], qa_ref[...],
                              preferred_element_type=F32) for h in range(n)]
            if biased:
                ss = [s + tbl_ref[0, jnp.where(c0 + h < n_tiles, a - 2 * (c0 + h), SEL_NEAR_TILES - 1)]
                      for h, s in enumerate(ss)]
            return ss

        def weighted(ss, m):
            ps = [jnp.exp2(s - m).astype(BF16) for s in ss]
            if whole:
                return jnp.dot(vt_ref[0, 0, :, span], jnp.concatenate(ps, axis=0), preferred_element_type=F32)
            return sum(jnp.dot(vt_ref[0, 0, :, pl.ds(key_off(c0 + h), tk)], p, preferred_element_type=F32)
                       for h, p in enumerate(ps))

        ss = scores()
        m_new = m_old
        for s in ss:
            m_new = jnp.maximum(m_new, jnp.max(s, axis=0, keepdims=True))
        acc_ref[...] = (acc_old + weighted(ss, m_old)) * jnp.exp2(m_old - m_new)
        m_ref[...] = m_new

        @pl.when(jnp.max(m_new - m_old) > SAFE_LOG2_RANGE)
        def _():
            acc_ref[...] = jnp.exp2(m_old - m_new) * acc_old + weighted(scores(), m_new)

    def body(first, n, biased):
        def step(i, carry):
            attend(first + n * i, n, biased)
            return carry
        return step

    sub = 8
    m_ref[...] = jnp.dot(ka_ref[0, 0, 0:sub, :], qa_ref[...], preferred_element_type=F32)[0:1, :]
    lax.fori_loop(0, far_octs, body(0, 8, False), 0)
    lax.fori_loop(0, far_quads, body(8 * far_octs, 4, False), 0)
    lax.fori_loop(0, far_pairs - 4 * far_octs - 2 * far_quads, body(8 * far_octs + 4 * far_quads, 2, False), 0)
    lax.fori_loop(0, near_quads, body(first_near, 4, True), 0)
    lax.fori_loop(0, near_pairs, body(first_near + 4 * near_quads, 2, True), 0)
    acc = acc_ref[...]
    return acc[0:HEAD_DIM] * (gate / acc[HEAD_DIM:HEAD_DIM + 1])


def _win_branch(a, qt, gate, k_ref, vt_ref, tbl_ref):
    span = WINDOW + Q_TILE
    off = pl.multiple_of(a * Q_TILE, Q_TILE)
    s = jnp.dot(k_ref[0, 0, pl.ds(off, span), :], qt, preferred_element_type=F32)
    s = s + tbl_ref[0, 0]
    e = jnp.exp2(s - jnp.max(s, axis=0, keepdims=True))
    ev = jnp.dot(vt_ref[0, 0, :, pl.ds(off, span)], e.astype(BF16), preferred_element_type=F32)
    return ev * (gate / jnp.sum(e, axis=0, keepdims=True))


def _nsa_kernel(q_ref, g_ref, kc_ref, vtc_ref, tblc_ref, smat_ref, ka_ref, vst_ref, tbls_ref,
                kw_ref, vwt_ref, tblw_ref, o_ref, gt_ref, oc_ref, st_ref, rank_ref, qa_ref, m_ref, acc_ref):
    g = pl.program_id(0)
    a = pl.program_id(2)
    n_sel = smat_ref.shape[0]
    half = 2 * HEAD_DIM
    q = q_ref[...] * (HEAD_DIM ** -0.5) * LOG2E
    t0 = q[:, :half].T
    t1 = q[:, half:].T
    qt = jnp.concatenate([t0[:HEAD_DIM], t0[HEAD_DIM:], t1[:HEAD_DIM], t1[HEAD_DIM:]],
                         axis=1).astype(BF16)
    gt_ref[...] = g_ref[:, :Q_TILE].T

    def gate(branch):
        rows = [gt_ref[pl.ds((g * GQA_R + r) * 3 + branch, 1), :] for r in range(GQA_R)]
        return jax.nn.sigmoid(jnp.concatenate(rows, axis=1))

    o, sel = _cmp_branch(a, qt, gate(0), kc_ref, vtc_ref, tblc_ref, smat_ref, oc_ref, st_ref, rank_ref)
    for r in range(GQA_R):
        qa_ref[0:n_sel, r * Q_TILE:(r + 1) * Q_TILE] = sel
    qa_ref[n_sel:, :] = qt
    o = o + _sel_branch(a, gate(1), ka_ref, vst_ref, tbls_ref, qa_ref, m_ref, acc_ref)
    o = o + _win_branch(a, qt, gate(2), kw_ref, vwt_ref, tblw_ref)
    u0 = jnp.concatenate([o[:, 0:Q_TILE], o[:, Q_TILE:2 * Q_TILE]], axis=0).T
    u1 = jnp.concatenate([o[:, 2 * Q_TILE:3 * Q_TILE], o[:, 3 * Q_TILE:]], axis=0).T
    o_ref[...] = jnp.concatenate([u0, u1], axis=1).astype(o_ref.dtype)


def nsa_attention(proj, n_batch, seq, k_cmp, vt_cmp, ka, vst, kwp, vwtp, tables, smat):
    tbl_c, tbl_s, tbl_w = tables
    n_a = seq // Q_TILE
    nc = k_cmp.shape[2]
    n_sel = smat.shape[0]
    dh = HEAD_DIM
    gw = GQA_R * HEAD_DIM
    tps = FAR_DIST // Q_TILE
    per_bg = lambda gi, bi, ai: (bi, gi, 0, 0)
    return pl.pallas_call(
        _nsa_kernel,
        grid=(N_KV, n_batch, n_a),
        in_specs=[pl.BlockSpec((Q_TILE, gw), lambda gi, bi, ai: (bi * n_a + ai, COL_Q // gw + gi)),
                  pl.BlockSpec((Q_TILE, gw), lambda gi, bi, ai: (bi * n_a + ai, COL_GATE // gw)),
                  pl.BlockSpec((1, 1, nc, dh), per_bg),
                  pl.BlockSpec((1, 1, dh, nc), per_bg),
                  pl.BlockSpec((1, 1, 2 * CMP_LANE, ROWS), lambda gi, bi, ai: (gi, ai % tps, 0, 0)),
                  pl.BlockSpec(smat.shape, lambda gi, bi, ai: (0, 0)),
                  pl.BlockSpec((1, 1, seq, n_sel + dh), per_bg),
                  pl.BlockSpec((1, 1, V_ROWS, seq), per_bg),
                  pl.BlockSpec((1, SEL_NEAR_TILES, SEL_KEY_TILE, ROWS), lambda gi, bi, ai: (gi, 0, 0, 0)),
                  pl.BlockSpec((1, 1, seq + WINDOW, dh), per_bg),
                  pl.BlockSpec((1, 1, dh, seq + WINDOW), per_bg),
                  pl.BlockSpec((1, 1, WINDOW + Q_TILE, ROWS),
                               lambda gi, bi, ai: (gi, jnp.minimum(ai, WINDOW // Q_TILE), 0, 0))],
        out_specs=pl.BlockSpec((Q_TILE, gw), lambda gi, bi, ai: (bi * n_a + ai, gi)),
        out_shape=jax.ShapeDtypeStruct((n_batch * seq, N_HEADS * HEAD_DIM), BF16),
        scratch_shapes=[pltpu.VMEM((Q_TILE, Q_TILE), F32),
                        pltpu.VMEM((dh, ROWS), F32),
                        pltpu.VMEM((n_sel, Q_TILE), F32), pltpu.VMEM((n_sel, Q_TILE), F32),
                        pltpu.VMEM((n_sel + dh, ROWS), BF16), pltpu.VMEM((1, ROWS), F32),
                        pltpu.VMEM((V_ROWS, ROWS), F32)],
        compiler_params=_cparams(("parallel", "parallel", "arbitrary")),
        name="nsa_attention",
    )(proj, proj, k_cmp, vt_cmp, tbl_c, smat, ka, vst, tbl_s, kwp, vwtp, tbl_w)


def _rel_bucket(dist):
    n = jnp.maximum(dist, 0)
    nf = jnp.maximum(n, 1).astype(F32)
    large = RPB_MAX_EXACT + (jnp.log(nf / RPB_MAX_EXACT) / math.log(RPB_MAX_DIST / RPB_MAX_EXACT)
                             * (N_BUCKETS - RPB_MAX_EXACT)).astype(jnp.int32)
    large = jnp.minimum(large, N_BUCKETS - 1)
    return jnp.where(n < RPB_MAX_EXACT, n, large)


def _shifted_rows(e, n_rows, stride):
    return pltpu.roll(jnp.broadcast_to(e, (n_rows, e.shape[1])), 0, 1, stride=stride, stride_axis=0)


def _bias_tables_kernel(ec_ref, es_ref, ew_ref, tc_ref, ts_ref, tw_ref):
    n_m = tc_ref.shape[1]
    n_am = tc_ref.shape[0]
    mc = _shifted_rows(ec_ref[...], n_m, CMP_STRIDE)
    first = mc.shape[1] - n_am * Q_TILE
    for am in range(n_am):
        tc_ref[am] = mc[:, first + am * Q_TILE:first + (am + 1) * Q_TILE]
    tk = ts_ref.shape[1]
    for k in range(ts_ref.shape[0]):
        ts_ref[k] = _shifted_rows(es_ref[k:k + 1, :], tk, 1)[:, tk:tk + Q_TILE]
    span = tw_ref.shape[1]
    mw = _shifted_rows(ew_ref[...], span, 1)[:, span:span + Q_TILE]
    kj = lax.broadcasted_iota(jnp.int32, mw.shape, 0)
    for v in range(tw_ref.shape[0]):
        tw_ref[v] = jnp.where(kj >= WINDOW - v * Q_TILE, mw, NEG)


def _bias_tables(rpb_table):
    tab = rpb_table.astype(F32)
    by_dist = tab[_rel_bucket(jnp.arange(FAR_DIST))].T * LOG2E
    rel = by_dist - (tab[N_BUCKETS - 1] * LOG2E)[:, None]
    neg = lambda n: jnp.full((N_HEADS, n), NEG, F32)
    zero = lambda n: jnp.zeros((N_HEADS, n), F32)
    tk, span, n_m, n_am = SEL_KEY_TILE, WINDOW + Q_TILE, 2 * CMP_LANE, FAR_DIST // Q_TILE

    lead = CMP_STRIDE * n_m - (FAR_DIST - CMP_BLOCK + 1)
    e_c = jnp.concatenate([neg(lead), rel, zero(CMP_STRIDE * n_m - lead)], axis=1)[:, None, :]
    ext = jnp.concatenate([neg(tk), rel], axis=1)
    e_s = jnp.stack([ext[:, Q_TILE * k:Q_TILE * k + tk + Q_TILE] for k in range(SEL_NEAR_TILES - 1)]
                    + [neg(tk + Q_TILE)], axis=1)
    e_w = jnp.concatenate([neg(Q_TILE), by_dist[:, :WINDOW], neg(Q_TILE)], axis=1)[:, None, :]

    n_var = WINDOW // Q_TILE + 1
    head = lambda h: (h // GQA_R, 0, 0, h % GQA_R)
    return pl.pallas_call(
        _bias_tables_kernel,
        grid=(N_HEADS,),
        in_specs=[pl.BlockSpec((None,) + e_c.shape[1:], lambda h: (h, 0, 0)),
                  pl.BlockSpec((None,) + e_s.shape[1:], lambda h: (h, 0, 0)),
                  pl.BlockSpec((None,) + e_w.shape[1:], lambda h: (h, 0, 0))],
        out_specs=[pl.BlockSpec((None, n_am, n_m, Q_TILE), head),
                   pl.BlockSpec((None, SEL_NEAR_TILES, tk, Q_TILE), head),
                   pl.BlockSpec((None, n_var, span, Q_TILE), head)],
        out_shape=[jax.ShapeDtypeStruct((N_KV, n_am, n_m, ROWS), F32),
                   jax.ShapeDtypeStruct((N_KV, SEL_NEAR_TILES, tk, ROWS), F32),
                   jax.ShapeDtypeStruct((N_KV, n_var, span, ROWS), F32)],
        compiler_params=_cparams(("parallel",)),
        name="bias_tables",
    )(e_c, e_s, e_w)


def _slc_matrix(n_cmp_pad, n_sel):
    sel_ratio, cmp_ratio = SEL_BLOCK // CMP_STRIDE, CMP_BLOCK // CMP_STRIDE
    mat = np.zeros((n_sel, n_cmp_pad), np.float32)
    for j in range(n_sel):
        for mm in range(sel_ratio):
            for nn in range(cmp_ratio):
                idx = sel_ratio * j - mm - nn
                if 0 <= idx < n_cmp_pad - 1:
                    mat[j, idx] += 1.0
    return jnp.asarray(mat, BF16)


def nsa_mixer(proj, n_batch, seq, cmp_k_w, cmp_v_w, tables, smat):
    kv = N_KV * HEAD_DIM
    rows16 = lambda col: proj[:, col:col + kv].reshape(n_batch, seq // CMP_STRIDE, CMP_STRIDE * kv)
    k_cmp = compress(rows16(COL_KC), *cmp_k_w)
    v_cmp = compress(rows16(COL_VC), *cmp_v_w)
    k_cmp = k_cmp.reshape(n_batch, -1, N_KV, HEAD_DIM).transpose(0, 2, 1, 3)
    vt_cmp = v_cmp.reshape(n_batch, -1, N_KV, HEAD_DIM).transpose(0, 2, 3, 1)
    ka, vst, kwp, vwtp = kv_prep(proj, n_batch, seq)
    return nsa_attention(proj, n_batch, seq, k_cmp, vt_cmp, ka, vst, kwp, vwtp, tables, smat)


def kernel(x, rpb_table, attn_norm, ffn_norm, final_norm, w_in, ssm_a_re, ssm_a_im, ssm_log_dt,
           ssm_b_re, ssm_b_im, ssm_c_re, ssm_c_im, ssm_d, w_glu, b_glu, cmp_pos_k, cmp_w1_k, cmp_w2_k,
           cmp_pos_v, cmp_w1_v, cmp_w2_v, w_out, w_ffn_gate, w_ffn_up, w_ffn_down):
    n_batch, seq, d_model = x.shape
    depth = w_in.shape[0]
    tables = _bias_tables(rpb_table)
    smat = _slc_matrix(seq // CMP_STRIDE, seq // SEL_BLOCK)

    w_in_b = jnp.pad(w_in, ((0, 0), (0, 0), (0, IN_COLS_PAD - IN_COLS))).astype(BF16)
    w_glu_b, w_out_b = w_glu.astype(BF16), w_out.astype(BF16)
    w_gate_b, w_up_b, w_down_b = w_ffn_gate.astype(BF16), w_ffn_up.astype(BF16), w_ffn_down.astype(BF16)

    xf = x.reshape(n_batch * seq, d_model)
    for l in range(depth):
        proj = norm_matmul(xf, attn_norm[l].reshape(1, -1), w_in_b, l)
        s5w = _s5_weights(ssm_a_re[l], ssm_a_im[l], ssm_log_dt[l], ssm_b_re[l], ssm_b_im[l],
                          ssm_c_re[l], ssm_c_im[l])
        y_ssm = s5_glu(proj, n_batch, *s5w, ssm_d[l].reshape(1, -1), w_glu_b, b_glu[l].reshape(1, -1), l)
        y_nsa = nsa_mixer(proj, n_batch, seq,
                          _compress_weights(cmp_pos_k[l], cmp_w1_k[l], cmp_w2_k[l]),
                          _compress_weights(cmp_pos_v[l], cmp_w1_v[l], cmp_w2_v[l]),
                          tables, smat)
        xf = out_proj(xf, y_ssm, y_nsa, w_out_b, l)
        xf = ffn(xf, ffn_norm[l].reshape(1, -1), w_gate_b, w_up_b, w_down_b, l)
    out = final_rms_norm(xf, final_norm.reshape(1, -1))
    return out.reshape(n_batch, seq, d_model)
```

```python
import math

import jax
import jax.numpy as jnp
import numpy as np
from jax import lax
from jax.experimental import pallas as pl
from jax.experimental.pallas import tpu as pltpu

F32 = jnp.float32
BF16 = jnp.bfloat16

D_MODEL = 2048
DEPTH = 4
D_SSM = 1024
SSM_GROUP = 16
N_SSM_GROUPS = 64
SSM_STATE = 64
N_STATE = N_SSM_GROUPS * SSM_STATE
SSM_PACK = 16
N_PACKS = N_SSM_GROUPS // SSM_PACK
N_HEADS = 16
N_KV = 4
GQA_R = 4
HEAD_DIM = 64
CMP_BLOCK = 32
CMP_STRIDE = 16
CMP_HIDDEN = 128
SEL_BLOCK = 64
SEL_TOP = 16
WINDOW = 512
Q_TILE = 128
ROWS = GQA_R * Q_TILE
FORCED_SCORE = 1e4
N_BUCKETS = 32
RPB_MAX_EXACT = 16
RPB_MAX_DIST = 1024
D_FF = 5632
NORM_EPS = 1e-6
IN_COLS = 3632
IN_COLS_PAD = 3840
COL_U, COL_Q, COL_KC, COL_VC, COL_KS, COL_VS, COL_KW, COL_VW, COL_GATE = (
    0, 1024, 2048, 2304, 2560, 2816, 3072, 3328, 3584)
NEG = -1e30
ONEHOT_BIG = 1e30
FAR_DIST = 2048
SEL_KEY_TILE = 256
SEL_NEAR_TILES = 12
LOG2E = math.log2(math.e)
SAFE_LOG2_RANGE = 64.0
CMP_LANE = 128
V_ROWS = 80
VMEM_LIMIT = 56 * 1024 * 1024


def _cparams(sem):
    return pltpu.CompilerParams(dimension_semantics=sem, vmem_limit_bytes=VMEM_LIMIT)


def _rms(x, g):
    ms = jnp.mean(x * x, axis=-1, keepdims=True)
    return x * lax.rsqrt(ms + NORM_EPS) * g


def _norm_matmul_kernel(x_ref, g_ref, w_ref, o_ref, h_ref):
    @pl.when(pl.program_id(1) == 0)
    def _():
        h_ref[...] = _rms(x_ref[...], g_ref[...]).astype(BF16)

    o_ref[...] = jnp.dot(h_ref[...], w_ref[...], preferred_element_type=F32)


def norm_matmul(x, g, w, layer, *, tm=1024, tn=768):
    m, k = x.shape
    n = w.shape[2]
    return pl.pallas_call(
        _norm_matmul_kernel,
        grid=(m // tm, n // tn),
        in_specs=[pl.BlockSpec((tm, k), lambda i, j: (i, 0)),
                  pl.BlockSpec((1, k), lambda i, j: (0, 0)),
                  pl.BlockSpec((None, k, tn), lambda i, j: (layer, 0, j))],
        out_specs=pl.BlockSpec((tm, tn), lambda i, j: (i, j)),
        out_shape=jax.ShapeDtypeStruct((m, n), F32),
        scratch_shapes=[pltpu.VMEM((tm, k), BF16)],
        compiler_params=_cparams(("parallel", "arbitrary")),
        name="norm_matmul",
    )(x, g, w)


def _out_proj_kernel(x_ref, ys_ref, yn_ref, w1_ref, w2_ref, o_ref):
    o_ref[...] = (x_ref[...]
                  + jnp.dot(ys_ref[...], w1_ref[...], preferred_element_type=F32)
                  + jnp.dot(yn_ref[...], w2_ref[...], preferred_element_type=F32))


def out_proj(x, y_ssm, y_nsa, w_out, layer, *, tm=512):
    m, d = x.shape
    k = y_ssm.shape[1]
    row = lambda i: (i, 0)
    return pl.pallas_call(
        _out_proj_kernel,
        grid=(m // tm,),
        in_specs=[pl.BlockSpec((tm, d), row), pl.BlockSpec((tm, k), row), pl.BlockSpec((tm, k), row),
                  pl.BlockSpec((None, k, d), lambda i: (layer, 0, 0)),
                  pl.BlockSpec((None, k, d), lambda i: (layer, 1, 0))],
        out_specs=pl.BlockSpec((tm, d), row),
        out_shape=jax.ShapeDtypeStruct((m, d), F32),
        compiler_params=_cparams(("parallel",)),
        name="out_proj",
    )(x, y_ssm, y_nsa, w_out, w_out)


def _ffn_kernel(x_ref, g_ref, wg_ref, wu_ref, wd_ref, o_ref, h_ref):
    f = pl.program_id(1)

    @pl.when(f == 0)
    def _():
        x = x_ref[...]
        h_ref[...] = _rms(x, g_ref[...]).astype(BF16)
        o_ref[...] = x

    h = h_ref[...]
    gate = jnp.dot(h, wg_ref[...], preferred_element_type=F32)
    up = jnp.dot(h, wu_ref[...], preferred_element_type=F32)
    act = (jax.nn.silu(gate) * up).astype(BF16)
    o_ref[...] += jnp.dot(act, wd_ref[...], preferred_element_type=F32)


def ffn(x, g, w_gate, w_up, w_down, layer, *, tm=1024, tf=256):
    m, d = x.shape
    dff = w_gate.shape[2]
    return pl.pallas_call(
        _ffn_kernel,
        grid=(m // tm, dff // tf),
        in_specs=[pl.BlockSpec((tm, d), lambda i, f: (i, 0)),
                  pl.BlockSpec((1, d), lambda i, f: (0, 0)),
                  pl.BlockSpec((None, d, tf), lambda i, f: (layer, 0, f)),
                  pl.BlockSpec((None, d, tf), lambda i, f: (layer, 0, f)),
                  pl.BlockSpec((None, tf, d), lambda i, f: (layer, f, 0))],
        out_specs=pl.BlockSpec((tm, d), lambda i, f: (i, 0)),
        out_shape=jax.ShapeDtypeStruct((m, d), F32),
        scratch_shapes=[pltpu.VMEM((tm, d), BF16)],
        compiler_params=_cparams(("parallel", "arbitrary")),
        name="ffn",
    )(x, g, w_gate, w_up, w_down)


def _final_norm_kernel(x_ref, g_ref, o_ref):
    o_ref[...] = _rms(x_ref[...], g_ref[...])


def final_rms_norm(x, g, *, tm=512):
    m, d = x.shape
    return pl.pallas_call(
        _final_norm_kernel,
        grid=(m // tm,),
        in_specs=[pl.BlockSpec((tm, d), lambda i: (i, 0)), pl.BlockSpec((1, d), lambda i: (0, 0))],
        out_specs=pl.BlockSpec((tm, d), lambda i: (i, 0)),
        out_shape=jax.ShapeDtypeStruct((m, d), F32),
        compiler_params=_cparams(("parallel",)),
        name="final_norm",
    )(x, g)


def _s5_discretize_kernel(are_ref, aim_ref, ldt_ref, bre_ref, bim_ref,
                          abre_ref, abim_ref, bbre_ref, bbim_ref):
    lam_re = jnp.minimum(are_ref[...], -1e-4)
    lam_im = aim_ref[...]
    dt = jnp.exp(ldt_ref[...])
    mag = jnp.exp(lam_re * dt)
    ab_re = mag * jnp.cos(lam_im * dt)
    ab_im = mag * jnp.sin(lam_im * dt)
    nr, ni = ab_re - 1.0, ab_im
    den = lam_re * lam_re + lam_im * lam_im
    f_re = (nr * lam_re + ni * lam_im) / den
    f_im = (ni * lam_re - nr * lam_im) / den
    br, bi = bre_ref[...], bim_ref[...]
    abre_ref[...] = ab_re
    abim_ref[...] = ab_im
    bbre_ref[...] = f_re * br - f_im * bi
    bbim_ref[...] = f_re * bi + f_im * br


def s5_discretize(a_re, a_im, log_dt, b_re_t, b_im_t):
    g, p = a_re.shape
    h = b_re_t.shape[1]
    outs = pl.pallas_call(
        _s5_discretize_kernel,
        out_shape=[jax.ShapeDtypeStruct((g, 1, p), F32), jax.ShapeDtypeStruct((g, 1, p), F32),
                   jax.ShapeDtypeStruct((g, h, p), F32), jax.ShapeDtypeStruct((g, h, p), F32)],
        name="s5_discretize",
    )(a_re.reshape(g, 1, p), a_im.reshape(g, 1, p), log_dt.reshape(g, 1, 1), b_re_t, b_im_t)
    return outs


def _s5_kernel(u_ref, bre_ref, bim_ref, cre_ref, cimn_ref, a_ref, d_ref, wglu_ref, bglu_ref,
               o_ref, xr_ref, xi_ref, st_ref):
    tc = u_ref.shape[0]
    pw = SSM_PACK * SSM_GROUP
    sw = SSM_PACK * SSM_STATE

    @pl.when(pl.program_id(1) == 0)
    def _():
        st_ref[...] = jnp.zeros_like(st_ref)

    u = u_ref[...]
    ub = u.astype(BF16)
    for k in range(N_PACKS):
        uk = ub[:, k * pw:(k + 1) * pw]
        xr_ref[:, k * sw:(k + 1) * sw] = jnp.dot(uk, bre_ref[k], preferred_element_type=F32)
        xi_ref[:, k * sw:(k + 1) * sw] = jnp.dot(uk, bim_ref[k], preferred_element_type=F32)

    a_re = a_ref[0:1, :]
    a_im = a_ref[1:2, :]

    def step(t, state):
        s_re, s_im = state
        n_re = a_re * s_re - a_im * s_im + xr_ref[pl.ds(t, 1), :]
        n_im = a_re * s_im + a_im * s_re + xi_ref[pl.ds(t, 1), :]
        xr_ref[pl.ds(t, 1), :] = n_re
        xi_ref[pl.ds(t, 1), :] = n_im
        return n_re, n_im

    s_re, s_im = lax.fori_loop(0, tc, step, (st_ref[0:1, :], st_ref[1:2, :]), unroll=4)
    st_ref[0:1, :] = s_re
    st_ref[1:2, :] = s_im

    ys = []
    for k in range(N_PACKS):
        xr = xr_ref[:, k * sw:(k + 1) * sw].astype(BF16)
        xi = xi_ref[:, k * sw:(k + 1) * sw].astype(BF16)
        ys.append(jnp.dot(xr, cre_ref[k], preferred_element_type=F32)
                  + jnp.dot(xi, cimn_ref[k], preferred_element_type=F32))
    y = jnp.concatenate(ys, axis=-1) + d_ref[...] * u
    y = jax.nn.gelu(y).astype(BF16)
    ab = jnp.dot(y, wglu_ref[...], preferred_element_type=F32) + bglu_ref[...]
    o_ref[...] = (ab[:, :D_SSM] * jax.nn.sigmoid(ab[:, D_SSM:])).astype(o_ref.dtype)


def s5_glu(proj, n_batch, bbd_re, bbd_im, cbd_re, cbd_imn, a_flat, d, w_glu, b_glu, layer, *, tc=256):
    t = proj.shape[0]
    n_chunks = t // n_batch // tc
    c2 = lambda b, c: (0, 0)
    of_layer = lambda w: pl.BlockSpec((None,) + w.shape[1:], lambda b, c: (layer,) + (0,) * (w.ndim - 1))
    return pl.pallas_call(
        _s5_kernel,
        grid=(n_batch, n_chunks),
        in_specs=[pl.BlockSpec((tc, D_SSM), lambda b, c: (b * n_chunks + c, 0)),
                  of_layer(bbd_re), of_layer(bbd_im), of_layer(cbd_re), of_layer(cbd_imn),
                  of_layer(a_flat), pl.BlockSpec(d.shape, c2), of_layer(w_glu),
                  pl.BlockSpec(b_glu.shape, c2)],
        out_specs=pl.BlockSpec((tc, D_SSM), lambda b, c: (b * n_chunks + c, 0)),
        out_shape=jax.ShapeDtypeStruct((t, D_SSM), BF16),
        scratch_shapes=[pltpu.VMEM((tc, N_STATE), F32), pltpu.VMEM((tc, N_STATE), F32),
                        pltpu.VMEM((2, N_STATE), F32)],
        compiler_params=_cparams(("arbitrary", "arbitrary")),
        name="s5_glu",
    )(proj, bbd_re, bbd_im, cbd_re, cbd_imn, a_flat, d, w_glu, b_glu)


def _s5_weights(a_re, a_im, log_dt, b_re, b_im, c_re, c_im):
    depth, g = a_re.shape[:2]
    flat = lambda z: z.reshape((depth * g,) + z.shape[2:])
    outs = s5_discretize(flat(a_re), flat(a_im), flat(log_dt),
                         flat(b_re).transpose(0, 2, 1), flat(b_im).transpose(0, 2, 1))
    ab_re, ab_im, bb_re, bb_im = [z.reshape((depth, g) + z.shape[1:]) for z in outs]
    eye = jnp.eye(SSM_PACK, dtype=F32)

    def pack_b(bb):
        bb = bb.reshape(depth, N_PACKS, SSM_PACK, SSM_GROUP, 1, SSM_STATE)
        m = bb * eye[None, None, :, None, :, None]
        return m.reshape(depth, N_PACKS, SSM_PACK * SSM_GROUP, SSM_PACK * SSM_STATE).astype(BF16)

    def pack_c(c):
        c = c.reshape(depth, N_PACKS, SSM_PACK, SSM_GROUP, SSM_STATE)
        m = c.transpose(0, 1, 2, 4, 3)[:, :, :, :, None, :] * eye[None, None, :, None, :, None]
        return m.reshape(depth, N_PACKS, SSM_PACK * SSM_STATE, SSM_PACK * SSM_GROUP).astype(BF16)

    a_flat = jnp.stack([ab_re.reshape(depth, -1), ab_im.reshape(depth, -1)], axis=1)
    return pack_b(bb_re), pack_b(bb_im), pack_c(c_re), pack_c(-c_im), a_flat


def _compress_kernel(r_ref, n_ref, pos_ref, w1_ref, w2_ref, o_ref):
    r = r_ref[0]
    rows = r.shape[0]
    rowid = lax.broadcasted_iota(jnp.int32, (rows, 1), 0)
    nxt = jnp.where(rowid == rows - 1, n_ref[0, 0:1, :], pltpu.roll(r, rows - 1, 0))
    first = (r + pos_ref[0:1, :]).astype(BF16)
    second = (nxt + pos_ref[1:2, :]).astype(BF16)
    hid = (jnp.dot(first, w1_ref[0], preferred_element_type=F32)
           + jnp.dot(second, w1_ref[1], preferred_element_type=F32))
    hid = jax.nn.gelu(hid).astype(BF16)
    o_ref[0] = jnp.dot(hid, w2_ref[...], preferred_element_type=F32).astype(o_ref.dtype)


def compress(kc_rows, pos2, w1e, w2e, layer):
    b, nc, w = kc_rows.shape
    tr = 128
    sub = 8
    of_layer = lambda z: pl.BlockSpec((None,) + z.shape[1:], lambda i, j: (layer,) + (0,) * (z.ndim - 1))
    nxt_blk = lambda i, j: (i, jnp.minimum((j + 1) * (tr // sub), nc // sub - 1), 0)
    return pl.pallas_call(
        _compress_kernel,
        grid=(b, nc // tr),
        in_specs=[pl.BlockSpec((1, tr, w), lambda i, j: (i, j, 0)),
                  pl.BlockSpec((1, sub, w), nxt_blk),
                  of_layer(pos2), of_layer(w1e), of_layer(w2e)],
        out_specs=pl.BlockSpec((1, tr, N_KV * HEAD_DIM), lambda i, j: (i, j, 0)),
        out_shape=jax.ShapeDtypeStruct((b, nc, N_KV * HEAD_DIM), BF16),
        compiler_params=_cparams(("parallel", "parallel")),
        name="compress",
    )(kc_rows, kc_rows, pos2, w1e, w2e)


def _compress_weights(pos, w1, w2):
    depth = pos.shape[0]
    eye = jnp.eye(N_KV, dtype=F32)
    w1 = w1.reshape(depth, 2, CMP_STRIDE, 1, HEAD_DIM, 1, CMP_HIDDEN)
    w1e = (w1 * eye[None, None, None, :, None, :, None]).reshape(
        depth, 2, CMP_STRIDE * N_KV * HEAD_DIM, N_KV * CMP_HIDDEN).astype(BF16)
    w2e = (w2[:, None, :, None, :] * eye[None, :, None, :, None]).reshape(
        depth, N_KV * CMP_HIDDEN, N_KV * HEAD_DIM).astype(BF16)
    pos2 = jnp.broadcast_to(pos.reshape(depth, 2, CMP_STRIDE, 1, HEAD_DIM),
                            (depth, 2, CMP_STRIDE, N_KV, HEAD_DIM)).reshape(depth, 2, -1)
    return pos2, w1e, w2e


def _kv_prep_kernel(ks_ref, vs_ref, kw_ref, vw_ref, ka_ref, vst_ref, kwp_ref, vwtp_ref):
    j = pl.program_id(1)
    ts = ks_ref.shape[0]
    n_sel = ka_ref.shape[3] - HEAD_DIM
    blk0 = jnp.maximum(j - 1, 0) * (ts // SEL_BLOCK)
    row_blk = blk0 + lax.broadcasted_iota(jnp.int32, (ts, n_sel), 0) // SEL_BLOCK
    col = lax.broadcasted_iota(jnp.int32, (ts, n_sel), 1)
    onehot = jnp.where(row_blk == col, ONEHOT_BIG, 0.0).astype(BF16)
    ks = ks_ref[...]
    vs_t = vs_ref[...].T
    pad_rows = lax.broadcasted_iota(jnp.int32, (V_ROWS - HEAD_DIM, ts), 0)
    extra = jnp.where(pad_rows == 0, 1.0, 0.0).astype(BF16)
    for g in range(N_KV):
        lo, hi = g * HEAD_DIM, (g + 1) * HEAD_DIM
        ka_ref[0, g, :, 0:n_sel] = onehot
        ka_ref[0, g, :, n_sel:] = ks[:, lo:hi].astype(BF16)
        vst_ref[0, g, 0:HEAD_DIM, :] = vs_t[lo:hi].astype(BF16)
        vst_ref[0, g, HEAD_DIM:, :] = extra

    @pl.when(j == 0)
    def _():
        kwp_ref[...] = jnp.zeros_like(kwp_ref)
        vwtp_ref[...] = jnp.zeros_like(vwtp_ref)

    @pl.when(j > 0)
    def _():
        kw = kw_ref[...]
        vw_t = vw_ref[...].T
        for g in range(N_KV):
            lo, hi = g * HEAD_DIM, (g + 1) * HEAD_DIM
            kwp_ref[0, g] = kw[:, lo:hi].astype(BF16)
            vwtp_ref[0, g] = vw_t[lo:hi].astype(BF16)


def kv_prep(proj, n_batch, seq):
    ts = WINDOW
    nblk = seq // ts
    n_sel = seq // SEL_BLOCK
    kvw = N_KV * HEAD_DIM
    src = lambda col: pl.BlockSpec((ts, kvw), lambda b, j: (b * nblk + jnp.maximum(j - 1, 0), col // kvw))
    same = lambda b, j: (b, 0, jnp.maximum(j - 1, 0), 0)
    same_t = lambda b, j: (b, 0, 0, jnp.maximum(j - 1, 0))
    return pl.pallas_call(
        _kv_prep_kernel,
        grid=(n_batch, nblk + 1),
        in_specs=[src(COL_KS), src(COL_VS), src(COL_KW), src(COL_VW)],
        out_specs=[pl.BlockSpec((1, N_KV, ts, n_sel + HEAD_DIM), same),
                   pl.BlockSpec((1, N_KV, V_ROWS, ts), same_t),
                   pl.BlockSpec((1, N_KV, ts, HEAD_DIM), lambda b, j: (b, 0, j, 0)),
                   pl.BlockSpec((1, N_KV, HEAD_DIM, ts), lambda b, j: (b, 0, 0, j))],
        out_shape=[jax.ShapeDtypeStruct((n_batch, N_KV, seq, n_sel + HEAD_DIM), BF16),
                   jax.ShapeDtypeStruct((n_batch, N_KV, V_ROWS, seq), BF16),
                   jax.ShapeDtypeStruct((n_batch, N_KV, seq + WINDOW, HEAD_DIM), BF16),
                   jax.ShapeDtypeStruct((n_batch, N_KV, HEAD_DIM, seq + WINDOW), BF16)],
        compiler_params=_cparams(("parallel", "arbitrary")),
        name="kv_prep",
    )(proj, proj, proj, proj)


def _softmax_cols(s):
    m = jnp.max(s, axis=0, keepdims=True)
    m = jnp.where(m < 0.5 * NEG, 0.0, m)
    e = jnp.exp2(s - m)
    return e * (1.0 / jnp.maximum(jnp.sum(e, axis=0, keepdims=True), 1e-30))


def _cmp_branch(a, qt, gate, k_ref, vt_ref, tbl_ref, smat_ref, o_ref, st_ref, rank_ref):
    n_row_tiles = k_ref.shape[2] // CMP_LANE
    ap = a // (FAR_DIST // Q_TILE)

    def attend(n):
        rows = n * CMP_LANE
        s = jnp.dot(k_ref[0, 0, 0:rows, :], qt, preferred_element_type=F32)
        tbl = tbl_ref[0, 0]
        pieces = [s[c * CMP_LANE:(c + 1) * CMP_LANE] for c in range(n)]
        pieces[-1] = pieces[-1] + tbl[CMP_LANE:]
        if n > 1:
            pieces[-2] = pieces[-2] + tbl[:CMP_LANE]
        p = _softmax_cols(jnp.concatenate(pieces, axis=0))
        o_ref[...] = gate * jnp.dot(vt_ref[0, 0, :, 0:rows], p.astype(BF16), preferred_element_type=F32)
        imp = (p[:, 0:Q_TILE] + p[:, Q_TILE:2 * Q_TILE] + p[:, 2 * Q_TILE:3 * Q_TILE]
               + p[:, 3 * Q_TILE:4 * Q_TILE])
        hi = imp.astype(BF16)
        r1 = imp - hi.astype(F32)
        mid = r1.astype(BF16)
        lo = (r1 - mid.astype(F32)).astype(BF16)
        smat = smat_ref[:, 0:rows]
        st_ref[...] = (jnp.dot(smat, hi, preferred_element_type=F32)
                       + jnp.dot(smat, mid, preferred_element_type=F32)
                       + jnp.dot(smat, lo, preferred_element_type=F32))

    for n in range(1, n_row_tiles + 1):
        pl.when(ap == n - 1)(lambda n=n: attend(n))

    o = o_ref[...]
    st = st_ref[...]
    t = a * Q_TILE + lax.broadcasted_iota(jnp.int32, st.shape, 1)
    jj = lax.broadcasted_iota(jnp.int32, st.shape, 0)
    cur_blk = t // SEL_BLOCK
    valid = jj * SEL_BLOCK <= t
    forced = (jj == 0) | (jj == cur_blk) | (jj == cur_blk - 1)
    st = jnp.where(valid, jnp.where(forced, FORCED_SCORE, st), -1.0)
    st_ref[...] = st
    rank_ref[...] = jnp.zeros_like(rank_ref)
    sub = 8
    chunk = 16
    n_sel = st.shape[0]
    last_valid = (a * Q_TILE + Q_TILE - 1) // SEL_BLOCK
    for q0 in range(0, n_sel, chunk):
        @pl.when(q0 <= last_valid)
        def _(q0=q0):
            for v0 in range(0, n_sel, chunk):
                @pl.when(v0 <= last_valid)
                def _(v0=v0):
                    for v in range(v0 // sub, min(v0 + chunk, n_sel) // sub):
                        sv = st_ref[v * sub:(v + 1) * sub, :]
                        jv = jj[v * sub:(v + 1) * sub]
                        cnt = rank_ref[v * sub:(v + 1) * sub, :]
                        for jp in range(q0, min(q0 + chunk, n_sel)):
                            row = st_ref[jp:jp + 1, :]
                            if jp < v * sub:
                                ahead = row >= sv
                            elif jp >= (v + 1) * sub:
                                ahead = row > sv
                            else:
                                ahead = (row > sv) | ((row == sv) & (jv > jp))
                            cnt = cnt + jnp.where(ahead, 1.0, 0.0)
                        rank_ref[v * sub:(v + 1) * sub, :] = cnt

    keep = (rank_ref[...] < SEL_TOP) & (st >= 0.0)
    return o, (keep.astype(F32) - 1.0).astype(BF16)


def _sel_branch(a, gate, ka_ref, vt_ref, tbl_ref, qa_ref, m_ref, acc_ref):
    tk = SEL_KEY_TILE
    acc_ref[...] = jnp.zeros_like(acc_ref)
    n_tiles = a // (tk // Q_TILE) + 1
    far_pairs = jnp.maximum(0, (a - (SEL_NEAR_TILES - 5)) // 4)
    far_octs = far_pairs // 4
    far_quads = (far_pairs - 4 * far_octs) // 2
    first_near = 2 * far_pairs
    near_quads = (n_tiles - first_near) // 4
    near_pairs = (n_tiles - first_near - 4 * near_quads + 1) // 2

    def key_off(c):
        return pl.multiple_of(jnp.minimum(c, n_tiles - 1) * tk, tk)

    def attend(c0, n, biased):
        m_old = m_ref[...]
        acc_old = acc_ref[...]

        whole = n >= 4
        span = pl.ds(pl.multiple_of(c0 * tk, tk), n * tk)

        def scores():
            if whole:
                s_all = jnp.dot(ka_ref[0, 0, span, :], qa_ref[...], preferred_element_type=F32)
                ss = [s_all[h * tk:(h + 1) * tk] for h in range(n)]
            else:
                ss = [jnp.dot(ka_ref[0, 0, pl.ds(key_off(c0 + h), tk), :], qa_ref[...],
                              preferred_element_type=F32) for h in range(n)]
            if biased:
                ss = [s + tbl_ref[0, jnp.where(c0 + h < n_tiles, a - 2 * (c0 + h), SEL_NEAR_TILES - 1)]
                      for h, s in enumerate(ss)]
            return ss

        def weighted(ss, m):
            ps = [jnp.exp2(s - m).astype(BF16) for s in ss]
            if whole:
                return jnp.dot(vt_ref[0, 0, :, span], jnp.concatenate(ps, axis=0), preferred_element_type=F32)
            return sum(jnp.dot(vt_ref[0, 0, :, pl.ds(key_off(c0 + h), tk)], p, preferred_element_type=F32)
                       for h, p in enumerate(ps))

        ss = scores()
        m_new = m_old
        for s in ss:
            m_new = jnp.maximum(m_new, jnp.max(s, axis=0, keepdims=True))
        acc_ref[...] = (acc_old + weighted(ss, m_old)) * jnp.exp2(m_old - m_new)
        m_ref[...] = m_new

        @pl.when(jnp.max(m_new - m_old) > SAFE_LOG2_RANGE)
        def _():
            acc_ref[...] = jnp.exp2(m_old - m_new) * acc_old + weighted(scores(), m_new)

    def body(first, n, biased):
        def step(i, carry):
            attend(first + n * i, n, biased)
            return carry
        return step

    sub = 8
    m_ref[...] = jnp.dot(ka_ref[0, 0, 0:sub, :], qa_ref[...], preferred_element_type=F32)[0:1, :]
    lax.fori_loop(0, far_octs, body(0, 8, False), 0)
    lax.fori_loop(0, far_quads, body(8 * far_octs, 4, False), 0)
    lax.fori_loop(0, far_pairs - 4 * far_octs - 2 * far_quads, body(8 * far_octs + 4 * far_quads, 2, False), 0)
    lax.fori_loop(0, near_quads, body(first_near, 4, True), 0)
    lax.fori_loop(0, near_pairs, body(first_near + 4 * near_quads, 2, True), 0)
    acc = acc_ref[...]
    return acc[0:HEAD_DIM] * (gate / acc[HEAD_DIM:HEAD_DIM + 1])


def _win_branch(a, qt, gate, k_ref, vt_ref, tbl_ref):
    span = WINDOW + Q_TILE
    off = pl.multiple_of(a * Q_TILE, Q_TILE)
    s = jnp.dot(k_ref[0, 0, pl.ds(off, span), :], qt, preferred_element_type=F32)
    s = s + tbl_ref[0, 0]
    e = jnp.exp2(s - jnp.max(s, axis=0, keepdims=True))
    ev = jnp.dot(vt_ref[0, 0, :, pl.ds(off, span)], e.astype(BF16), preferred_element_type=F32)
    return ev * (gate / jnp.sum(e, axis=0, keepdims=True))


def _nsa_kernel(q_ref, g_ref, kc_ref, vtc_ref, tblc_ref, smat_ref, ka_ref, vst_ref, tbls_ref,
                kw_ref, vwt_ref, tblw_ref, o_ref, gt_ref, oc_ref, st_ref, rank_ref, qa_ref, m_ref, acc_ref):
    g = pl.program_id(0)
    a = pl.program_id(2)
    n_sel = smat_ref.shape[0]
    half = 2 * HEAD_DIM
    q = q_ref[...] * (HEAD_DIM ** -0.5) * LOG2E
    t0 = q[:, :half].T
    t1 = q[:, half:].T
    qt = jnp.concatenate([t0[:HEAD_DIM], t0[HEAD_DIM:], t1[:HEAD_DIM], t1[HEAD_DIM:]],
                         axis=1).astype(BF16)
    gt_ref[...] = g_ref[:, :Q_TILE].T

    def gate(branch):
        rows = [gt_ref[pl.ds((g * GQA_R + r) * 3 + branch, 1), :] for r in range(GQA_R)]
        return jax.nn.sigmoid(jnp.concatenate(rows, axis=1))

    o, sel = _cmp_branch(a, qt, gate(0), kc_ref, vtc_ref, tblc_ref, smat_ref, oc_ref, st_ref, rank_ref)
    for r in range(GQA_R):
        qa_ref[0:n_sel, r * Q_TILE:(r + 1) * Q_TILE] = sel
    qa_ref[n_sel:, :] = qt
    o = o + _sel_branch(a, gate(1), ka_ref, vst_ref, tbls_ref, qa_ref, m_ref, acc_ref)
    o = o + _win_branch(a, qt, gate(2), kw_ref, vwt_ref, tblw_ref)
    u0 = jnp.concatenate([o[:, 0:Q_TILE], o[:, Q_TILE:2 * Q_TILE]], axis=0).T
    u1 = jnp.concatenate([o[:, 2 * Q_TILE:3 * Q_TILE], o[:, 3 * Q_TILE:]], axis=0).T
    o_ref[...] = jnp.concatenate([u0, u1], axis=1).astype(o_ref.dtype)


def nsa_attention(proj, n_batch, seq, k_cmp, vt_cmp, ka, vst, kwp, vwtp, tables, smat):
    tbl_c, tbl_s, tbl_w = tables
    n_a = seq // Q_TILE
    nc = k_cmp.shape[2]
    n_sel = smat.shape[0]
    dh = HEAD_DIM
    gw = GQA_R * HEAD_DIM
    tps = FAR_DIST // Q_TILE
    per_bg = lambda gi, bi, ai: (bi, gi, 0, 0)
    return pl.pallas_call(
        _nsa_kernel,
        grid=(N_KV, n_batch, n_a),
        in_specs=[pl.BlockSpec((Q_TILE, gw), lambda gi, bi, ai: (bi * n_a + ai, COL_Q // gw + gi)),
                  pl.BlockSpec((Q_TILE, gw), lambda gi, bi, ai: (bi * n_a + ai, COL_GATE // gw)),
                  pl.BlockSpec((1, 1, nc, dh), per_bg),
                  pl.BlockSpec((1, 1, dh, nc), per_bg),
                  pl.BlockSpec((1, 1, 2 * CMP_LANE, ROWS), lambda gi, bi, ai: (gi, ai % tps, 0, 0)),
                  pl.BlockSpec(smat.shape, lambda gi, bi, ai: (0, 0)),
                  pl.BlockSpec((1, 1, seq, n_sel + dh), per_bg),
                  pl.BlockSpec((1, 1, V_ROWS, seq), per_bg),
                  pl.BlockSpec((1, SEL_NEAR_TILES, SEL_KEY_TILE, ROWS), lambda gi, bi, ai: (gi, 0, 0, 0)),
                  pl.BlockSpec((1, 1, seq + WINDOW, dh), per_bg),
                  pl.BlockSpec((1, 1, dh, seq + WINDOW), per_bg),
                  pl.BlockSpec((1, 1, WINDOW + Q_TILE, ROWS),
                               lambda gi, bi, ai: (gi, jnp.minimum(ai, WINDOW // Q_TILE), 0, 0))],
        out_specs=pl.BlockSpec((Q_TILE, gw), lambda gi, bi, ai: (bi * n_a + ai, gi)),
        out_shape=jax.ShapeDtypeStruct((n_batch * seq, N_HEADS * HEAD_DIM), BF16),
        scratch_shapes=[pltpu.VMEM((Q_TILE, Q_TILE), F32),
                        pltpu.VMEM((dh, ROWS), F32),
                        pltpu.VMEM((n_sel, Q_TILE), F32), pltpu.VMEM((n_sel, Q_TILE), F32),
                        pltpu.VMEM((n_sel + dh, ROWS), BF16), pltpu.VMEM((1, ROWS), F32),
                        pltpu.VMEM((V_ROWS, ROWS), F32)],
        compiler_params=_cparams(("parallel", "parallel", "arbitrary")),
        name="nsa_attention",
    )(proj, proj, k_cmp, vt_cmp, tbl_c, smat, ka, vst, tbl_s, kwp, vwtp, tbl_w)


def _rel_bucket(dist):
    n = jnp.maximum(dist, 0)
    nf = jnp.maximum(n, 1).astype(F32)
    large = RPB_MAX_EXACT + (jnp.log(nf / RPB_MAX_EXACT) / math.log(RPB_MAX_DIST / RPB_MAX_EXACT)
                             * (N_BUCKETS - RPB_MAX_EXACT)).astype(jnp.int32)
    large = jnp.minimum(large, N_BUCKETS - 1)
    return jnp.where(n < RPB_MAX_EXACT, n, large)


def _shifted_rows(e, n_rows, stride):
    return pltpu.roll(jnp.broadcast_to(e, (n_rows, e.shape[1])), 0, 1, stride=stride, stride_axis=0)


def _bias_tables_kernel(ec_ref, es_ref, ew_ref, tc_ref, ts_ref, tw_ref):
    n_m = tc_ref.shape[1]
    n_am = tc_ref.shape[0]
    mc = _shifted_rows(ec_ref[...], n_m, CMP_STRIDE)
    first = mc.shape[1] - n_am * Q_TILE
    for am in range(n_am):
        tc_ref[am] = mc[:, first + am * Q_TILE:first + (am + 1) * Q_TILE]
    tk = ts_ref.shape[1]
    for k in range(ts_ref.shape[0]):
        ts_ref[k] = _shifted_rows(es_ref[k:k + 1, :], tk, 1)[:, tk:tk + Q_TILE]
    span = tw_ref.shape[1]
    mw = _shifted_rows(ew_ref[...], span, 1)[:, span:span + Q_TILE]
    kj = lax.broadcasted_iota(jnp.int32, mw.shape, 0)
    for v in range(tw_ref.shape[0]):
        tw_ref[v] = jnp.where(kj >= WINDOW - v * Q_TILE, mw, NEG)


def _bias_tables(rpb_table):
    tab = rpb_table.astype(F32)
    by_dist = tab[_rel_bucket(jnp.arange(FAR_DIST))].T * LOG2E
    rel = by_dist - (tab[N_BUCKETS - 1] * LOG2E)[:, None]
    neg = lambda n: jnp.full((N_HEADS, n), NEG, F32)
    zero = lambda n: jnp.zeros((N_HEADS, n), F32)
    tk, span, n_m, n_am = SEL_KEY_TILE, WINDOW + Q_TILE, 2 * CMP_LANE, FAR_DIST // Q_TILE

    lead = CMP_STRIDE * n_m - (FAR_DIST - CMP_BLOCK + 1)
    e_c = jnp.concatenate([neg(lead), rel, zero(CMP_STRIDE * n_m - lead)], axis=1)[:, None, :]
    ext = jnp.concatenate([neg(tk), rel], axis=1)
    e_s = jnp.stack([ext[:, Q_TILE * k:Q_TILE * k + tk + Q_TILE] for k in range(SEL_NEAR_TILES - 1)]
                    + [neg(tk + Q_TILE)], axis=1)
    e_w = jnp.concatenate([neg(Q_TILE), by_dist[:, :WINDOW], neg(Q_TILE)], axis=1)[:, None, :]

    n_var = WINDOW // Q_TILE + 1
    head = lambda h: (h // GQA_R, 0, 0, h % GQA_R)
    return pl.pallas_call(
        _bias_tables_kernel,
        grid=(N_HEADS,),
        in_specs=[pl.BlockSpec((None,) + e_c.shape[1:], lambda h: (h, 0, 0)),
                  pl.BlockSpec((None,) + e_s.shape[1:], lambda h: (h, 0, 0)),
                  pl.BlockSpec((None,) + e_w.shape[1:], lambda h: (h, 0, 0))],
        out_specs=[pl.BlockSpec((None, n_am, n_m, Q_TILE), head),
                   pl.BlockSpec((None, SEL_NEAR_TILES, tk, Q_TILE), head),
                   pl.BlockSpec((None, n_var, span, Q_TILE), head)],
        out_shape=[jax.ShapeDtypeStruct((N_KV, n_am, n_m, ROWS), F32),
                   jax.ShapeDtypeStruct((N_KV, SEL_NEAR_TILES, tk, ROWS), F32),
                   jax.ShapeDtypeStruct((N_KV, n_var, span, ROWS), F32)],
        compiler_params=_cparams(("parallel",)),
        name="bias_tables",
    )(e_c, e_s, e_w)


def _slc_matrix(n_cmp_pad, n_sel):
    sel_ratio, cmp_ratio = SEL_BLOCK // CMP_STRIDE, CMP_BLOCK // CMP_STRIDE
    mat = np.zeros((n_sel, n_cmp_pad), np.float32)
    for j in range(n_sel):
        for mm in range(sel_ratio):
            for nn in range(cmp_ratio):
                idx = sel_ratio * j - mm - nn
                if 0 <= idx < n_cmp_pad - 1:
                    mat[j, idx] += 1.0
    return jnp.asarray(mat, BF16)


def nsa_mixer(proj, n_batch, seq, cmp_k_w, cmp_v_w, tables, smat, layer):
    kv = N_KV * HEAD_DIM
    rows16 = lambda col: proj[:, col:col + kv].reshape(n_batch, seq // CMP_STRIDE, CMP_STRIDE * kv)
    k_cmp = compress(rows16(COL_KC), *cmp_k_w, layer)
    v_cmp = compress(rows16(COL_VC), *cmp_v_w, layer)
    k_cmp = k_cmp.reshape(n_batch, -1, N_KV, HEAD_DIM).transpose(0, 2, 1, 3)
    vt_cmp = v_cmp.reshape(n_batch, -1, N_KV, HEAD_DIM).transpose(0, 2, 3, 1)
    ka, vst, kwp, vwtp = kv_prep(proj, n_batch, seq)
    return nsa_attention(proj, n_batch, seq, k_cmp, vt_cmp, ka, vst, kwp, vwtp, tables, smat)


def kernel(x, rpb_table, attn_norm, ffn_norm, final_norm, w_in, ssm_a_re, ssm_a_im, ssm_log_dt,
           ssm_b_re, ssm_b_im, ssm_c_re, ssm_c_im, ssm_d, w_glu, b_glu, cmp_pos_k, cmp_w1_k, cmp_w2_k,
           cmp_pos_v, cmp_w1_v, cmp_w2_v, w_out, w_ffn_gate, w_ffn_up, w_ffn_down):
    n_batch, seq, d_model = x.shape
    depth = w_in.shape[0]
    tables = _bias_tables(rpb_table)
    smat = _slc_matrix(seq // CMP_STRIDE, seq // SEL_BLOCK)

    w_in_b = jnp.pad(w_in, ((0, 0), (0, 0), (0, IN_COLS_PAD - IN_COLS))).astype(BF16)
    w_glu_b, w_out_b = w_glu.astype(BF16), w_out.astype(BF16)
    w_gate_b, w_up_b, w_down_b = w_ffn_gate.astype(BF16), w_ffn_up.astype(BF16), w_ffn_down.astype(BF16)

    s5w = _s5_weights(ssm_a_re, ssm_a_im, ssm_log_dt, ssm_b_re, ssm_b_im, ssm_c_re, ssm_c_im)
    cmp_k_w = _compress_weights(cmp_pos_k, cmp_w1_k, cmp_w2_k)
    cmp_v_w = _compress_weights(cmp_pos_v, cmp_w1_v, cmp_w2_v)

    xf = x.reshape(n_batch * seq, d_model)
    for l in range(depth):
        proj = norm_matmul(xf, attn_norm[l].reshape(1, -1), w_in_b, l)
        y_ssm = s5_glu(proj, n_batch, *s5w, ssm_d[l].reshape(1, -1), w_glu_b, b_glu[l].reshape(1, -1), l)
        y_nsa = nsa_mixer(proj, n_batch, seq, cmp_k_w, cmp_v_w, tables, smat, l)
        xf = out_proj(xf, y_ssm, y_nsa, w_out_b, l)
        xf = ffn(xf, ffn_norm[l].reshape(1, -1), w_gate_b, w_up_b, w_down_b, l)
    out = final_rms_norm(xf, final_norm.reshape(1, -1))
    return out.reshape(n_batch, seq, d_model)
```

```python
import functools
import math

import jax
import jax.numpy as jnp
import numpy as np
from jax import lax
from jax.experimental import pallas as pl
from jax.experimental.pallas import tpu as pltpu

F32 = jnp.float32
BF16 = jnp.bfloat16

D_MODEL = 2048
DEPTH = 4
D_SSM = 1024
SSM_GROUP = 16
N_SSM_GROUPS = 64
SSM_STATE = 64
N_STATE = N_SSM_GROUPS * SSM_STATE
SSM_PACK = 16
N_PACKS = N_SSM_GROUPS // SSM_PACK
N_HEADS = 16
N_KV = 4
GQA_R = 4
HEAD_DIM = 64
CMP_BLOCK = 32
CMP_STRIDE = 16
CMP_HIDDEN = 128
SEL_BLOCK = 64
SEL_TOP = 16
WINDOW = 512
Q_TILE = 128
ROWS = GQA_R * Q_TILE
FORCED_SCORE = 1e4
N_BUCKETS = 32
RPB_MAX_EXACT = 16
RPB_MAX_DIST = 1024
D_FF = 5632
NORM_EPS = 1e-6
IN_COLS = 3632
IN_COLS_PAD = 3840
COL_U, COL_Q, COL_KC, COL_VC, COL_KS, COL_VS, COL_KW, COL_VW, COL_GATE = (
    0, 1024, 2048, 2304, 2560, 2816, 3072, 3328, 3584)
NEG = -1e30
ONEHOT_BIG = 1e30
FAR_DIST = 2048
SEL_KEY_TILE = 256
SEL_NEAR_TILES = 12
LOG2E = math.log2(math.e)
SAFE_LOG2_RANGE = 64.0
CMP_LANE = 128
V_ROWS = 80
VMEM_LIMIT = 56 * 1024 * 1024


def _cparams(sem):
    return pltpu.CompilerParams(dimension_semantics=sem, vmem_limit_bytes=VMEM_LIMIT)


def _rms(x, g):
    ms = jnp.mean(x * x, axis=-1, keepdims=True)
    return x * lax.rsqrt(ms + NORM_EPS) * g


def _norm_matmul_kernel(x_ref, g_ref, w_ref, o_ref, h_ref):
    @pl.when(pl.program_id(1) == 0)
    def _():
        h_ref[...] = _rms(x_ref[...], g_ref[...]).astype(BF16)

    o_ref[...] = jnp.dot(h_ref[...], w_ref[...], preferred_element_type=F32)


def norm_matmul(x, g, w, layer, *, tm=1024, tn=768):
    m, k = x.shape
    n = w.shape[2]
    return pl.pallas_call(
        _norm_matmul_kernel,
        grid=(m // tm, n // tn),
        in_specs=[pl.BlockSpec((tm, k), lambda i, j: (i, 0)),
                  pl.BlockSpec((1, k), lambda i, j: (0, 0)),
                  pl.BlockSpec((None, k, tn), lambda i, j: (layer, 0, j))],
        out_specs=pl.BlockSpec((tm, tn), lambda i, j: (i, j)),
        out_shape=jax.ShapeDtypeStruct((m, n), F32),
        scratch_shapes=[pltpu.VMEM((tm, k), BF16)],
        compiler_params=_cparams(("parallel", "arbitrary")),
        name="norm_matmul",
    )(x, g, w)


def _out_proj_kernel(x_ref, ys_ref, yn_ref, w1_ref, w2_ref, o_ref):
    o_ref[...] = (x_ref[...]
                  + jnp.dot(ys_ref[...], w1_ref[...], preferred_element_type=F32)
                  + jnp.dot(yn_ref[...], w2_ref[...], preferred_element_type=F32))


def out_proj(x, y_ssm, y_nsa, w_out, layer, *, tm=512):
    m, d = x.shape
    k = y_ssm.shape[1]
    row = lambda i: (i, 0)
    return pl.pallas_call(
        _out_proj_kernel,
        grid=(m // tm,),
        in_specs=[pl.BlockSpec((tm, d), row), pl.BlockSpec((tm, k), row), pl.BlockSpec((tm, k), row),
                  pl.BlockSpec((None, k, d), lambda i: (layer, 0, 0)),
                  pl.BlockSpec((None, k, d), lambda i: (layer, 1, 0))],
        out_specs=pl.BlockSpec((tm, d), row),
        out_shape=jax.ShapeDtypeStruct((m, d), F32),
        compiler_params=_cparams(("parallel",)),
        name="out_proj",
    )(x, y_ssm, y_nsa, w_out, w_out)


def _ffn_kernel(x_ref, g_ref, wg_ref, wu_ref, wd_ref, g_out_ref, o_ref, h_ref, *, norm_out):
    f = pl.program_id(1)

    @pl.when(f == 0)
    def _():
        x = x_ref[...]
        h_ref[...] = _rms(x, g_ref[...]).astype(BF16)
        o_ref[...] = x

    h = h_ref[...]
    gate = jnp.dot(h, wg_ref[...], preferred_element_type=F32)
    up = jnp.dot(h, wu_ref[...], preferred_element_type=F32)
    act = (jax.nn.silu(gate) * up).astype(BF16)
    o_ref[...] += jnp.dot(act, wd_ref[...], preferred_element_type=F32)

    if norm_out:
        @pl.when(f == pl.num_programs(1) - 1)
        def _():
            o_ref[...] = _rms(o_ref[...], g_out_ref[...])


def ffn(x, g, w_gate, w_up, w_down, layer, g_out, *, norm_out, tm=1024, tf=256):
    m, d = x.shape
    dff = w_gate.shape[2]
    return pl.pallas_call(
        functools.partial(_ffn_kernel, norm_out=norm_out),
        grid=(m // tm, dff // tf),
        in_specs=[pl.BlockSpec((tm, d), lambda i, f: (i, 0)),
                  pl.BlockSpec((1, d), lambda i, f: (0, 0)),
                  pl.BlockSpec((None, d, tf), lambda i, f: (layer, 0, f)),
                  pl.BlockSpec((None, d, tf), lambda i, f: (layer, 0, f)),
                  pl.BlockSpec((None, tf, d), lambda i, f: (layer, f, 0)),
                  pl.BlockSpec((1, d), lambda i, f: (0, 0))],
        out_specs=pl.BlockSpec((tm, d), lambda i, f: (i, 0)),
        out_shape=jax.ShapeDtypeStruct((m, d), F32),
        scratch_shapes=[pltpu.VMEM((tm, d), BF16)],
        compiler_params=_cparams(("parallel", "arbitrary")),
        name="ffn",
    )(x, g, w_gate, w_up, w_down, g_out)


def _s5_discretize_kernel(are_ref, aim_ref, ldt_ref, bre_ref, bim_ref,
                          abre_ref, abim_ref, bbre_ref, bbim_ref):
    lam_re = jnp.minimum(are_ref[...], -1e-4)
    lam_im = aim_ref[...]
    dt = jnp.exp(ldt_ref[...])
    mag = jnp.exp(lam_re * dt)
    ab_re = mag * jnp.cos(lam_im * dt)
    ab_im = mag * jnp.sin(lam_im * dt)
    nr, ni = ab_re - 1.0, ab_im
    den = lam_re * lam_re + lam_im * lam_im
    f_re = (nr * lam_re + ni * lam_im) / den
    f_im = (ni * lam_re - nr * lam_im) / den
    br, bi = bre_ref[...], bim_ref[...]
    abre_ref[...] = ab_re
    abim_ref[...] = ab_im
    bbre_ref[...] = f_re * br - f_im * bi
    bbim_ref[...] = f_re * bi + f_im * br


def s5_discretize(a_re, a_im, log_dt, b_re_t, b_im_t):
    g, p = a_re.shape
    h = b_re_t.shape[1]
    outs = pl.pallas_call(
        _s5_discretize_kernel,
        out_shape=[jax.ShapeDtypeStruct((g, 1, p), F32), jax.ShapeDtypeStruct((g, 1, p), F32),
                   jax.ShapeDtypeStruct((g, h, p), F32), jax.ShapeDtypeStruct((g, h, p), F32)],
        name="s5_discretize",
    )(a_re.reshape(g, 1, p), a_im.reshape(g, 1, p), log_dt.reshape(g, 1, 1), b_re_t, b_im_t)
    return outs


def _s5_kernel(u_ref, bre_ref, bim_ref, cre_ref, cimn_ref, a_ref, d_ref, wglu_ref, bglu_ref,
               o_ref, xr_ref, xi_ref, st_ref):
    tc = u_ref.shape[0]
    pw = SSM_PACK * SSM_GROUP
    sw = SSM_PACK * SSM_STATE

    @pl.when(pl.program_id(1) == 0)
    def _():
        st_ref[...] = jnp.zeros_like(st_ref)

    u = u_ref[...]
    ub = u.astype(BF16)
    for k in range(N_PACKS):
        uk = ub[:, k * pw:(k + 1) * pw]
        xr_ref[:, k * sw:(k + 1) * sw] = jnp.dot(uk, bre_ref[k], preferred_element_type=F32)
        xi_ref[:, k * sw:(k + 1) * sw] = jnp.dot(uk, bim_ref[k], preferred_element_type=F32)

    a_re = a_ref[0:1, :]
    a_im = a_ref[1:2, :]

    def step(t, state):
        s_re, s_im = state
        n_re = a_re * s_re - a_im * s_im + xr_ref[pl.ds(t, 1), :]
        n_im = a_re * s_im + a_im * s_re + xi_ref[pl.ds(t, 1), :]
        xr_ref[pl.ds(t, 1), :] = n_re
        xi_ref[pl.ds(t, 1), :] = n_im
        return n_re, n_im

    s_re, s_im = lax.fori_loop(0, tc, step, (st_ref[0:1, :], st_ref[1:2, :]), unroll=4)
    st_ref[0:1, :] = s_re
    st_ref[1:2, :] = s_im

    ys = []
    for k in range(N_PACKS):
        xr = xr_ref[:, k * sw:(k + 1) * sw].astype(BF16)
        xi = xi_ref[:, k * sw:(k + 1) * sw].astype(BF16)
        ys.append(jnp.dot(xr, cre_ref[k], preferred_element_type=F32)
                  + jnp.dot(xi, cimn_ref[k], preferred_element_type=F32))
    y = jnp.concatenate(ys, axis=-1) + d_ref[...] * u
    y = jax.nn.gelu(y).astype(BF16)
    ab = jnp.dot(y, wglu_ref[...], preferred_element_type=F32) + bglu_ref[...]
    o_ref[...] = (ab[:, :D_SSM] * jax.nn.sigmoid(ab[:, D_SSM:])).astype(o_ref.dtype)


def s5_glu(proj, n_batch, bbd_re, bbd_im, cbd_re, cbd_imn, a_flat, d, w_glu, b_glu, layer, *, tc=256):
    t = proj.shape[0]
    n_chunks = t // n_batch // tc
    c2 = lambda b, c: (0, 0)
    of_layer = lambda w: pl.BlockSpec((None,) + w.shape[1:], lambda b, c: (layer,) + (0,) * (w.ndim - 1))
    return pl.pallas_call(
        _s5_kernel,
        grid=(n_batch, n_chunks),
        in_specs=[pl.BlockSpec((tc, D_SSM), lambda b, c: (b * n_chunks + c, 0)),
                  of_layer(bbd_re), of_layer(bbd_im), of_layer(cbd_re), of_layer(cbd_imn),
                  of_layer(a_flat), pl.BlockSpec(d.shape, c2), of_layer(w_glu),
                  pl.BlockSpec(b_glu.shape, c2)],
        out_specs=pl.BlockSpec((tc, D_SSM), lambda b, c: (b * n_chunks + c, 0)),
        out_shape=jax.ShapeDtypeStruct((t, D_SSM), BF16),
        scratch_shapes=[pltpu.VMEM((tc, N_STATE), F32), pltpu.VMEM((tc, N_STATE), F32),
                        pltpu.VMEM((2, N_STATE), F32)],
        compiler_params=_cparams(("arbitrary", "arbitrary")),
        name="s5_glu",
    )(proj, bbd_re, bbd_im, cbd_re, cbd_imn, a_flat, d, w_glu, b_glu)


def _s5_weights(a_re, a_im, log_dt, b_re, b_im, c_re, c_im):
    depth, g = a_re.shape[:2]
    flat = lambda z: z.reshape((depth * g,) + z.shape[2:])
    outs = s5_discretize(flat(a_re), flat(a_im), flat(log_dt),
                         flat(b_re).transpose(0, 2, 1), flat(b_im).transpose(0, 2, 1))
    ab_re, ab_im, bb_re, bb_im = [z.reshape((depth, g) + z.shape[1:]) for z in outs]
    eye = jnp.eye(SSM_PACK, dtype=F32)

    def pack_b(bb):
        bb = bb.reshape(depth, N_PACKS, SSM_PACK, SSM_GROUP, 1, SSM_STATE)
        m = bb * eye[None, None, :, None, :, None]
        return m.reshape(depth, N_PACKS, SSM_PACK * SSM_GROUP, SSM_PACK * SSM_STATE).astype(BF16)

    def pack_c(c):
        c = c.reshape(depth, N_PACKS, SSM_PACK, SSM_GROUP, SSM_STATE)
        m = c.transpose(0, 1, 2, 4, 3)[:, :, :, :, None, :] * eye[None, None, :, None, :, None]
        return m.reshape(depth, N_PACKS, SSM_PACK * SSM_STATE, SSM_PACK * SSM_GROUP).astype(BF16)

    a_flat = jnp.stack([ab_re.reshape(depth, -1), ab_im.reshape(depth, -1)], axis=1)
    return pack_b(bb_re), pack_b(bb_im), pack_c(c_re), pack_c(-c_im), a_flat


def _compress_kernel(r_ref, n_ref, pos_ref, w1_ref, w2_ref, o_ref):
    r = r_ref[0]
    rows = r.shape[0]
    rowid = lax.broadcasted_iota(jnp.int32, (rows, 1), 0)
    nxt = jnp.where(rowid == rows - 1, n_ref[0, 0:1, :], pltpu.roll(r, rows - 1, 0))
    first = (r + pos_ref[0:1, :]).astype(BF16)
    second = (nxt + pos_ref[1:2, :]).astype(BF16)
    hid = (jnp.dot(first, w1_ref[0], preferred_element_type=F32)
           + jnp.dot(second, w1_ref[1], preferred_element_type=F32))
    hid = jax.nn.gelu(hid).astype(BF16)
    o_ref[0] = jnp.dot(hid, w2_ref[...], preferred_element_type=F32).astype(o_ref.dtype)


def compress(kc_rows, pos2, w1e, w2e, layer):
    b, nc, w = kc_rows.shape
    tr = 128
    sub = 8
    of_layer = lambda z: pl.BlockSpec((None,) + z.shape[1:], lambda i, j: (layer,) + (0,) * (z.ndim - 1))
    nxt_blk = lambda i, j: (i, jnp.minimum((j + 1) * (tr // sub), nc // sub - 1), 0)
    return pl.pallas_call(
        _compress_kernel,
        grid=(b, nc // tr),
        in_specs=[pl.BlockSpec((1, tr, w), lambda i, j: (i, j, 0)),
                  pl.BlockSpec((1, sub, w), nxt_blk),
                  of_layer(pos2), of_layer(w1e), of_layer(w2e)],
        out_specs=pl.BlockSpec((1, tr, N_KV * HEAD_DIM), lambda i, j: (i, j, 0)),
        out_shape=jax.ShapeDtypeStruct((b, nc, N_KV * HEAD_DIM), BF16),
        compiler_params=_cparams(("parallel", "parallel")),
        name="compress",
    )(kc_rows, kc_rows, pos2, w1e, w2e)


def _compress_weights(pos, w1, w2):
    depth = pos.shape[0]
    eye = jnp.eye(N_KV, dtype=F32)
    w1 = w1.reshape(depth, 2, CMP_STRIDE, 1, HEAD_DIM, 1, CMP_HIDDEN)
    w1e = (w1 * eye[None, None, None, :, None, :, None]).reshape(
        depth, 2, CMP_STRIDE * N_KV * HEAD_DIM, N_KV * CMP_HIDDEN).astype(BF16)
    w2e = (w2[:, None, :, None, :] * eye[None, :, None, :, None]).reshape(
        depth, N_KV * CMP_HIDDEN, N_KV * HEAD_DIM).astype(BF16)
    pos2 = jnp.broadcast_to(pos.reshape(depth, 2, CMP_STRIDE, 1, HEAD_DIM),
                            (depth, 2, CMP_STRIDE, N_KV, HEAD_DIM)).reshape(depth, 2, -1)
    return pos2, w1e, w2e


def _kv_prep_kernel(ks_ref, vs_ref, kw_ref, vw_ref, ka_ref, vst_ref, kwp_ref, vwtp_ref):
    j = pl.program_id(1)
    ts = ks_ref.shape[0]
    n_sel = ka_ref.shape[3] - HEAD_DIM
    blk0 = jnp.maximum(j - 1, 0) * (ts // SEL_BLOCK)
    row_blk = blk0 + lax.broadcasted_iota(jnp.int32, (ts, n_sel), 0) // SEL_BLOCK
    col = lax.broadcasted_iota(jnp.int32, (ts, n_sel), 1)
    onehot = jnp.where(row_blk == col, ONEHOT_BIG, 0.0).astype(BF16)
    ks = ks_ref[...]
    vs_t = vs_ref[...].T
    pad_rows = lax.broadcasted_iota(jnp.int32, (V_ROWS - HEAD_DIM, ts), 0)
    extra = jnp.where(pad_rows == 0, 1.0, 0.0).astype(BF16)
    for g in range(N_KV):
        lo, hi = g * HEAD_DIM, (g + 1) * HEAD_DIM
        ka_ref[0, g, :, 0:n_sel] = onehot
        ka_ref[0, g, :, n_sel:] = ks[:, lo:hi].astype(BF16)
        vst_ref[0, g, 0:HEAD_DIM, :] = vs_t[lo:hi].astype(BF16)
        vst_ref[0, g, HEAD_DIM:, :] = extra

    @pl.when(j == 0)
    def _():
        kwp_ref[...] = jnp.zeros_like(kwp_ref)
        vwtp_ref[...] = jnp.zeros_like(vwtp_ref)

    @pl.when(j > 0)
    def _():
        kw = kw_ref[...]
        vw_t = vw_ref[...].T
        for g in range(N_KV):
            lo, hi = g * HEAD_DIM, (g + 1) * HEAD_DIM
            kwp_ref[0, g] = kw[:, lo:hi].astype(BF16)
            vwtp_ref[0, g] = vw_t[lo:hi].astype(BF16)


def kv_prep(proj, n_batch, seq):
    ts = WINDOW
    nblk = seq // ts
    n_sel = seq // SEL_BLOCK
    kvw = N_KV * HEAD_DIM
    src = lambda col: pl.BlockSpec((ts, kvw), lambda b, j: (b * nblk + jnp.maximum(j - 1, 0), col // kvw))
    same = lambda b, j: (b, 0, jnp.maximum(j - 1, 0), 0)
    same_t = lambda b, j: (b, 0, 0, jnp.maximum(j - 1, 0))
    return pl.pallas_call(
        _kv_prep_kernel,
        grid=(n_batch, nblk + 1),
        in_specs=[src(COL_KS), src(COL_VS), src(COL_KW), src(COL_VW)],
        out_specs=[pl.BlockSpec((1, N_KV, ts, n_sel + HEAD_DIM), same),
                   pl.BlockSpec((1, N_KV, V_ROWS, ts), same_t),
                   pl.BlockSpec((1, N_KV, ts, HEAD_DIM), lambda b, j: (b, 0, j, 0)),
                   pl.BlockSpec((1, N_KV, HEAD_DIM, ts), lambda b, j: (b, 0, 0, j))],
        out_shape=[jax.ShapeDtypeStruct((n_batch, N_KV, seq, n_sel + HEAD_DIM), BF16),
                   jax.ShapeDtypeStruct((n_batch, N_KV, V_ROWS, seq), BF16),
                   jax.ShapeDtypeStruct((n_batch, N_KV, seq + WINDOW, HEAD_DIM), BF16),
                   jax.ShapeDtypeStruct((n_batch, N_KV, HEAD_DIM, seq + WINDOW), BF16)],
        compiler_params=_cparams(("parallel", "arbitrary")),
        name="kv_prep",
    )(proj, proj, proj, proj)


def _softmax_cols(s):
    m = jnp.max(s, axis=0, keepdims=True)
    m = jnp.where(m < 0.5 * NEG, 0.0, m)
    e = jnp.exp2(s - m)
    return e * (1.0 / jnp.maximum(jnp.sum(e, axis=0, keepdims=True), 1e-30))


def _cmp_branch(a, qt, gate, k_ref, vt_ref, tbl_ref, smat_ref, o_ref, st_ref, rank_ref):
    n_row_tiles = k_ref.shape[2] // CMP_LANE
    ap = a // (FAR_DIST // Q_TILE)

    def attend(n):
        rows = n * CMP_LANE
        s = jnp.dot(k_ref[0, 0, 0:rows, :], qt, preferred_element_type=F32)
        tbl = tbl_ref[0, 0]
        pieces = [s[c * CMP_LANE:(c + 1) * CMP_LANE] for c in range(n)]
        pieces[-1] = pieces[-1] + tbl[CMP_LANE:]
        if n > 1:
            pieces[-2] = pieces[-2] + tbl[:CMP_LANE]
        p = _softmax_cols(jnp.concatenate(pieces, axis=0))
        o_ref[...] = gate * jnp.dot(vt_ref[0, 0, :, 0:rows], p.astype(BF16), preferred_element_type=F32)
        imp = (p[:, 0:Q_TILE] + p[:, Q_TILE:2 * Q_TILE] + p[:, 2 * Q_TILE:3 * Q_TILE]
               + p[:, 3 * Q_TILE:4 * Q_TILE])
        hi = imp.astype(BF16)
        r1 = imp - hi.astype(F32)
        mid = r1.astype(BF16)
        lo = (r1 - mid.astype(F32)).astype(BF16)
        smat = smat_ref[:, 0:rows]
        st_ref[...] = (jnp.dot(smat, hi, preferred_element_type=F32)
                       + jnp.dot(smat, mid, preferred_element_type=F32)
                       + jnp.dot(smat, lo, preferred_element_type=F32))

    for n in range(1, n_row_tiles + 1):
        pl.when(ap == n - 1)(lambda n=n: attend(n))

    o = o_ref[...]
    st = st_ref[...]
    t = a * Q_TILE + lax.broadcasted_iota(jnp.int32, st.shape, 1)
    jj = lax.broadcasted_iota(jnp.int32, st.shape, 0)
    cur_blk = t // SEL_BLOCK
    valid = jj * SEL_BLOCK <= t
    forced = (jj == 0) | (jj == cur_blk) | (jj == cur_blk - 1)
    st = jnp.where(valid, jnp.where(forced, FORCED_SCORE, st), -1.0)
    st_ref[...] = st
    rank_ref[...] = jnp.zeros_like(rank_ref)
    sub = 8
    chunk = 16
    n_sel = st.shape[0]
    last_valid = (a * Q_TILE + Q_TILE - 1) // SEL_BLOCK
    for q0 in range(0, n_sel, chunk):
        @pl.when(q0 <= last_valid)
        def _(q0=q0):
            for v0 in range(0, n_sel, chunk):
                @pl.when(v0 <= last_valid)
                def _(v0=v0):
                    for v in range(v0 // sub, min(v0 + chunk, n_sel) // sub):
                        sv = st_ref[v * sub:(v + 1) * sub, :]
                        jv = jj[v * sub:(v + 1) * sub]
                        cnt = rank_ref[v * sub:(v + 1) * sub, :]
                        for jp in range(q0, min(q0 + chunk, n_sel)):
                            row = st_ref[jp:jp + 1, :]
                            if jp < v * sub:
                                ahead = row >= sv
                            elif jp >= (v + 1) * sub:
                                ahead = row > sv
                            else:
                                ahead = (row > sv) | ((row == sv) & (jv > jp))
                            cnt = cnt + jnp.where(ahead, 1.0, 0.0)
                        rank_ref[v * sub:(v + 1) * sub, :] = cnt

    keep = (rank_ref[...] < SEL_TOP) & (st >= 0.0)
    return o, (keep.astype(F32) - 1.0).astype(BF16)


def _sel_branch(a, gate, ka_ref, vt_ref, tbl_ref, qa_ref, m_ref, acc_ref):
    tk = SEL_KEY_TILE
    acc_ref[...] = jnp.zeros_like(acc_ref)
    n_tiles = a // (tk // Q_TILE) + 1
    far_pairs = jnp.maximum(0, (a - (SEL_NEAR_TILES - 5)) // 4)
    far_octs = far_pairs // 4
    far_quads = (far_pairs - 4 * far_octs) // 2
    first_near = 2 * far_pairs
    near_quads = (n_tiles - first_near) // 4
    near_pairs = (n_tiles - first_near - 4 * near_quads + 1) // 2

    def key_off(c):
        return pl.multiple_of(jnp.minimum(c, n_tiles - 1) * tk, tk)

    def attend(c0, n, biased):
        m_old = m_ref[...]
        acc_old = acc_ref[...]

        whole = n >= 4
        span = pl.ds(pl.multiple_of(c0 * tk, tk), n * tk)

        def scores():
            if whole:
                s_all = jnp.dot(ka_ref[0, 0, span, :], qa_ref[...], preferred_element_type=F32)
                ss = [s_all[h * tk:(h + 1) * tk] for h in range(n)]
            else:
                ss = [jnp.dot(ka_ref[0, 0, pl.ds(key_off(c0 + h), tk), :], qa_ref[...],
                              preferred_element_type=F32) for h in range(n)]
            if biased:
                ss = [s + tbl_ref[0, jnp.where(c0 + h < n_tiles, a - 2 * (c0 + h), SEL_NEAR_TILES - 1)]
                      for h, s in enumerate(ss)]
            return ss

        def weighted(ss, m):
            ps = [jnp.exp2(s - m).astype(BF16) for s in ss]
            if whole:
                return jnp.dot(vt_ref[0, 0, :, span], jnp.concatenate(ps, axis=0), preferred_element_type=F32)
            return sum(jnp.dot(vt_ref[0, 0, :, pl.ds(key_off(c0 + h), tk)], p, preferred_element_type=F32)
                       for h, p in enumerate(ps))

        ss = scores()
        m_new = m_old
        for s in ss:
            m_new = jnp.maximum(m_new, jnp.max(s, axis=0, keepdims=True))
        acc_ref[...] = (acc_old + weighted(ss, m_old)) * jnp.exp2(m_old - m_new)
        m_ref[...] = m_new

        @pl.when(jnp.max(m_new - m_old) > SAFE_LOG2_RANGE)
        def _():
            acc_ref[...] = jnp.exp2(m_old - m_new) * acc_old + weighted(scores(), m_new)

    def body(first, n, biased):
        def step(i, carry):
            attend(first + n * i, n, biased)
            return carry
        return step

    sub = 8
    m_ref[...] = jnp.dot(ka_ref[0, 0, 0:sub, :], qa_ref[...], preferred_element_type=F32)[0:1, :]
    lax.fori_loop(0, far_octs, body(0, 8, False), 0)
    lax.fori_loop(0, far_quads, body(8 * far_octs, 4, False), 0)
    lax.fori_loop(0, far_pairs - 4 * far_octs - 2 * far_quads, body(8 * far_octs + 4 * far_quads, 2, False), 0)
    lax.fori_loop(0, near_quads, body(first_near, 4, True), 0)
    lax.fori_loop(0, near_pairs, body(first_near + 4 * near_quads, 2, True), 0)
    acc = acc_ref[...]
    return acc[0:HEAD_DIM] * (gate / acc[HEAD_DIM:HEAD_DIM + 1])


def _win_branch(a, qt, gate, k_ref, vt_ref, tbl_ref):
    span = WINDOW + Q_TILE
    off = pl.multiple_of(a * Q_TILE, Q_TILE)
    s = jnp.dot(k_ref[0, 0, pl.ds(off, span), :], qt, preferred_element_type=F32)
    s = s + tbl_ref[0, 0]
    e = jnp.exp2(s - jnp.max(s, axis=0, keepdims=True))
    ev = jnp.dot(vt_ref[0, 0, :, pl.ds(off, span)], e.astype(BF16), preferred_element_type=F32)
    return ev * (gate / jnp.sum(e, axis=0, keepdims=True))


def _nsa_kernel(q_ref, g_ref, kc_ref, vtc_ref, tblc_ref, smat_ref, ka_ref, vst_ref, tbls_ref,
                kw_ref, vwt_ref, tblw_ref, o_ref, gt_ref, oc_ref, st_ref, rank_ref, qa_ref, m_ref, acc_ref):
    g = pl.program_id(0)
    a = pl.program_id(2)
    n_sel = smat_ref.shape[0]
    half = 2 * HEAD_DIM
    q = q_ref[...] * (HEAD_DIM ** -0.5) * LOG2E
    t0 = q[:, :half].T
    t1 = q[:, half:].T
    qt = jnp.concatenate([t0[:HEAD_DIM], t0[HEAD_DIM:], t1[:HEAD_DIM], t1[HEAD_DIM:]],
                         axis=1).astype(BF16)
    gt_ref[...] = g_ref[:, :Q_TILE].T

    def gate(branch):
        rows = [gt_ref[pl.ds((g * GQA_R + r) * 3 + branch, 1), :] for r in range(GQA_R)]
        return jax.nn.sigmoid(jnp.concatenate(rows, axis=1))

    o, sel = _cmp_branch(a, qt, gate(0), kc_ref, vtc_ref, tblc_ref, smat_ref, oc_ref, st_ref, rank_ref)
    for r in range(GQA_R):
        qa_ref[0:n_sel, r * Q_TILE:(r + 1) * Q_TILE] = sel
    qa_ref[n_sel:, :] = qt
    o = o + _sel_branch(a, gate(1), ka_ref, vst_ref, tbls_ref, qa_ref, m_ref, acc_ref)
    o = o + _win_branch(a, qt, gate(2), kw_ref, vwt_ref, tblw_ref)
    u0 = jnp.concatenate([o[:, 0:Q_TILE], o[:, Q_TILE:2 * Q_TILE]], axis=0).T
    u1 = jnp.concatenate([o[:, 2 * Q_TILE:3 * Q_TILE], o[:, 3 * Q_TILE:]], axis=0).T
    o_ref[...] = jnp.concatenate([u0, u1], axis=1).astype(o_ref.dtype)


def nsa_attention(proj, n_batch, seq, k_cmp, vt_cmp, ka, vst, kwp, vwtp, tables, smat):
    tbl_c, tbl_s, tbl_w = tables
    n_a = seq // Q_TILE
    nc = k_cmp.shape[2]
    n_sel = smat.shape[0]
    dh = HEAD_DIM
    gw = GQA_R * HEAD_DIM
    tps = FAR_DIST // Q_TILE
    per_bg = lambda gi, bi, ai: (bi, gi, 0, 0)
    return pl.pallas_call(
        _nsa_kernel,
        grid=(N_KV, n_batch, n_a),
        in_specs=[pl.BlockSpec((Q_TILE, gw), lambda gi, bi, ai: (bi * n_a + ai, COL_Q // gw + gi)),
                  pl.BlockSpec((Q_TILE, gw), lambda gi, bi, ai: (bi * n_a + ai, COL_GATE // gw)),
                  pl.BlockSpec((1, 1, nc, dh), per_bg),
                  pl.BlockSpec((1, 1, dh, nc), per_bg),
                  pl.BlockSpec((1, 1, 2 * CMP_LANE, ROWS), lambda gi, bi, ai: (gi, ai % tps, 0, 0)),
                  pl.BlockSpec(smat.shape, lambda gi, bi, ai: (0, 0)),
                  pl.BlockSpec((1, 1, seq, n_sel + dh), per_bg),
                  pl.BlockSpec((1, 1, V_ROWS, seq), per_bg),
                  pl.BlockSpec((1, SEL_NEAR_TILES, SEL_KEY_TILE, ROWS), lambda gi, bi, ai: (gi, 0, 0, 0)),
                  pl.BlockSpec((1, 1, seq + WINDOW, dh), per_bg),
                  pl.BlockSpec((1, 1, dh, seq + WINDOW), per_bg),
                  pl.BlockSpec((1, 1, WINDOW + Q_TILE, ROWS),
                               lambda gi, bi, ai: (gi, jnp.minimum(ai, WINDOW // Q_TILE), 0, 0))],
        out_specs=pl.BlockSpec((Q_TILE, gw), lambda gi, bi, ai: (bi * n_a + ai, gi)),
        out_shape=jax.ShapeDtypeStruct((n_batch * seq, N_HEADS * HEAD_DIM), BF16),
        scratch_shapes=[pltpu.VMEM((Q_TILE, Q_TILE), F32),
                        pltpu.VMEM((dh, ROWS), F32),
                        pltpu.VMEM((n_sel, Q_TILE), F32), pltpu.VMEM((n_sel, Q_TILE), F32),
                        pltpu.VMEM((n_sel + dh, ROWS), BF16), pltpu.VMEM((1, ROWS), F32),
                        pltpu.VMEM((V_ROWS, ROWS), F32)],
        compiler_params=_cparams(("parallel", "parallel", "arbitrary")),
        name="nsa_attention",
    )(proj, proj, k_cmp, vt_cmp, tbl_c, smat, ka, vst, tbl_s, kwp, vwtp, tbl_w)


def _rel_bucket(dist):
    n = jnp.maximum(dist, 0)
    nf = jnp.maximum(n, 1).astype(F32)
    large = RPB_MAX_EXACT + (jnp.log(nf / RPB_MAX_EXACT) / math.log(RPB_MAX_DIST / RPB_MAX_EXACT)
                             * (N_BUCKETS - RPB_MAX_EXACT)).astype(jnp.int32)
    large = jnp.minimum(large, N_BUCKETS - 1)
    return jnp.where(n < RPB_MAX_EXACT, n, large)


def _shifted_rows(e, n_rows, stride):
    return pltpu.roll(jnp.broadcast_to(e, (n_rows, e.shape[1])), 0, 1, stride=stride, stride_axis=0)


def _bias_tables_kernel(ec_ref, es_ref, ew_ref, tc_ref, ts_ref, tw_ref):
    n_m = tc_ref.shape[1]
    n_am = tc_ref.shape[0]
    mc = _shifted_rows(ec_ref[...], n_m, CMP_STRIDE)
    first = mc.shape[1] - n_am * Q_TILE
    for am in range(n_am):
        tc_ref[am] = mc[:, first + am * Q_TILE:first + (am + 1) * Q_TILE]
    tk = ts_ref.shape[1]
    for k in range(ts_ref.shape[0]):
        ts_ref[k] = _shifted_rows(es_ref[k:k + 1, :], tk, 1)[:, tk:tk + Q_TILE]
    span = tw_ref.shape[1]
    mw = _shifted_rows(ew_ref[...], span, 1)[:, span:span + Q_TILE]
    kj = lax.broadcasted_iota(jnp.int32, mw.shape, 0)
    for v in range(tw_ref.shape[0]):
        tw_ref[v] = jnp.where(kj >= WINDOW - v * Q_TILE, mw, NEG)


def _bias_tables(rpb_table):
    tab = rpb_table.astype(F32)
    by_dist = tab[_rel_bucket(jnp.arange(FAR_DIST))].T * LOG2E
    rel = by_dist - (tab[N_BUCKETS - 1] * LOG2E)[:, None]
    neg = lambda n: jnp.full((N_HEADS, n), NEG, F32)
    zero = lambda n: jnp.zeros((N_HEADS, n), F32)
    tk, span, n_m, n_am = SEL_KEY_TILE, WINDOW + Q_TILE, 2 * CMP_LANE, FAR_DIST // Q_TILE

    lead = CMP_STRIDE * n_m - (FAR_DIST - CMP_BLOCK + 1)
    e_c = jnp.concatenate([neg(lead), rel, zero(CMP_STRIDE * n_m - lead)], axis=1)[:, None, :]
    ext = jnp.concatenate([neg(tk), rel], axis=1)
    e_s = jnp.stack([ext[:, Q_TILE * k:Q_TILE * k + tk + Q_TILE] for k in range(SEL_NEAR_TILES - 1)]
                    + [neg(tk + Q_TILE)], axis=1)
    e_w = jnp.concatenate([neg(Q_TILE), by_dist[:, :WINDOW], neg(Q_TILE)], axis=1)[:, None, :]

    n_var = WINDOW // Q_TILE + 1
    head = lambda h: (h // GQA_R, 0, 0, h % GQA_R)
    return pl.pallas_call(
        _bias_tables_kernel,
        grid=(N_HEADS,),
        in_specs=[pl.BlockSpec((None,) + e_c.shape[1:], lambda h: (h, 0, 0)),
                  pl.BlockSpec((None,) + e_s.shape[1:], lambda h: (h, 0, 0)),
                  pl.BlockSpec((None,) + e_w.shape[1:], lambda h: (h, 0, 0))],
        out_specs=[pl.BlockSpec((None, n_am, n_m, Q_TILE), head),
                   pl.BlockSpec((None, SEL_NEAR_TILES, tk, Q_TILE), head),
                   pl.BlockSpec((None, n_var, span, Q_TILE), head)],
        out_shape=[jax.ShapeDtypeStruct((N_KV, n_am, n_m, ROWS), F32),
                   jax.ShapeDtypeStruct((N_KV, SEL_NEAR_TILES, tk, ROWS), F32),
                   jax.ShapeDtypeStruct((N_KV, n_var, span, ROWS), F32)],
        compiler_params=_cparams(("parallel",)),
        name="bias_tables",
    )(e_c, e_s, e_w)


def _slc_matrix(n_cmp_pad, n_sel):
    sel_ratio, cmp_ratio = SEL_BLOCK // CMP_STRIDE, CMP_BLOCK // CMP_STRIDE
    mat = np.zeros((n_sel, n_cmp_pad), np.float32)
    for j in range(n_sel):
        for mm in range(sel_ratio):
            for nn in range(cmp_ratio):
                idx = sel_ratio * j - mm - nn
                if 0 <= idx < n_cmp_pad - 1:
                    mat[j, idx] += 1.0
    return jnp.asarray(mat, BF16)


def nsa_mixer(proj, n_batch, seq, cmp_k_w, cmp_v_w, tables, smat, layer):
    kv = N_KV * HEAD_DIM
    rows16 = lambda col: proj[:, col:col + kv].reshape(n_batch, seq // CMP_STRIDE, CMP_STRIDE * kv)
    k_cmp = compress(rows16(COL_KC), *cmp_k_w, layer)
    v_cmp = compress(rows16(COL_VC), *cmp_v_w, layer)
    k_cmp = k_cmp.reshape(n_batch, -1, N_KV, HEAD_DIM).transpose(0, 2, 1, 3)
    vt_cmp = v_cmp.reshape(n_batch, -1, N_KV, HEAD_DIM).transpose(0, 2, 3, 1)
    ka, vst, kwp, vwtp = kv_prep(proj, n_batch, seq)
    return nsa_attention(proj, n_batch, seq, k_cmp, vt_cmp, ka, vst, kwp, vwtp, tables, smat)


def kernel(x, rpb_table, attn_norm, ffn_norm, final_norm, w_in, ssm_a_re, ssm_a_im, ssm_log_dt,
           ssm_b_re, ssm_b_im, ssm_c_re, ssm_c_im, ssm_d, w_glu, b_glu, cmp_pos_k, cmp_w1_k, cmp_w2_k,
           cmp_pos_v, cmp_w1_v, cmp_w2_v, w_out, w_ffn_gate, w_ffn_up, w_ffn_down):
    n_batch, seq, d_model = x.shape
    depth = w_in.shape[0]
    tables = _bias_tables(rpb_table)
    smat = _slc_matrix(seq // CMP_STRIDE, seq // SEL_BLOCK)

    w_in_b = jnp.pad(w_in, ((0, 0), (0, 0), (0, IN_COLS_PAD - IN_COLS))).astype(BF16)
    w_glu_b, w_out_b = w_glu.astype(BF16), w_out.astype(BF16)
    w_gate_b, w_up_b, w_down_b = w_ffn_gate.astype(BF16), w_ffn_up.astype(BF16), w_ffn_down.astype(BF16)

    s5w = _s5_weights(ssm_a_re, ssm_a_im, ssm_log_dt, ssm_b_re, ssm_b_im, ssm_c_re, ssm_c_im)
    cmp_k_w = _compress_weights(cmp_pos_k, cmp_w1_k, cmp_w2_k)
    cmp_v_w = _compress_weights(cmp_pos_v, cmp_w1_v, cmp_w2_v)

    xf = x.reshape(n_batch * seq, d_model)
    for l in range(depth):
        proj = norm_matmul(xf, attn_norm[l].reshape(1, -1), w_in_b, l)
        y_ssm = s5_glu(proj, n_batch, *s5w, ssm_d[l].reshape(1, -1), w_glu_b, b_glu[l].reshape(1, -1), l)
        y_nsa = nsa_mixer(proj, n_batch, seq, cmp_k_w, cmp_v_w, tables, smat, l)
        xf = out_proj(xf, y_ssm, y_nsa, w_out_b, l)
        xf = ffn(xf, ffn_norm[l].reshape(1, -1), w_gate_b, w_up_b, w_down_b, l, final_norm.reshape(1, -1),
                 norm_out=l == depth - 1)
    return xf.reshape(n_batch, seq, d_model)
```

```python
import functools
import math

import jax
import jax.numpy as jnp
import numpy as np
from jax import lax
from jax.experimental import pallas as pl
from jax.experimental.pallas import tpu as pltpu

F32 = jnp.float32
BF16 = jnp.bfloat16

D_MODEL = 2048
DEPTH = 4
D_SSM = 1024
SSM_GROUP = 16
N_SSM_GROUPS = 64
SSM_STATE = 64
N_STATE = N_SSM_GROUPS * SSM_STATE
SSM_PACK = 16
N_PACKS = N_SSM_GROUPS // SSM_PACK
N_HEADS = 16
N_KV = 4
GQA_R = 4
HEAD_DIM = 64
CMP_BLOCK = 32
CMP_STRIDE = 16
CMP_HIDDEN = 128
SEL_BLOCK = 64
SEL_TOP = 16
WINDOW = 512
Q_TILE = 128
ROWS = GQA_R * Q_TILE
FORCED_SCORE = 1e4
N_BUCKETS = 32
RPB_MAX_EXACT = 16
RPB_MAX_DIST = 1024
D_FF = 5632
NORM_EPS = 1e-6
IN_COLS = 3632
IN_COLS_PAD = 3840
COL_U, COL_Q, COL_KC, COL_VC, COL_KS, COL_VS, COL_KW, COL_VW, COL_GATE = (
    0, 1024, 2048, 2304, 2560, 2816, 3072, 3328, 3584)
NEG = -1e30
ONEHOT_BIG = 1e30
FAR_DIST = 2048
SEL_KEY_TILE = 256
SEL_NEAR_TILES = 12
LOG2E = math.log2(math.e)
SAFE_LOG2_RANGE = 64.0
CMP_LANE = 128
V_ROWS = 80
VMEM_LIMIT = 56 * 1024 * 1024


def _cparams(sem):
    return pltpu.CompilerParams(dimension_semantics=sem, vmem_limit_bytes=VMEM_LIMIT)


def _rms(x, g):
    ms = jnp.mean(x * x, axis=-1, keepdims=True)
    return x * lax.rsqrt(ms + NORM_EPS) * g


def _norm_matmul_kernel(x_ref, g_ref, w_ref, o_ref, h_ref):
    @pl.when(pl.program_id(1) == 0)
    def _():
        h_ref[...] = _rms(x_ref[...], g_ref[...]).astype(BF16)

    o_ref[...] = jnp.dot(h_ref[...], w_ref[...], preferred_element_type=F32)


def norm_matmul(x, g, w, layer, *, tm=1024, tn=768):
    m, k = x.shape
    n = w.shape[2]
    return pl.pallas_call(
        _norm_matmul_kernel,
        grid=(m // tm, n // tn),
        in_specs=[pl.BlockSpec((tm, k), lambda i, j: (i, 0)),
                  pl.BlockSpec((1, k), lambda i, j: (0, 0)),
                  pl.BlockSpec((None, k, tn), lambda i, j: (layer, 0, j))],
        out_specs=pl.BlockSpec((tm, tn), lambda i, j: (i, j)),
        out_shape=jax.ShapeDtypeStruct((m, n), F32),
        scratch_shapes=[pltpu.VMEM((tm, k), BF16)],
        compiler_params=_cparams(("parallel", "arbitrary")),
        name="norm_matmul",
    )(x, g, w)


def _out_proj_kernel(x_ref, ys_ref, yn_ref, w1_ref, w2_ref, o_ref):
    o_ref[...] = (x_ref[...]
                  + jnp.dot(ys_ref[...], w1_ref[...], preferred_element_type=F32)
                  + jnp.dot(yn_ref[...], w2_ref[...], preferred_element_type=F32))


def out_proj(x, y_ssm, y_nsa, w_out, layer, *, tm=512):
    m, d = x.shape
    k = y_ssm.shape[1]
    row = lambda i: (i, 0)
    return pl.pallas_call(
        _out_proj_kernel,
        grid=(m // tm,),
        in_specs=[pl.BlockSpec((tm, d), row), pl.BlockSpec((tm, k), row), pl.BlockSpec((tm, k), row),
                  pl.BlockSpec((None, k, d), lambda i: (layer, 0, 0)),
                  pl.BlockSpec((None, k, d), lambda i: (layer, 1, 0))],
        out_specs=pl.BlockSpec((tm, d), row),
        out_shape=jax.ShapeDtypeStruct((m, d), F32),
        compiler_params=_cparams(("parallel",)),
        name="out_proj",
    )(x, y_ssm, y_nsa, w_out, w_out)


def _ffn_kernel(x_ref, g_ref, wg_ref, wu_ref, wd_ref, g_out_ref, o_ref, h_ref, *, norm_out):
    f = pl.program_id(1)

    @pl.when(f == 0)
    def _():
        x = x_ref[...]
        h_ref[...] = _rms(x, g_ref[...]).astype(BF16)
        o_ref[...] = x

    h = h_ref[...]
    gate = jnp.dot(h, wg_ref[...], preferred_element_type=F32)
    up = jnp.dot(h, wu_ref[...], preferred_element_type=F32)
    act = (jax.nn.silu(gate) * up).astype(BF16)
    o_ref[...] += jnp.dot(act, wd_ref[...], preferred_element_type=F32)

    if norm_out:
        @pl.when(f == pl.num_programs(1) - 1)
        def _():
            o_ref[...] = _rms(o_ref[...], g_out_ref[...])


def ffn(x, g, w_gate, w_up, w_down, layer, g_out, *, norm_out, tm=1024, tf=256):
    m, d = x.shape
    dff = w_gate.shape[2]
    return pl.pallas_call(
        functools.partial(_ffn_kernel, norm_out=norm_out),
        grid=(m // tm, dff // tf),
        in_specs=[pl.BlockSpec((tm, d), lambda i, f: (i, 0)),
                  pl.BlockSpec((1, d), lambda i, f: (0, 0)),
                  pl.BlockSpec((None, d, tf), lambda i, f: (layer, 0, f)),
                  pl.BlockSpec((None, d, tf), lambda i, f: (layer, 0, f)),
                  pl.BlockSpec((None, tf, d), lambda i, f: (layer, f, 0)),
                  pl.BlockSpec((1, d), lambda i, f: (0, 0))],
        out_specs=pl.BlockSpec((tm, d), lambda i, f: (i, 0)),
        out_shape=jax.ShapeDtypeStruct((m, d), F32),
        scratch_shapes=[pltpu.VMEM((tm, d), BF16)],
        compiler_params=_cparams(("parallel", "arbitrary")),
        name="ffn",
    )(x, g, w_gate, w_up, w_down, g_out)


def _s5_discretize_kernel(are_ref, aim_ref, ldt_ref, bre_ref, bim_ref,
                          abre_ref, abim_ref, bbre_ref, bbim_ref):
    lam_re = jnp.minimum(are_ref[...], -1e-4)
    lam_im = aim_ref[...]
    dt = jnp.exp(ldt_ref[...])
    mag = jnp.exp(lam_re * dt)
    ab_re = mag * jnp.cos(lam_im * dt)
    ab_im = mag * jnp.sin(lam_im * dt)
    nr, ni = ab_re - 1.0, ab_im
    den = lam_re * lam_re + lam_im * lam_im
    f_re = (nr * lam_re + ni * lam_im) / den
    f_im = (ni * lam_re - nr * lam_im) / den
    br, bi = bre_ref[...], bim_ref[...]
    abre_ref[...] = ab_re
    abim_ref[...] = ab_im
    bbre_ref[...] = f_re * br - f_im * bi
    bbim_ref[...] = f_re * bi + f_im * br


def s5_discretize(a_re, a_im, log_dt, b_re_t, b_im_t):
    g, p = a_re.shape
    h = b_re_t.shape[1]
    outs = pl.pallas_call(
        _s5_discretize_kernel,
        out_shape=[jax.ShapeDtypeStruct((g, 1, p), F32), jax.ShapeDtypeStruct((g, 1, p), F32),
                   jax.ShapeDtypeStruct((g, h, p), F32), jax.ShapeDtypeStruct((g, h, p), F32)],
        name="s5_discretize",
    )(a_re.reshape(g, 1, p), a_im.reshape(g, 1, p), log_dt.reshape(g, 1, 1), b_re_t, b_im_t)
    return outs


def _s5_kernel(u_ref, bre_ref, bim_ref, cre_ref, cimn_ref, a_ref, d_ref, wglu_ref, bglu_ref,
               o_ref, xr_ref, xi_ref, st_ref):
    tc = u_ref.shape[0]
    pw = SSM_PACK * SSM_GROUP
    sw = SSM_PACK * SSM_STATE

    @pl.when(pl.program_id(1) == 0)
    def _():
        st_ref[...] = jnp.zeros_like(st_ref)

    u = u_ref[...]
    ub = u.astype(BF16)
    for k in range(N_PACKS):
        uk = ub[:, k * pw:(k + 1) * pw]
        xr_ref[:, k * sw:(k + 1) * sw] = jnp.dot(uk, bre_ref[k], preferred_element_type=F32)
        xi_ref[:, k * sw:(k + 1) * sw] = jnp.dot(uk, bim_ref[k], preferred_element_type=F32)

    a_re = a_ref[0:1, :]
    a_im = a_ref[1:2, :]

    def step(t, state):
        s_re, s_im = state
        n_re = a_re * s_re - a_im * s_im + xr_ref[pl.ds(t, 1), :]
        n_im = a_re * s_im + a_im * s_re + xi_ref[pl.ds(t, 1), :]
        xr_ref[pl.ds(t, 1), :] = n_re
        xi_ref[pl.ds(t, 1), :] = n_im
        return n_re, n_im

    s_re, s_im = lax.fori_loop(0, tc, step, (st_ref[0:1, :], st_ref[1:2, :]), unroll=4)
    st_ref[0:1, :] = s_re
    st_ref[1:2, :] = s_im

    ys = []
    for k in range(N_PACKS):
        xr = xr_ref[:, k * sw:(k + 1) * sw].astype(BF16)
        xi = xi_ref[:, k * sw:(k + 1) * sw].astype(BF16)
        ys.append(jnp.dot(xr, cre_ref[k], preferred_element_type=F32)
                  + jnp.dot(xi, cimn_ref[k], preferred_element_type=F32))
    y = jnp.concatenate(ys, axis=-1) + d_ref[...] * u
    y = jax.nn.gelu(y).astype(BF16)
    ab = jnp.dot(y, wglu_ref[...], preferred_element_type=F32) + bglu_ref[...]
    o_ref[...] = (ab[:, :D_SSM] * jax.nn.sigmoid(ab[:, D_SSM:])).astype(o_ref.dtype)


def s5_glu(proj, n_batch, bbd_re, bbd_im, cbd_re, cbd_imn, a_flat, d, w_glu, b_glu, layer, *, tc=256):
    t = proj.shape[0]
    n_chunks = t // n_batch // tc
    c2 = lambda b, c: (0, 0)
    of_layer = lambda w: pl.BlockSpec((None,) + w.shape[1:], lambda b, c: (layer,) + (0,) * (w.ndim - 1))
    return pl.pallas_call(
        _s5_kernel,
        grid=(n_batch, n_chunks),
        in_specs=[pl.BlockSpec((tc, D_SSM), lambda b, c: (b * n_chunks + c, 0)),
                  of_layer(bbd_re), of_layer(bbd_im), of_layer(cbd_re), of_layer(cbd_imn),
                  of_layer(a_flat), pl.BlockSpec(d.shape, c2), of_layer(w_glu),
                  pl.BlockSpec(b_glu.shape, c2)],
        out_specs=pl.BlockSpec((tc, D_SSM), lambda b, c: (b * n_chunks + c, 0)),
        out_shape=jax.ShapeDtypeStruct((t, D_SSM), BF16),
        scratch_shapes=[pltpu.VMEM((tc, N_STATE), F32), pltpu.VMEM((tc, N_STATE), F32),
                        pltpu.VMEM((2, N_STATE), F32)],
        compiler_params=_cparams(("arbitrary", "arbitrary")),
        name="s5_glu",
    )(proj, bbd_re, bbd_im, cbd_re, cbd_imn, a_flat, d, w_glu, b_glu)


def _s5_weights(a_re, a_im, log_dt, b_re, b_im, c_re, c_im):
    depth, g = a_re.shape[:2]
    flat = lambda z: z.reshape((depth * g,) + z.shape[2:])
    outs = s5_discretize(flat(a_re), flat(a_im), flat(log_dt),
                         flat(b_re).transpose(0, 2, 1), flat(b_im).transpose(0, 2, 1))
    ab_re, ab_im, bb_re, bb_im = [z.reshape((depth, g) + z.shape[1:]) for z in outs]
    eye = jnp.eye(SSM_PACK, dtype=F32)

    def pack_b(bb):
        bb = bb.reshape(depth, N_PACKS, SSM_PACK, SSM_GROUP, 1, SSM_STATE)
        m = bb * eye[None, None, :, None, :, None]
        return m.reshape(depth, N_PACKS, SSM_PACK * SSM_GROUP, SSM_PACK * SSM_STATE).astype(BF16)

    def pack_c(c):
        c = c.reshape(depth, N_PACKS, SSM_PACK, SSM_GROUP, SSM_STATE)
        m = c.transpose(0, 1, 2, 4, 3)[:, :, :, :, None, :] * eye[None, None, :, None, :, None]
        return m.reshape(depth, N_PACKS, SSM_PACK * SSM_STATE, SSM_PACK * SSM_GROUP).astype(BF16)

    a_flat = jnp.stack([ab_re.reshape(depth, -1), ab_im.reshape(depth, -1)], axis=1)
    return pack_b(bb_re), pack_b(bb_im), pack_c(c_re), pack_c(-c_im), a_flat


def _compress_kernel(r_ref, n_ref, pos_ref, w1_ref, w2_ref, o_ref):
    r = r_ref[0]
    rows = r.shape[0]
    rowid = lax.broadcasted_iota(jnp.int32, (rows, 1), 0)
    nxt = jnp.where(rowid == rows - 1, n_ref[0, 0:1, :], pltpu.roll(r, rows - 1, 0))
    first = (r + pos_ref[0:1, :]).astype(BF16)
    second = (nxt + pos_ref[1:2, :]).astype(BF16)
    hid = (jnp.dot(first, w1_ref[0], preferred_element_type=F32)
           + jnp.dot(second, w1_ref[1], preferred_element_type=F32))
    hid = jax.nn.gelu(hid).astype(BF16)
    o_ref[0] = jnp.dot(hid, w2_ref[...], preferred_element_type=F32).astype(o_ref.dtype)


def compress(kc_rows, pos2, w1e, w2e, layer):
    b, nc, w = kc_rows.shape
    tr = 128
    sub = 8
    of_layer = lambda z: pl.BlockSpec((None,) + z.shape[1:], lambda i, j: (layer,) + (0,) * (z.ndim - 1))
    nxt_blk = lambda i, j: (i, jnp.minimum((j + 1) * (tr // sub), nc // sub - 1), 0)
    return pl.pallas_call(
        _compress_kernel,
        grid=(b, nc // tr),
        in_specs=[pl.BlockSpec((1, tr, w), lambda i, j: (i, j, 0)),
                  pl.BlockSpec((1, sub, w), nxt_blk),
                  of_layer(pos2), of_layer(w1e), of_layer(w2e)],
        out_specs=pl.BlockSpec((1, tr, N_KV * HEAD_DIM), lambda i, j: (i, j, 0)),
        out_shape=jax.ShapeDtypeStruct((b, nc, N_KV * HEAD_DIM), BF16),
        compiler_params=_cparams(("parallel", "parallel")),
        name="compress",
    )(kc_rows, kc_rows, pos2, w1e, w2e)


def _compress_weights(pos, w1, w2):
    depth = pos.shape[0]
    eye = jnp.eye(N_KV, dtype=F32)
    w1 = w1.reshape(depth, 2, CMP_STRIDE, 1, HEAD_DIM, 1, CMP_HIDDEN)
    w1e = (w1 * eye[None, None, None, :, None, :, None]).reshape(
        depth, 2, CMP_STRIDE * N_KV * HEAD_DIM, N_KV * CMP_HIDDEN).astype(BF16)
    w2e = (w2[:, None, :, None, :] * eye[None, :, None, :, None]).reshape(
        depth, N_KV * CMP_HIDDEN, N_KV * HEAD_DIM).astype(BF16)
    pos2 = jnp.broadcast_to(pos.reshape(depth, 2, CMP_STRIDE, 1, HEAD_DIM),
                            (depth, 2, CMP_STRIDE, N_KV, HEAD_DIM)).reshape(depth, 2, -1)
    return pos2, w1e, w2e


def _kv_prep_kernel(ks_ref, vs_ref, kw_ref, vw_ref, ka_ref, vst_ref, kwp_ref, vwtp_ref):
    j = pl.program_id(1)
    ts = ks_ref.shape[0]
    n_sel = ka_ref.shape[3] - HEAD_DIM
    blk0 = jnp.maximum(j - 1, 0) * (ts // SEL_BLOCK)
    row_blk = blk0 + lax.broadcasted_iota(jnp.int32, (ts, n_sel), 0) // SEL_BLOCK
    col = lax.broadcasted_iota(jnp.int32, (ts, n_sel), 1)
    onehot = jnp.where(row_blk == col, ONEHOT_BIG, 0.0).astype(BF16)
    ks = ks_ref[...]
    vs_t = vs_ref[...].T
    pad_rows = lax.broadcasted_iota(jnp.int32, (V_ROWS - HEAD_DIM, ts), 0)
    extra = jnp.where(pad_rows == 0, 1.0, 0.0).astype(BF16)
    for g in range(N_KV):
        lo, hi = g * HEAD_DIM, (g + 1) * HEAD_DIM
        ka_ref[0, g, :, 0:n_sel] = onehot
        ka_ref[0, g, :, n_sel:] = ks[:, lo:hi].astype(BF16)
        vst_ref[0, g, 0:HEAD_DIM, :] = vs_t[lo:hi].astype(BF16)
        vst_ref[0, g, HEAD_DIM:, :] = extra

    @pl.when(j == 0)
    def _():
        kwp_ref[...] = jnp.zeros_like(kwp_ref)
        vwtp_ref[...] = jnp.zeros_like(vwtp_ref)

    @pl.when(j > 0)
    def _():
        kw = kw_ref[...]
        vw_t = vw_ref[...].T
        for g in range(N_KV):
            lo, hi = g * HEAD_DIM, (g + 1) * HEAD_DIM
            kwp_ref[0, g] = kw[:, lo:hi].astype(BF16)
            vwtp_ref[0, g] = vw_t[lo:hi].astype(BF16)


def kv_prep(proj, n_batch, seq):
    ts = WINDOW
    nblk = seq // ts
    n_sel = seq // SEL_BLOCK
    kvw = N_KV * HEAD_DIM
    src = lambda col: pl.BlockSpec((ts, kvw), lambda b, j: (b * nblk + jnp.maximum(j - 1, 0), col // kvw))
    same = lambda b, j: (b, 0, jnp.maximum(j - 1, 0), 0)
    same_t = lambda b, j: (b, 0, 0, jnp.maximum(j - 1, 0))
    return pl.pallas_call(
        _kv_prep_kernel,
        grid=(n_batch, nblk + 1),
        in_specs=[src(COL_KS), src(COL_VS), src(COL_KW), src(COL_VW)],
        out_specs=[pl.BlockSpec((1, N_KV, ts, n_sel + HEAD_DIM), same),
                   pl.BlockSpec((1, N_KV, V_ROWS, ts), same_t),
                   pl.BlockSpec((1, N_KV, ts, HEAD_DIM), lambda b, j: (b, 0, j, 0)),
                   pl.BlockSpec((1, N_KV, HEAD_DIM, ts), lambda b, j: (b, 0, 0, j))],
        out_shape=[jax.ShapeDtypeStruct((n_batch, N_KV, seq, n_sel + HEAD_DIM), BF16),
                   jax.ShapeDtypeStruct((n_batch, N_KV, V_ROWS, seq), BF16),
                   jax.ShapeDtypeStruct((n_batch, N_KV, seq + WINDOW, HEAD_DIM), BF16),
                   jax.ShapeDtypeStruct((n_batch, N_KV, HEAD_DIM, seq + WINDOW), BF16)],
        compiler_params=_cparams(("parallel", "arbitrary")),
        name="kv_prep",
    )(proj, proj, proj, proj)


def _softmax_cols(s):
    m = jnp.max(s, axis=0, keepdims=True)
    m = jnp.where(m < 0.5 * NEG, 0.0, m)
    e = jnp.exp2(s - m)
    return e * (1.0 / jnp.maximum(jnp.sum(e, axis=0, keepdims=True), 1e-30))


def _cmp_branch(a, qt, gate, k_ref, vt_ref, tbl_ref, smat_ref, o_ref, st_ref, rank_ref, other_branch):
    n_row_tiles = k_ref.shape[2] // CMP_LANE
    ap = a // (FAR_DIST // Q_TILE)

    def attend(n):
        rows = n * CMP_LANE
        s = jnp.dot(k_ref[0, 0, 0:rows, :], qt, preferred_element_type=F32)
        tbl = tbl_ref[0, 0]
        pieces = [s[c * CMP_LANE:(c + 1) * CMP_LANE] for c in range(n)]
        pieces[-1] = pieces[-1] + tbl[CMP_LANE:]
        if n > 1:
            pieces[-2] = pieces[-2] + tbl[:CMP_LANE]
        other = other_branch()
        p = _softmax_cols(jnp.concatenate(pieces, axis=0))
        o_ref[...] = other + gate * jnp.dot(vt_ref[0, 0, :, 0:rows], p.astype(BF16), preferred_element_type=F32)
        imp = (p[:, 0:Q_TILE] + p[:, Q_TILE:2 * Q_TILE] + p[:, 2 * Q_TILE:3 * Q_TILE]
               + p[:, 3 * Q_TILE:4 * Q_TILE])
        hi = imp.astype(BF16)
        r1 = imp - hi.astype(F32)
        mid = r1.astype(BF16)
        lo = (r1 - mid.astype(F32)).astype(BF16)
        smat = smat_ref[:, 0:rows]
        st_ref[...] = (jnp.dot(smat, hi, preferred_element_type=F32)
                       + jnp.dot(smat, mid, preferred_element_type=F32)
                       + jnp.dot(smat, lo, preferred_element_type=F32))

    for n in range(1, n_row_tiles + 1):
        pl.when(ap == n - 1)(lambda n=n: attend(n))

    o = o_ref[...]
    st = st_ref[...]
    t = a * Q_TILE + lax.broadcasted_iota(jnp.int32, st.shape, 1)
    jj = lax.broadcasted_iota(jnp.int32, st.shape, 0)
    cur_blk = t // SEL_BLOCK
    valid = jj * SEL_BLOCK <= t
    forced = (jj == 0) | (jj == cur_blk) | (jj == cur_blk - 1)
    st = jnp.where(valid, jnp.where(forced, FORCED_SCORE, st), -1.0)
    st_ref[...] = st
    rank_ref[...] = jnp.zeros_like(rank_ref)
    sub = 8
    chunk = 16
    n_sel = st.shape[0]
    last_valid = (a * Q_TILE + Q_TILE - 1) // SEL_BLOCK
    for q0 in range(0, n_sel, chunk):
        @pl.when(q0 <= last_valid)
        def _(q0=q0):
            for v0 in range(0, n_sel, chunk):
                @pl.when(v0 <= last_valid)
                def _(v0=v0):
                    for v in range(v0 // sub, min(v0 + chunk, n_sel) // sub):
                        sv = st_ref[v * sub:(v + 1) * sub, :]
                        jv = jj[v * sub:(v + 1) * sub]
                        cnt = rank_ref[v * sub:(v + 1) * sub, :]
                        for jp in range(q0, min(q0 + chunk, n_sel)):
                            row = st_ref[jp:jp + 1, :]
                            if jp < v * sub:
                                ahead = row >= sv
                            elif jp >= (v + 1) * sub:
                                ahead = row > sv
                            else:
                                ahead = (row > sv) | ((row == sv) & (jv > jp))
                            cnt = cnt + jnp.where(ahead, 1.0, 0.0)
                        rank_ref[v * sub:(v + 1) * sub, :] = cnt

    keep = (rank_ref[...] < SEL_TOP) & (st >= 0.0)
    return o, (keep.astype(F32) - 1.0).astype(BF16)


def _sel_branch(a, gate, ka_ref, vt_ref, tbl_ref, qa_ref, m_ref, acc_ref):
    tk = SEL_KEY_TILE
    acc_ref[...] = jnp.zeros_like(acc_ref)
    n_tiles = a // (tk // Q_TILE) + 1
    far_pairs = jnp.maximum(0, (a - (SEL_NEAR_TILES - 5)) // 4)
    far_octs = far_pairs // 4
    far_quads = (far_pairs - 4 * far_octs) // 2
    first_near = 2 * far_pairs
    near_quads = (n_tiles - first_near) // 4
    near_pairs = (n_tiles - first_near - 4 * near_quads + 1) // 2

    def key_off(c):
        return pl.multiple_of(jnp.minimum(c, n_tiles - 1) * tk, tk)

    def attend(c0, n, biased):
        m_old = m_ref[...]
        acc_old = acc_ref[...]

        whole = n >= 4
        span = pl.ds(pl.multiple_of(c0 * tk, tk), n * tk)

        def scores():
            if whole:
                s_all = jnp.dot(ka_ref[0, 0, span, :], qa_ref[...], preferred_element_type=F32)
                ss = [s_all[h * tk:(h + 1) * tk] for h in range(n)]
            else:
                ss = [jnp.dot(ka_ref[0, 0, pl.ds(key_off(c0 + h), tk), :], qa_ref[...],
                              preferred_element_type=F32) for h in range(n)]
            if biased:
                ss = [s + tbl_ref[0, jnp.where(c0 + h < n_tiles, a - 2 * (c0 + h), SEL_NEAR_TILES - 1)]
                      for h, s in enumerate(ss)]
            return ss

        def weighted(ss, m):
            ps = [jnp.exp2(s - m).astype(BF16) for s in ss]
            if whole:
                return jnp.dot(vt_ref[0, 0, :, span], jnp.concatenate(ps, axis=0), preferred_element_type=F32)
            return sum(jnp.dot(vt_ref[0, 0, :, pl.ds(key_off(c0 + h), tk)], p, preferred_element_type=F32)
                       for h, p in enumerate(ps))

        ss = scores()
        m_new = m_old
        for s in ss:
            m_new = jnp.maximum(m_new, jnp.max(s, axis=0, keepdims=True))
        acc_ref[...] = (acc_old + weighted(ss, m_old)) * jnp.exp2(m_old - m_new)
        m_ref[...] = m_new

        @pl.when(jnp.max(m_new - m_old) > SAFE_LOG2_RANGE)
        def _():
            acc_ref[...] = jnp.exp2(m_old - m_new) * acc_old + weighted(scores(), m_new)

    def body(first, n, biased):
        def step(i, carry):
            attend(first + n * i, n, biased)
            return carry
        return step

    sub = 8
    m_ref[...] = jnp.dot(ka_ref[0, 0, 0:sub, :], qa_ref[...], preferred_element_type=F32)[0:1, :]
    lax.fori_loop(0, far_octs, body(0, 8, False), 0)
    lax.fori_loop(0, far_quads, body(8 * far_octs, 4, False), 0)
    lax.fori_loop(0, far_pairs - 4 * far_octs - 2 * far_quads, body(8 * far_octs + 4 * far_quads, 2, False), 0)
    lax.fori_loop(0, near_quads, body(first_near, 4, True), 0)
    lax.fori_loop(0, near_pairs, body(first_near + 4 * near_quads, 2, True), 0)
    acc = acc_ref[...]
    return acc[0:HEAD_DIM] * (gate / acc[HEAD_DIM:HEAD_DIM + 1])


def _win_branch(a, qt, gate, k_ref, vt_ref, tbl_ref):
    span = WINDOW + Q_TILE
    off = pl.multiple_of(a * Q_TILE, Q_TILE)
    s = jnp.dot(k_ref[0, 0, pl.ds(off, span), :], qt, preferred_element_type=F32)
    s = s + tbl_ref[0, 0]
    e = jnp.exp2(s - jnp.max(s, axis=0, keepdims=True))
    ev = jnp.dot(vt_ref[0, 0, :, pl.ds(off, span)], e.astype(BF16), preferred_element_type=F32)
    return ev * (gate / jnp.sum(e, axis=0, keepdims=True))


def _nsa_kernel(q_ref, g_ref, kc_ref, vtc_ref, tblc_ref, smat_ref, ka_ref, vst_ref, tbls_ref,
                kw_ref, vwt_ref, tblw_ref, o_ref, gt_ref, oc_ref, st_ref, rank_ref, qa_ref, m_ref, acc_ref):
    g = pl.program_id(0)
    a = pl.program_id(2)
    n_sel = smat_ref.shape[0]
    half = 2 * HEAD_DIM
    q = q_ref[...] * (HEAD_DIM ** -0.5) * LOG2E
    t0 = q[:, :half].T
    t1 = q[:, half:].T
    qt = jnp.concatenate([t0[:HEAD_DIM], t0[HEAD_DIM:], t1[:HEAD_DIM], t1[HEAD_DIM:]],
                         axis=1).astype(BF16)
    gt_ref[...] = g_ref[:, :Q_TILE].T

    def gate(branch):
        rows = [gt_ref[pl.ds((g * GQA_R + r) * 3 + branch, 1), :] for r in range(GQA_R)]
        return jax.nn.sigmoid(jnp.concatenate(rows, axis=1))

    o, sel = _cmp_branch(a, qt, gate(0), kc_ref, vtc_ref, tblc_ref, smat_ref, oc_ref, st_ref, rank_ref,
                         functools.partial(_win_branch, a, qt, gate(2), kw_ref, vwt_ref, tblw_ref))
    for r in range(GQA_R):
        qa_ref[0:n_sel, r * Q_TILE:(r + 1) * Q_TILE] = sel
    qa_ref[n_sel:, :] = qt
    o = o + _sel_branch(a, gate(1), ka_ref, vst_ref, tbls_ref, qa_ref, m_ref, acc_ref)
    u0 = jnp.concatenate([o[:, 0:Q_TILE], o[:, Q_TILE:2 * Q_TILE]], axis=0).T
    u1 = jnp.concatenate([o[:, 2 * Q_TILE:3 * Q_TILE], o[:, 3 * Q_TILE:]], axis=0).T
    o_ref[...] = jnp.concatenate([u0, u1], axis=1).astype(o_ref.dtype)


def nsa_attention(proj, n_batch, seq, k_cmp, vt_cmp, ka, vst, kwp, vwtp, tables, smat):
    tbl_c, tbl_s, tbl_w = tables
    n_a = seq // Q_TILE
    nc = k_cmp.shape[2]
    n_sel = smat.shape[0]
    dh = HEAD_DIM
    gw = GQA_R * HEAD_DIM
    tps = FAR_DIST // Q_TILE
    per_bg = lambda gi, bi, ai: (bi, gi, 0, 0)
    return pl.pallas_call(
        _nsa_kernel,
        grid=(N_KV, n_batch, n_a),
        in_specs=[pl.BlockSpec((Q_TILE, gw), lambda gi, bi, ai: (bi * n_a + ai, COL_Q // gw + gi)),
                  pl.BlockSpec((Q_TILE, gw), lambda gi, bi, ai: (bi * n_a + ai, COL_GATE // gw)),
                  pl.BlockSpec((1, 1, nc, dh), per_bg),
                  pl.BlockSpec((1, 1, dh, nc), per_bg),
                  pl.BlockSpec((1, 1, 2 * CMP_LANE, ROWS), lambda gi, bi, ai: (gi, ai % tps, 0, 0)),
                  pl.BlockSpec(smat.shape, lambda gi, bi, ai: (0, 0)),
                  pl.BlockSpec((1, 1, seq, n_sel + dh), per_bg),
                  pl.BlockSpec((1, 1, V_ROWS, seq), per_bg),
                  pl.BlockSpec((1, SEL_NEAR_TILES, SEL_KEY_TILE, ROWS), lambda gi, bi, ai: (gi, 0, 0, 0)),
                  pl.BlockSpec((1, 1, seq + WINDOW, dh), per_bg),
                  pl.BlockSpec((1, 1, dh, seq + WINDOW), per_bg),
                  pl.BlockSpec((1, 1, WINDOW + Q_TILE, ROWS),
                               lambda gi, bi, ai: (gi, jnp.minimum(ai, WINDOW // Q_TILE), 0, 0))],
        out_specs=pl.BlockSpec((Q_TILE, gw), lambda gi, bi, ai: (bi * n_a + ai, gi)),
        out_shape=jax.ShapeDtypeStruct((n_batch * seq, N_HEADS * HEAD_DIM), BF16),
        scratch_shapes=[pltpu.VMEM((Q_TILE, Q_TILE), F32),
                        pltpu.VMEM((dh, ROWS), F32),
                        pltpu.VMEM((n_sel, Q_TILE), F32), pltpu.VMEM((n_sel, Q_TILE), F32),
                        pltpu.VMEM((n_sel + dh, ROWS), BF16), pltpu.VMEM((1, ROWS), F32),
                        pltpu.VMEM((V_ROWS, ROWS), F32)],
        compiler_params=_cparams(("parallel", "parallel", "arbitrary")),
        name="nsa_attention",
    )(proj, proj, k_cmp, vt_cmp, tbl_c, smat, ka, vst, tbl_s, kwp, vwtp, tbl_w)


def _rel_bucket(dist):
    n = jnp.maximum(dist, 0)
    nf = jnp.maximum(n, 1).astype(F32)
    large = RPB_MAX_EXACT + (jnp.log(nf / RPB_MAX_EXACT) / math.log(RPB_MAX_DIST / RPB_MAX_EXACT)
                             * (N_BUCKETS - RPB_MAX_EXACT)).astype(jnp.int32)
    large = jnp.minimum(large, N_BUCKETS - 1)
    return jnp.where(n < RPB_MAX_EXACT, n, large)


def _shifted_rows(e, n_rows, stride):
    return pltpu.roll(jnp.broadcast_to(e, (n_rows, e.shape[1])), 0, 1, stride=stride, stride_axis=0)


def _bias_tables_kernel(ec_ref, es_ref, ew_ref, tc_ref, ts_ref, tw_ref):
    n_m = tc_ref.shape[1]
    n_am = tc_ref.shape[0]
    mc = _shifted_rows(ec_ref[...], n_m, CMP_STRIDE)
    first = mc.shape[1] - n_am * Q_TILE
    for am in range(n_am):
        tc_ref[am] = mc[:, first + am * Q_TILE:first + (am + 1) * Q_TILE]
    tk = ts_ref.shape[1]
    for k in range(ts_ref.shape[0]):
        ts_ref[k] = _shifted_rows(es_ref[k:k + 1, :], tk, 1)[:, tk:tk + Q_TILE]
    span = tw_ref.shape[1]
    mw = _shifted_rows(ew_ref[...], span, 1)[:, span:span + Q_TILE]
    kj = lax.broadcasted_iota(jnp.int32, mw.shape, 0)
    for v in range(tw_ref.shape[0]):
        tw_ref[v] = jnp.where(kj >= WINDOW - v * Q_TILE, mw, NEG)


def _bias_tables(rpb_table):
    tab = rpb_table.astype(F32)
    by_dist = tab[_rel_bucket(jnp.arange(FAR_DIST))].T * LOG2E
    rel = by_dist - (tab[N_BUCKETS - 1] * LOG2E)[:, None]
    neg = lambda n: jnp.full((N_HEADS, n), NEG, F32)
    zero = lambda n: jnp.zeros((N_HEADS, n), F32)
    tk, span, n_m, n_am = SEL_KEY_TILE, WINDOW + Q_TILE, 2 * CMP_LANE, FAR_DIST // Q_TILE

    lead = CMP_STRIDE * n_m - (FAR_DIST - CMP_BLOCK + 1)
    e_c = jnp.concatenate([neg(lead), rel, zero(CMP_STRIDE * n_m - lead)], axis=1)[:, None, :]
    ext = jnp.concatenate([neg(tk), rel], axis=1)
    e_s = jnp.stack([ext[:, Q_TILE * k:Q_TILE * k + tk + Q_TILE] for k in range(SEL_NEAR_TILES - 1)]
                    + [neg(tk + Q_TILE)], axis=1)
    e_w = jnp.concatenate([neg(Q_TILE), by_dist[:, :WINDOW], neg(Q_TILE)], axis=1)[:, None, :]

    n_var = WINDOW // Q_TILE + 1
    head = lambda h: (h // GQA_R, 0, 0, h % GQA_R)
    return pl.pallas_call(
        _bias_tables_kernel,
        grid=(N_HEADS,),
        in_specs=[pl.BlockSpec((None,) + e_c.shape[1:], lambda h: (h, 0, 0)),
                  pl.BlockSpec((None,) + e_s.shape[1:], lambda h: (h, 0, 0)),
                  pl.BlockSpec((None,) + e_w.shape[1:], lambda h: (h, 0, 0))],
        out_specs=[pl.BlockSpec((None, n_am, n_m, Q_TILE), head),
                   pl.BlockSpec((None, SEL_NEAR_TILES, tk, Q_TILE), head),
                   pl.BlockSpec((None, n_var, span, Q_TILE), head)],
        out_shape=[jax.ShapeDtypeStruct((N_KV, n_am, n_m, ROWS), F32),
                   jax.ShapeDtypeStruct((N_KV, SEL_NEAR_TILES, tk, ROWS), F32),
                   jax.ShapeDtypeStruct((N_KV, n_var, span, ROWS), F32)],
        compiler_params=_cparams(("parallel",)),
        name="bias_tables",
    )(e_c, e_s, e_w)


def _slc_matrix(n_cmp_pad, n_sel):
    sel_ratio, cmp_ratio = SEL_BLOCK // CMP_STRIDE, CMP_BLOCK // CMP_STRIDE
    mat = np.zeros((n_sel, n_cmp_pad), np.float32)
    for j in range(n_sel):
        for mm in range(sel_ratio):
            for nn in range(cmp_ratio):
                idx = sel_ratio * j - mm - nn
                if 0 <= idx < n_cmp_pad - 1:
                    mat[j, idx] += 1.0
    return jnp.asarray(mat, BF16)


def nsa_mixer(proj, n_batch, seq, cmp_k_w, cmp_v_w, tables, smat, layer):
    kv = N_KV * HEAD_DIM
    rows16 = lambda col: proj[:, col:col + kv].reshape(n_batch, seq // CMP_STRIDE, CMP_STRIDE * kv)
    k_cmp = compress(rows16(COL_KC), *cmp_k_w, layer)
    v_cmp = compress(rows16(COL_VC), *cmp_v_w, layer)
    k_cmp = k_cmp.reshape(n_batch, -1, N_KV, HEAD_DIM).transpose(0, 2, 1, 3)
    vt_cmp = v_cmp.reshape(n_batch, -1, N_KV, HEAD_DIM).transpose(0, 2, 3, 1)
    ka, vst, kwp, vwtp = kv_prep(proj, n_batch, seq)
    return nsa_attention(proj, n_batch, seq, k_cmp, vt_cmp, ka, vst, kwp, vwtp, tables, smat)


def kernel(x, rpb_table, attn_norm, ffn_norm, final_norm, w_in, ssm_a_re, ssm_a_im, ssm_log_dt,
           ssm_b_re, ssm_b_im, ssm_c_re, ssm_c_im, ssm_d, w_glu, b_glu, cmp_pos_k, cmp_w1_k, cmp_w2_k,
           cmp_pos_v, cmp_w1_v, cmp_w2_v, w_out, w_ffn_gate, w_ffn_up, w_ffn_down):
    n_batch, seq, d_model = x.shape
    depth = w_in.shape[0]
    tables = _bias_tables(rpb_table)
    smat = _slc_matrix(seq // CMP_STRIDE, seq // SEL_BLOCK)

    w_in_b = jnp.pad(w_in, ((0, 0), (0, 0), (0, IN_COLS_PAD - IN_COLS))).astype(BF16)
    w_glu_b, w_out_b = w_glu.astype(BF16), w_out.astype(BF16)
    w_gate_b, w_up_b, w_down_b = w_ffn_gate.astype(BF16), w_ffn_up.astype(BF16), w_ffn_down.astype(BF16)

    s5w = _s5_weights(ssm_a_re, ssm_a_im, ssm_log_dt, ssm_b_re, ssm_b_im, ssm_c_re, ssm_c_im)
    cmp_k_w = _compress_weights(cmp_pos_k, cmp_w1_k, cmp_w2_k)
    cmp_v_w = _compress_weights(cmp_pos_v, cmp_w1_v, cmp_w2_v)

    xf = x.reshape(n_batch * seq, d_model)
    for l in range(depth):
        proj = norm_matmul(xf, attn_norm[l].reshape(1, -1), w_in_b, l)
        y_ssm = s5_glu(proj, n_batch, *s5w, ssm_d[l].reshape(1, -1), w_glu_b, b_glu[l].reshape(1, -1), l)
        y_nsa = nsa_mixer(proj, n_batch, seq, cmp_k_w, cmp_v_w, tables, smat, l)
        xf = out_proj(xf, y_ssm, y_nsa, w_out_b, l)
        xf = ffn(xf, ffn_norm[l].reshape(1, -1), w_gate_b, w_up_b, w_down_b, l, final_norm.reshape(1, -1),
                 norm_out=l == depth - 1)
    return xf.reshape(n_batch, seq, d_model)
```

```python
import functools
import math

import jax
import jax.numpy as jnp
import numpy as np
from jax import lax
from jax.experimental import pallas as pl
from jax.experimental.pallas import tpu as pltpu

F32 = jnp.float32
BF16 = jnp.bfloat16

D_MODEL = 2048
DEPTH = 4
D_SSM = 1024
SSM_GROUP = 16
N_SSM_GROUPS = 64
SSM_STATE = 64
N_STATE = N_SSM_GROUPS * SSM_STATE
SSM_PACK = 16
N_PACKS = N_SSM_GROUPS // SSM_PACK
N_HEADS = 16
N_KV = 4
GQA_R = 4
HEAD_DIM = 64
CMP_BLOCK = 32
CMP_STRIDE = 16
CMP_HIDDEN = 128
SEL_BLOCK = 64
SEL_TOP = 16
WINDOW = 512
Q_TILE = 128
ROWS = GQA_R * Q_TILE
FORCED_SCORE = 1e4
N_BUCKETS = 32
RPB_MAX_EXACT = 16
RPB_MAX_DIST = 1024
D_FF = 5632
NORM_EPS = 1e-6
IN_COLS = 3632
IN_COLS_PAD = 3840
COL_U, COL_Q, COL_KC, COL_VC, COL_KS, COL_VS, COL_KW, COL_VW, COL_GATE = (
    0, 1024, 2048, 2304, 2560, 2816, 3072, 3328, 3584)
NEG = -1e30
ONEHOT_BIG = 1e30
FAR_DIST = 2048
SEL_KEY_TILE = 256
SEL_NEAR_TILES = 12
LOG2E = math.log2(math.e)
SAFE_LOG2_RANGE = 64.0
CMP_LANE = 128
V_ROWS = 80
VMEM_LIMIT = 56 * 1024 * 1024


def _cparams(sem):
    return pltpu.CompilerParams(dimension_semantics=sem, vmem_limit_bytes=VMEM_LIMIT)


def _rms(x, g):
    ms = jnp.mean(x * x, axis=-1, keepdims=True)
    return x * lax.rsqrt(ms + NORM_EPS) * g


def _norm_matmul_kernel(x_ref, g_ref, w_ref, o_ref, h_ref):
    @pl.when(pl.program_id(1) == 0)
    def _():
        h_ref[...] = _rms(x_ref[...], g_ref[...]).astype(BF16)

    o_ref[...] = jnp.dot(h_ref[...], w_ref[...], preferred_element_type=F32)


def norm_matmul(x, g, w, layer, *, tm=1024, tn=768):
    m, k = x.shape
    n = w.shape[2]
    return pl.pallas_call(
        _norm_matmul_kernel,
        grid=(m // tm, n // tn),
        in_specs=[pl.BlockSpec((tm, k), lambda i, j: (i, 0)),
                  pl.BlockSpec((1, k), lambda i, j: (0, 0)),
                  pl.BlockSpec((None, k, tn), lambda i, j: (layer, 0, j))],
        out_specs=pl.BlockSpec((tm, tn), lambda i, j: (i, j)),
        out_shape=jax.ShapeDtypeStruct((m, n), F32),
        scratch_shapes=[pltpu.VMEM((tm, k), BF16)],
        compiler_params=_cparams(("parallel", "arbitrary")),
        name="norm_matmul",
    )(x, g, w)


def _out_proj_kernel(x_ref, ys_ref, yn_ref, w1_ref, w2_ref, o_ref):
    o_ref[...] = (x_ref[...]
                  + jnp.dot(ys_ref[...], w1_ref[...], preferred_element_type=F32)
                  + jnp.dot(yn_ref[...], w2_ref[...], preferred_element_type=F32))


def out_proj(x, y_ssm, y_nsa, w_out, layer, *, tm=512):
    m, d = x.shape
    k = y_ssm.shape[1]
    row = lambda i: (i, 0)
    return pl.pallas_call(
        _out_proj_kernel,
        grid=(m // tm,),
        in_specs=[pl.BlockSpec((tm, d), row), pl.BlockSpec((tm, k), row), pl.BlockSpec((tm, k), row),
                  pl.BlockSpec((None, k, d), lambda i: (layer, 0, 0)),
                  pl.BlockSpec((None, k, d), lambda i: (layer, 1, 0))],
        out_specs=pl.BlockSpec((tm, d), row),
        out_shape=jax.ShapeDtypeStruct((m, d), F32),
        compiler_params=_cparams(("parallel",)),
        name="out_proj",
    )(x, y_ssm, y_nsa, w_out, w_out)


def _ffn_kernel(x_ref, g_ref, wg_ref, wu_ref, wd_ref, g_out_ref, o_ref, h_ref, *, norm_out):
    f = pl.program_id(1)

    @pl.when(f == 0)
    def _():
        x = x_ref[...]
        h_ref[...] = _rms(x, g_ref[...]).astype(BF16)
        o_ref[...] = x

    h = h_ref[...]
    gate = jnp.dot(h, wg_ref[...], preferred_element_type=F32)
    up = jnp.dot(h, wu_ref[...], preferred_element_type=F32)
    act = (jax.nn.silu(gate) * up).astype(BF16)
    o_ref[...] += jnp.dot(act, wd_ref[...], preferred_element_type=F32)

    if norm_out:
        @pl.when(f == pl.num_programs(1) - 1)
        def _():
            o_ref[...] = _rms(o_ref[...], g_out_ref[...])


def ffn(x, g, w_gate, w_up, w_down, layer, g_out, *, norm_out, tm=1024, tf=256):
    m, d = x.shape
    dff = w_gate.shape[2]
    return pl.pallas_call(
        functools.partial(_ffn_kernel, norm_out=norm_out),
        grid=(m // tm, dff // tf),
        in_specs=[pl.BlockSpec((tm, d), lambda i, f: (i, 0)),
                  pl.BlockSpec((1, d), lambda i, f: (0, 0)),
                  pl.BlockSpec((None, d, tf), lambda i, f: (layer, 0, f)),
                  pl.BlockSpec((None, d, tf), lambda i, f: (layer, 0, f)),
                  pl.BlockSpec((None, tf, d), lambda i, f: (layer, f, 0)),
                  pl.BlockSpec((1, d), lambda i, f: (0, 0))],
        out_specs=pl.BlockSpec((tm, d), lambda i, f: (i, 0)),
        out_shape=jax.ShapeDtypeStruct((m, d), F32),
        scratch_shapes=[pltpu.VMEM((tm, d), BF16)],
        compiler_params=_cparams(("parallel", "arbitrary")),
        name="ffn",
    )(x, g, w_gate, w_up, w_down, g_out)


def _s5_discretize_kernel(are_ref, aim_ref, ldt_ref, bre_ref, bim_ref,
                          abre_ref, abim_ref, bbre_ref, bbim_ref):
    lam_re = jnp.minimum(are_ref[...], -1e-4)
    lam_im = aim_ref[...]
    dt = jnp.exp(ldt_ref[...])
    mag = jnp.exp(lam_re * dt)
    ab_re = mag * jnp.cos(lam_im * dt)
    ab_im = mag * jnp.sin(lam_im * dt)
    nr, ni = ab_re - 1.0, ab_im
    den = lam_re * lam_re + lam_im * lam_im
    f_re = (nr * lam_re + ni * lam_im) / den
    f_im = (ni * lam_re - nr * lam_im) / den
    br, bi = bre_ref[...], bim_ref[...]
    abre_ref[...] = ab_re
    abim_ref[...] = ab_im
    bbre_ref[...] = f_re * br - f_im * bi
    bbim_ref[...] = f_re * bi + f_im * br


def s5_discretize(a_re, a_im, log_dt, b_re_t, b_im_t):
    g, p = a_re.shape
    h = b_re_t.shape[1]
    outs = pl.pallas_call(
        _s5_discretize_kernel,
        out_shape=[jax.ShapeDtypeStruct((g, 1, p), F32), jax.ShapeDtypeStruct((g, 1, p), F32),
                   jax.ShapeDtypeStruct((g, h, p), F32), jax.ShapeDtypeStruct((g, h, p), F32)],
        name="s5_discretize",
    )(a_re.reshape(g, 1, p), a_im.reshape(g, 1, p), log_dt.reshape(g, 1, 1), b_re_t, b_im_t)
    return outs


def _s5_kernel(u_ref, bre_ref, bim_ref, cre_ref, cimn_ref, a_ref, d_ref, wglu_ref, bglu_ref,
               o_ref, xr_ref, xi_ref, st_ref):
    tc = u_ref.shape[0]
    pw = SSM_PACK * SSM_GROUP
    sw = SSM_PACK * SSM_STATE

    @pl.when(pl.program_id(1) == 0)
    def _():
        st_ref[...] = jnp.zeros_like(st_ref)

    u = u_ref[...]
    ub = u.astype(BF16)
    for k in range(N_PACKS):
        uk = ub[:, k * pw:(k + 1) * pw]
        xr_ref[:, k * sw:(k + 1) * sw] = jnp.dot(uk, bre_ref[k], preferred_element_type=F32)
        xi_ref[:, k * sw:(k + 1) * sw] = jnp.dot(uk, bim_ref[k], preferred_element_type=F32)

    a_re = a_ref[0:1, :]
    a_im = a_ref[1:2, :]

    def step(t, state):
        s_re, s_im = state
        n_re = a_re * s_re - a_im * s_im + xr_ref[pl.ds(t, 1), :]
        n_im = a_re * s_im + a_im * s_re + xi_ref[pl.ds(t, 1), :]
        xr_ref[pl.ds(t, 1), :] = n_re
        xi_ref[pl.ds(t, 1), :] = n_im
        return n_re, n_im

    s_re, s_im = lax.fori_loop(0, tc, step, (st_ref[0:1, :], st_ref[1:2, :]), unroll=4)
    st_ref[0:1, :] = s_re
    st_ref[1:2, :] = s_im

    ys = []
    for k in range(N_PACKS):
        xr = xr_ref[:, k * sw:(k + 1) * sw].astype(BF16)
        xi = xi_ref[:, k * sw:(k + 1) * sw].astype(BF16)
        ys.append(jnp.dot(xr, cre_ref[k], preferred_element_type=F32)
                  + jnp.dot(xi, cimn_ref[k], preferred_element_type=F32))
    y = jnp.concatenate(ys, axis=-1) + d_ref[...] * u
    y = jax.nn.gelu(y).astype(BF16)
    ab = jnp.dot(y, wglu_ref[...], preferred_element_type=F32) + bglu_ref[...]
    o_ref[...] = (ab[:, :D_SSM] * jax.nn.sigmoid(ab[:, D_SSM:])).astype(o_ref.dtype)


def s5_glu(proj, n_batch, bbd_re, bbd_im, cbd_re, cbd_imn, a_flat, d, w_glu, b_glu, layer, *, tc=256):
    t = proj.shape[0]
    n_chunks = t // n_batch // tc
    c2 = lambda b, c: (0, 0)
    of_layer = lambda w: pl.BlockSpec((None,) + w.shape[1:], lambda b, c: (layer,) + (0,) * (w.ndim - 1))
    return pl.pallas_call(
        _s5_kernel,
        grid=(n_batch, n_chunks),
        in_specs=[pl.BlockSpec((tc, D_SSM), lambda b, c: (b * n_chunks + c, 0)),
                  of_layer(bbd_re), of_layer(bbd_im), of_layer(cbd_re), of_layer(cbd_imn),
                  of_layer(a_flat), pl.BlockSpec(d.shape, c2), of_layer(w_glu),
                  pl.BlockSpec(b_glu.shape, c2)],
        out_specs=pl.BlockSpec((tc, D_SSM), lambda b, c: (b * n_chunks + c, 0)),
        out_shape=jax.ShapeDtypeStruct((t, D_SSM), BF16),
        scratch_shapes=[pltpu.VMEM((tc, N_STATE), F32), pltpu.VMEM((tc, N_STATE), F32),
                        pltpu.VMEM((2, N_STATE), F32)],
        compiler_params=_cparams(("arbitrary", "arbitrary")),
        name="s5_glu",
    )(proj, bbd_re, bbd_im, cbd_re, cbd_imn, a_flat, d, w_glu, b_glu)


def _s5_weights(a_re, a_im, log_dt, b_re, b_im, c_re, c_im):
    depth, g = a_re.shape[:2]
    flat = lambda z: z.reshape((depth * g,) + z.shape[2:])
    outs = s5_discretize(flat(a_re), flat(a_im), flat(log_dt),
                         flat(b_re).transpose(0, 2, 1), flat(b_im).transpose(0, 2, 1))
    ab_re, ab_im, bb_re, bb_im = [z.reshape((depth, g) + z.shape[1:]) for z in outs]
    eye = jnp.eye(SSM_PACK, dtype=F32)

    def pack_b(bb):
        bb = bb.reshape(depth, N_PACKS, SSM_PACK, SSM_GROUP, 1, SSM_STATE)
        m = bb * eye[None, None, :, None, :, None]
        return m.reshape(depth, N_PACKS, SSM_PACK * SSM_GROUP, SSM_PACK * SSM_STATE).astype(BF16)

    def pack_c(c):
        c = c.reshape(depth, N_PACKS, SSM_PACK, SSM_GROUP, SSM_STATE)
        m = c.transpose(0, 1, 2, 4, 3)[:, :, :, :, None, :] * eye[None, None, :, None, :, None]
        return m.reshape(depth, N_PACKS, SSM_PACK * SSM_STATE, SSM_PACK * SSM_GROUP).astype(BF16)

    a_flat = jnp.stack([ab_re.reshape(depth, -1), ab_im.reshape(depth, -1)], axis=1)
    return pack_b(bb_re), pack_b(bb_im), pack_c(c_re), pack_c(-c_im), a_flat


def _compress_kernel(r_ref, n_ref, pos_ref, w1_ref, w2_ref, o_ref):
    r = r_ref[0]
    rows = r.shape[0]
    rowid = lax.broadcasted_iota(jnp.int32, (rows, 1), 0)
    nxt = jnp.where(rowid == rows - 1, n_ref[0, 0:1, :], pltpu.roll(r, rows - 1, 0))
    first = (r + pos_ref[0:1, :]).astype(BF16)
    second = (nxt + pos_ref[1:2, :]).astype(BF16)
    hid = (jnp.dot(first, w1_ref[0], preferred_element_type=F32)
           + jnp.dot(second, w1_ref[1], preferred_element_type=F32))
    hid = jax.nn.gelu(hid).astype(BF16)
    o_ref[0] = jnp.dot(hid, w2_ref[...], preferred_element_type=F32).astype(o_ref.dtype)


def compress(kc_rows, pos2, w1e, w2e, layer):
    b, nc, w = kc_rows.shape
    tr = 128
    sub = 8
    of_layer = lambda z: pl.BlockSpec((None,) + z.shape[1:], lambda i, j: (layer,) + (0,) * (z.ndim - 1))
    nxt_blk = lambda i, j: (i, jnp.minimum((j + 1) * (tr // sub), nc // sub - 1), 0)
    return pl.pallas_call(
        _compress_kernel,
        grid=(b, nc // tr),
        in_specs=[pl.BlockSpec((1, tr, w), lambda i, j: (i, j, 0)),
                  pl.BlockSpec((1, sub, w), nxt_blk),
                  of_layer(pos2), of_layer(w1e), of_layer(w2e)],
        out_specs=pl.BlockSpec((1, tr, N_KV * HEAD_DIM), lambda i, j: (i, j, 0)),
        out_shape=jax.ShapeDtypeStruct((b, nc, N_KV * HEAD_DIM), BF16),
        compiler_params=_cparams(("parallel", "parallel")),
        name="compress",
    )(kc_rows, kc_rows, pos2, w1e, w2e)


def _compress_weights(pos, w1, w2):
    depth = pos.shape[0]
    eye = jnp.eye(N_KV, dtype=F32)
    w1 = w1.reshape(depth, 2, CMP_STRIDE, 1, HEAD_DIM, 1, CMP_HIDDEN)
    w1e = (w1 * eye[None, None, None, :, None, :, None]).reshape(
        depth, 2, CMP_STRIDE * N_KV * HEAD_DIM, N_KV * CMP_HIDDEN).astype(BF16)
    w2e = (w2[:, None, :, None, :] * eye[None, :, None, :, None]).reshape(
        depth, N_KV * CMP_HIDDEN, N_KV * HEAD_DIM).astype(BF16)
    pos2 = jnp.broadcast_to(pos.reshape(depth, 2, CMP_STRIDE, 1, HEAD_DIM),
                            (depth, 2, CMP_STRIDE, N_KV, HEAD_DIM)).reshape(depth, 2, -1)
    return pos2, w1e, w2e


def _kv_prep_kernel(ks_ref, vs_ref, kw_ref, vw_ref, ka_ref, vst_ref, kwp_ref, vwtp_ref):
    j = pl.program_id(1)
    ts = ks_ref.shape[0]
    n_sel = ka_ref.shape[3] - HEAD_DIM
    blk0 = jnp.maximum(j - 1, 0) * (ts // SEL_BLOCK)
    row_blk = blk0 + lax.broadcasted_iota(jnp.int32, (ts, n_sel), 0) // SEL_BLOCK
    col = lax.broadcasted_iota(jnp.int32, (ts, n_sel), 1)
    onehot = jnp.where(row_blk == col, ONEHOT_BIG, 0.0).astype(BF16)
    ks = ks_ref[...]
    vs_t = vs_ref[...].T
    pad_rows = lax.broadcasted_iota(jnp.int32, (V_ROWS - HEAD_DIM, ts), 0)
    extra = jnp.where(pad_rows == 0, 1.0, 0.0).astype(BF16)
    for g in range(N_KV):
        lo, hi = g * HEAD_DIM, (g + 1) * HEAD_DIM
        ka_ref[0, g, :, 0:n_sel] = onehot
        ka_ref[0, g, :, n_sel:] = ks[:, lo:hi].astype(BF16)
        vst_ref[0, g, 0:HEAD_DIM, :] = vs_t[lo:hi].astype(BF16)
        vst_ref[0, g, HEAD_DIM:, :] = extra

    @pl.when(j == 0)
    def _():
        kwp_ref[...] = jnp.zeros_like(kwp_ref)
        vwtp_ref[...] = jnp.zeros_like(vwtp_ref)

    @pl.when(j > 0)
    def _():
        kw = kw_ref[...]
        vw_t = vw_ref[...].T
        for g in range(N_KV):
            lo, hi = g * HEAD_DIM, (g + 1) * HEAD_DIM
            kwp_ref[0, g] = kw[:, lo:hi].astype(BF16)
            vwtp_ref[0, g] = vw_t[lo:hi].astype(BF16)


def kv_prep(proj, n_batch, seq):
    ts = WINDOW
    nblk = seq // ts
    n_sel = seq // SEL_BLOCK
    kvw = N_KV * HEAD_DIM
    src = lambda col: pl.BlockSpec((ts, kvw), lambda b, j: (b * nblk + jnp.maximum(j - 1, 0), col // kvw))
    same = lambda b, j: (b, 0, jnp.maximum(j - 1, 0), 0)
    same_t = lambda b, j: (b, 0, 0, jnp.maximum(j - 1, 0))
    return pl.pallas_call(
        _kv_prep_kernel,
        grid=(n_batch, nblk + 1),
        in_specs=[src(COL_KS), src(COL_VS), src(COL_KW), src(COL_VW)],
        out_specs=[pl.BlockSpec((1, N_KV, ts, n_sel + HEAD_DIM), same),
                   pl.BlockSpec((1, N_KV, V_ROWS, ts), same_t),
                   pl.BlockSpec((1, N_KV, ts, HEAD_DIM), lambda b, j: (b, 0, j, 0)),
                   pl.BlockSpec((1, N_KV, HEAD_DIM, ts), lambda b, j: (b, 0, 0, j))],
        out_shape=[jax.ShapeDtypeStruct((n_batch, N_KV, seq, n_sel + HEAD_DIM), BF16),
                   jax.ShapeDtypeStruct((n_batch, N_KV, V_ROWS, seq), BF16),
                   jax.ShapeDtypeStruct((n_batch, N_KV, seq + WINDOW, HEAD_DIM), BF16),
                   jax.ShapeDtypeStruct((n_batch, N_KV, HEAD_DIM, seq + WINDOW), BF16)],
        compiler_params=_cparams(("parallel", "arbitrary")),
        name="kv_prep",
    )(proj, proj, proj, proj)


def _softmax_cols(s):
    m = jnp.max(s, axis=0, keepdims=True)
    m = jnp.where(m < 0.5 * NEG, 0.0, m)
    e = jnp.exp2(s - m)
    return e * (1.0 / jnp.maximum(jnp.sum(e, axis=0, keepdims=True), 1e-30))


def _cmp_branch(a, qt, gate, k_ref, vt_ref, tbl_ref, smat_ref, o_ref, st_ref, rank_ref, other_branch):
    n_row_tiles = k_ref.shape[2] // CMP_LANE
    ap = a // (FAR_DIST // Q_TILE)

    def attend(n):
        rows = n * CMP_LANE
        s = jnp.dot(k_ref[0, 0, 0:rows, :], qt, preferred_element_type=F32)
        tbl = tbl_ref[0, 0]
        pieces = [s[c * CMP_LANE:(c + 1) * CMP_LANE] for c in range(n)]
        pieces[-1] = pieces[-1] + tbl[CMP_LANE:]
        if n > 1:
            pieces[-2] = pieces[-2] + tbl[:CMP_LANE]
        other = other_branch()
        p = _softmax_cols(jnp.concatenate(pieces, axis=0))
        o_ref[...] = other + gate * jnp.dot(vt_ref[0, 0, :, 0:rows], p.astype(BF16), preferred_element_type=F32)
        imp = (p[:, 0:Q_TILE] + p[:, Q_TILE:2 * Q_TILE] + p[:, 2 * Q_TILE:3 * Q_TILE]
               + p[:, 3 * Q_TILE:4 * Q_TILE])
        hi = imp.astype(BF16)
        r1 = imp - hi.astype(F32)
        mid = r1.astype(BF16)
        lo = (r1 - mid.astype(F32)).astype(BF16)
        smat = smat_ref[:, 0:rows]
        st_ref[...] = (jnp.dot(smat, hi, preferred_element_type=F32)
                       + jnp.dot(smat, mid, preferred_element_type=F32)
                       + jnp.dot(smat, lo, preferred_element_type=F32))

    for n in range(1, n_row_tiles + 1):
        pl.when(ap == n - 1)(lambda n=n: attend(n))

    o = o_ref[...]
    st = st_ref[...]
    t = a * Q_TILE + lax.broadcasted_iota(jnp.int32, st.shape, 1)
    jj = lax.broadcasted_iota(jnp.int32, st.shape, 0)
    cur_blk = t // SEL_BLOCK
    valid = jj * SEL_BLOCK <= t
    forced = (jj == 0) | (jj == cur_blk) | (jj == cur_blk - 1)
    st = jnp.where(valid, jnp.where(forced, FORCED_SCORE, st), -1.0)
    st_ref[...] = st
    rank_ref[...] = jnp.zeros_like(rank_ref)
    sub = 8
    chunk = 16
    n_sel = st.shape[0]
    last_valid = (a * Q_TILE + Q_TILE - 1) // SEL_BLOCK
    for q0 in range(0, n_sel, chunk):
        @pl.when(q0 <= last_valid)
        def _(q0=q0):
            for v0 in range(0, n_sel, chunk):
                @pl.when(v0 <= last_valid)
                def _(v0=v0):
                    for v in range(v0 // sub, min(v0 + chunk, n_sel) // sub):
                        sv = st_ref[v * sub:(v + 1) * sub, :]
                        jv = jj[v * sub:(v + 1) * sub]
                        cnt = rank_ref[v * sub:(v + 1) * sub, :]
                        for jp in range(q0, min(q0 + chunk, n_sel)):
                            row = st_ref[jp:jp + 1, :]
                            if jp < v * sub:
                                ahead = row >= sv
                            elif jp >= (v + 1) * sub:
                                ahead = row > sv
                            else:
                                ahead = (row > sv) | ((row == sv) & (jv > jp))
                            cnt = cnt + jnp.where(ahead, 1.0, 0.0)
                        rank_ref[v * sub:(v + 1) * sub, :] = cnt

    keep = (rank_ref[...] < SEL_TOP) & (st >= 0.0)
    return o, (keep.astype(F32) - 1.0).astype(BF16)


def _sel_branch(a, gate, ka_ref, vt_ref, tbl_ref, qa_ref, m_ref, acc_ref):
    tk = SEL_KEY_TILE
    acc_ref[...] = jnp.zeros_like(acc_ref)
    n_tiles = a // (tk // Q_TILE) + 1
    far_pairs = jnp.maximum(0, (a - (SEL_NEAR_TILES - 5)) // 4)
    far_hexes = far_pairs // 8
    far_octs = (far_pairs - 8 * far_hexes) // 4
    far_quads = (far_pairs - 8 * far_hexes - 4 * far_octs) // 2
    first_near = 2 * far_pairs
    near_quads = (n_tiles - first_near) // 4
    near_pairs = (n_tiles - first_near - 4 * near_quads + 1) // 2

    def key_off(c):
        return pl.multiple_of(jnp.minimum(c, n_tiles - 1) * tk, tk)

    def attend(c0, n, biased):
        m_old = m_ref[...]
        acc_old = acc_ref[...]

        whole = n >= 4
        span = pl.ds(pl.multiple_of(c0 * tk, tk), n * tk)

        def scores():
            if whole:
                s_all = jnp.dot(ka_ref[0, 0, span, :], qa_ref[...], preferred_element_type=F32)
                ss = [s_all[h * tk:(h + 1) * tk] for h in range(n)]
            else:
                ss = [jnp.dot(ka_ref[0, 0, pl.ds(key_off(c0 + h), tk), :], qa_ref[...],
                              preferred_element_type=F32) for h in range(n)]
            if biased:
                ss = [s + tbl_ref[0, jnp.where(c0 + h < n_tiles, a - 2 * (c0 + h), SEL_NEAR_TILES - 1)]
                      for h, s in enumerate(ss)]
            return ss

        def weighted(ss, m):
            ps = [jnp.exp2(s - m).astype(BF16) for s in ss]
            if whole:
                return jnp.dot(vt_ref[0, 0, :, span], jnp.concatenate(ps, axis=0), preferred_element_type=F32)
            return sum(jnp.dot(vt_ref[0, 0, :, pl.ds(key_off(c0 + h), tk)], p, preferred_element_type=F32)
                       for h, p in enumerate(ps))

        ss = scores()
        m_new = m_old
        for s in ss:
            m_new = jnp.maximum(m_new, jnp.max(s, axis=0, keepdims=True))
        acc_ref[...] = (acc_old + weighted(ss, m_old)) * jnp.exp2(m_old - m_new)
        m_ref[...] = m_new

        @pl.when(jnp.max(m_new - m_old) > SAFE_LOG2_RANGE)
        def _():
            acc_ref[...] = jnp.exp2(m_old - m_new) * acc_old + weighted(scores(), m_new)

    def body(first, n, biased):
        def step(i, carry):
            attend(first + n * i, n, biased)
            return carry
        return step

    sub = 8
    m_ref[...] = jnp.dot(ka_ref[0, 0, 0:sub, :], qa_ref[...], preferred_element_type=F32)[0:1, :]
    done = 16 * far_hexes
    lax.fori_loop(0, far_hexes, body(0, 16, False), 0)
    lax.fori_loop(0, far_octs, body(done, 8, False), 0)
    done = done + 8 * far_octs
    lax.fori_loop(0, far_quads, body(done, 4, False), 0)
    done = done + 4 * far_quads
    lax.fori_loop(0, (first_near - done) // 2, body(done, 2, False), 0)
    lax.fori_loop(0, near_quads, body(first_near, 4, True), 0)
    lax.fori_loop(0, near_pairs, body(first_near + 4 * near_quads, 2, True), 0)
    acc = acc_ref[...]
    return acc[0:HEAD_DIM] * (gate / acc[HEAD_DIM:HEAD_DIM + 1])


def _win_branch(a, qt, gate, k_ref, vt_ref, tbl_ref):
    span = WINDOW + Q_TILE
    off = pl.multiple_of(a * Q_TILE, Q_TILE)
    s = jnp.dot(k_ref[0, 0, pl.ds(off, span), :], qt, preferred_element_type=F32)
    s = s + tbl_ref[0, 0]
    e = jnp.exp2(s - jnp.max(s, axis=0, keepdims=True))
    ev = jnp.dot(vt_ref[0, 0, :, pl.ds(off, span)], e.astype(BF16), preferred_element_type=F32)
    return ev * (gate / jnp.sum(e, axis=0, keepdims=True))


def _nsa_kernel(q_ref, g_ref, kc_ref, vtc_ref, tblc_ref, smat_ref, ka_ref, vst_ref, tbls_ref,
                kw_ref, vwt_ref, tblw_ref, o_ref, gt_ref, oc_ref, st_ref, rank_ref, qa_ref, m_ref, acc_ref):
    g = pl.program_id(0)
    a = pl.program_id(2)
    n_sel = smat_ref.shape[0]
    half = 2 * HEAD_DIM
    q = q_ref[...] * (HEAD_DIM ** -0.5) * LOG2E
    t0 = q[:, :half].T
    t1 = q[:, half:].T
    qt = jnp.concatenate([t0[:HEAD_DIM], t0[HEAD_DIM:], t1[:HEAD_DIM], t1[HEAD_DIM:]],
                         axis=1).astype(BF16)
    gt_ref[...] = g_ref[:, :Q_TILE].T

    def gate(branch):
        rows = [gt_ref[pl.ds((g * GQA_R + r) * 3 + branch, 1), :] for r in range(GQA_R)]
        return jax.nn.sigmoid(jnp.concatenate(rows, axis=1))

    o, sel = _cmp_branch(a, qt, gate(0), kc_ref, vtc_ref, tblc_ref, smat_ref, oc_ref, st_ref, rank_ref,
                         functools.partial(_win_branch, a, qt, gate(2), kw_ref, vwt_ref, tblw_ref))
    for r in range(GQA_R):
        qa_ref[0:n_sel, r * Q_TILE:(r + 1) * Q_TILE] = sel
    qa_ref[n_sel:, :] = qt
    o = o + _sel_branch(a, gate(1), ka_ref, vst_ref, tbls_ref, qa_ref, m_ref, acc_ref)
    u0 = jnp.concatenate([o[:, 0:Q_TILE], o[:, Q_TILE:2 * Q_TILE]], axis=0).T
    u1 = jnp.concatenate([o[:, 2 * Q_TILE:3 * Q_TILE], o[:, 3 * Q_TILE:]], axis=0).T
    o_ref[...] = jnp.concatenate([u0, u1], axis=1).astype(o_ref.dtype)


def nsa_attention(proj, n_batch, seq, k_cmp, vt_cmp, ka, vst, kwp, vwtp, tables, smat):
    tbl_c, tbl_s, tbl_w = tables
    n_a = seq // Q_TILE
    nc = k_cmp.shape[2]
    n_sel = smat.shape[0]
    dh = HEAD_DIM
    gw = GQA_R * HEAD_DIM
    tps = FAR_DIST // Q_TILE
    per_bg = lambda gi, bi, ai: (bi, gi, 0, 0)
    return pl.pallas_call(
        _nsa_kernel,
        grid=(N_KV, n_batch, n_a),
        in_specs=[pl.BlockSpec((Q_TILE, gw), lambda gi, bi, ai: (bi * n_a + ai, COL_Q // gw + gi)),
                  pl.BlockSpec((Q_TILE, gw), lambda gi, bi, ai: (bi * n_a + ai, COL_GATE // gw)),
                  pl.BlockSpec((1, 1, nc, dh), per_bg),
                  pl.BlockSpec((1, 1, dh, nc), per_bg),
                  pl.BlockSpec((1, 1, 2 * CMP_LANE, ROWS), lambda gi, bi, ai: (gi, ai % tps, 0, 0)),
                  pl.BlockSpec(smat.shape, lambda gi, bi, ai: (0, 0)),
                  pl.BlockSpec((1, 1, seq, n_sel + dh), per_bg),
                  pl.BlockSpec((1, 1, V_ROWS, seq), per_bg),
                  pl.BlockSpec((1, SEL_NEAR_TILES, SEL_KEY_TILE, ROWS), lambda gi, bi, ai: (gi, 0, 0, 0)),
                  pl.BlockSpec((1, 1, seq + WINDOW, dh), per_bg),
                  pl.BlockSpec((1, 1, dh, seq + WINDOW), per_bg),
                  pl.BlockSpec((1, 1, WINDOW + Q_TILE, ROWS),
                               lambda gi, bi, ai: (gi, jnp.minimum(ai, WINDOW // Q_TILE), 0, 0))],
        out_specs=pl.BlockSpec((Q_TILE, gw), lambda gi, bi, ai: (bi * n_a + ai, gi)),
        out_shape=jax.ShapeDtypeStruct((n_batch * seq, N_HEADS * HEAD_DIM), BF16),
        scratch_shapes=[pltpu.VMEM((Q_TILE, Q_TILE), F32),
                        pltpu.VMEM((dh, ROWS), F32),
                        pltpu.VMEM((n_sel, Q_TILE), F32), pltpu.VMEM((n_sel, Q_TILE), F32),
                        pltpu.VMEM((n_sel + dh, ROWS), BF16), pltpu.VMEM((1, ROWS), F32),
                        pltpu.VMEM((V_ROWS, ROWS), F32)],
        compiler_params=_cparams(("parallel", "parallel", "arbitrary")),
        name="nsa_attention",
    )(proj, proj, k_cmp, vt_cmp, tbl_c, smat, ka, vst, tbl_s, kwp, vwtp, tbl_w)


def _rel_bucket(dist):
    n = jnp.maximum(dist, 0)
    nf = jnp.maximum(n, 1).astype(F32)
    large = RPB_MAX_EXACT + (jnp.log(nf / RPB_MAX_EXACT) / math.log(RPB_MAX_DIST / RPB_MAX_EXACT)
                             * (N_BUCKETS - RPB_MAX_EXACT)).astype(jnp.int32)
    large = jnp.minimum(large, N_BUCKETS - 1)
    return jnp.where(n < RPB_MAX_EXACT, n, large)


def _shifted_rows(e, n_rows, stride):
    return pltpu.roll(jnp.broadcast_to(e, (n_rows, e.shape[1])), 0, 1, stride=stride, stride_axis=0)


def _bias_tables_kernel(ec_ref, es_ref, ew_ref, tc_ref, ts_ref, tw_ref):
    n_m = tc_ref.shape[1]
    n_am = tc_ref.shape[0]
    mc = _shifted_rows(ec_ref[...], n_m, CMP_STRIDE)
    first = mc.shape[1] - n_am * Q_TILE
    for am in range(n_am):
        tc_ref[am] = mc[:, first + am * Q_TILE:first + (am + 1) * Q_TILE]
    tk = ts_ref.shape[1]
    for k in range(ts_ref.shape[0]):
        ts_ref[k] = _shifted_rows(es_ref[k:k + 1, :], tk, 1)[:, tk:tk + Q_TILE]
    span = tw_ref.shape[1]
    mw = _shifted_rows(ew_ref[...], span, 1)[:, span:span + Q_TILE]
    kj = lax.broadcasted_iota(jnp.int32, mw.shape, 0)
    for v in range(tw_ref.shape[0]):
        tw_ref[v] = jnp.where(kj >= WINDOW - v * Q_TILE, mw, NEG)


def _bias_tables(rpb_table):
    tab = rpb_table.astype(F32)
    by_dist = tab[_rel_bucket(jnp.arange(FAR_DIST))].T * LOG2E
    rel = by_dist - (tab[N_BUCKETS - 1] * LOG2E)[:, None]
    neg = lambda n: jnp.full((N_HEADS, n), NEG, F32)
    zero = lambda n: jnp.zeros((N_HEADS, n), F32)
    tk, span, n_m, n_am = SEL_KEY_TILE, WINDOW + Q_TILE, 2 * CMP_LANE, FAR_DIST // Q_TILE

    lead = CMP_STRIDE * n_m - (FAR_DIST - CMP_BLOCK + 1)
    e_c = jnp.concatenate([neg(lead), rel, zero(CMP_STRIDE * n_m - lead)], axis=1)[:, None, :]
    ext = jnp.concatenate([neg(tk), rel], axis=1)
    e_s = jnp.stack([ext[:, Q_TILE * k:Q_TILE * k + tk + Q_TILE] for k in range(SEL_NEAR_TILES - 1)]
                    + [neg(tk + Q_TILE)], axis=1)
    e_w = jnp.concatenate([neg(Q_TILE), by_dist[:, :WINDOW], neg(Q_TILE)], axis=1)[:, None, :]

    n_var = WINDOW // Q_TILE + 1
    head = lambda h: (h // GQA_R, 0, 0, h % GQA_R)
    return pl.pallas_call(
        _bias_tables_kernel,
        grid=(N_HEADS,),
        in_specs=[pl.BlockSpec((None,) + e_c.shape[1:], lambda h: (h, 0, 0)),
                  pl.BlockSpec((None,) + e_s.shape[1:], lambda h: (h, 0, 0)),
                  pl.BlockSpec((None,) + e_w.shape[1:], lambda h: (h, 0, 0))],
        out_specs=[pl.BlockSpec((None, n_am, n_m, Q_TILE), head),
                   pl.BlockSpec((None, SEL_NEAR_TILES, tk, Q_TILE), head),
                   pl.BlockSpec((None, n_var, span, Q_TILE), head)],
        out_shape=[jax.ShapeDtypeStruct((N_KV, n_am, n_m, ROWS), F32),
                   jax.ShapeDtypeStruct((N_KV, SEL_NEAR_TILES, tk, ROWS), F32),
                   jax.ShapeDtypeStruct((N_KV, n_var, span, ROWS), F32)],
        compiler_params=_cparams(("parallel",)),
        name="bias_tables",
    )(e_c, e_s, e_w)


def _slc_matrix(n_cmp_pad, n_sel):
    sel_ratio, cmp_ratio = SEL_BLOCK // CMP_STRIDE, CMP_BLOCK // CMP_STRIDE
    mat = np.zeros((n_sel, n_cmp_pad), np.float32)
    for j in range(n_sel):
        for mm in range(sel_ratio):
            for nn in range(cmp_ratio):
                idx = sel_ratio * j - mm - nn
                if 0 <= idx < n_cmp_pad - 1:
                    mat[j, idx] += 1.0
    return jnp.asarray(mat, BF16)


def nsa_mixer(proj, n_batch, seq, cmp_k_w, cmp_v_w, tables, smat, layer):
    kv = N_KV * HEAD_DIM
    rows16 = lambda col: proj[:, col:col + kv].reshape(n_batch, seq // CMP_STRIDE, CMP_STRIDE * kv)
    k_cmp = compress(rows16(COL_KC), *cmp_k_w, layer)
    v_cmp = compress(rows16(COL_VC), *cmp_v_w, layer)
    k_cmp = k_cmp.reshape(n_batch, -1, N_KV, HEAD_DIM).transpose(0, 2, 1, 3)
    vt_cmp = v_cmp.reshape(n_batch, -1, N_KV, HEAD_DIM).transpose(0, 2, 3, 1)
    ka, vst, kwp, vwtp = kv_prep(proj, n_batch, seq)
    return nsa_attention(proj, n_batch, seq, k_cmp, vt_cmp, ka, vst, kwp, vwtp, tables, smat)


def kernel(x, rpb_table, attn_norm, ffn_norm, final_norm, w_in, ssm_a_re, ssm_a_im, ssm_log_dt,
           ssm_b_re, ssm_b_im, ssm_c_re, ssm_c_im, ssm_d, w_glu, b_glu, cmp_pos_k, cmp_w1_k, cmp_w2_k,
           cmp_pos_v, cmp_w1_v, cmp_w2_v, w_out, w_ffn_gate, w_ffn_up, w_ffn_down):
    n_batch, seq, d_model = x.shape
    depth = w_in.shape[0]
    tables = _bias_tables(rpb_table)
    smat = _slc_matrix(seq // CMP_STRIDE, seq // SEL_BLOCK)

    w_in_b = jnp.pad(w_in, ((0, 0), (0, 0), (0, IN_COLS_PAD - IN_COLS))).astype(BF16)
    w_glu_b, w_out_b = w_glu.astype(BF16), w_out.astype(BF16)
    w_gate_b, w_up_b, w_down_b = w_ffn_gate.astype(BF16), w_ffn_up.astype(BF16), w_ffn_down.astype(BF16)

    s5w = _s5_weights(ssm_a_re, ssm_a_im, ssm_log_dt, ssm_b_re, ssm_b_im, ssm_c_re, ssm_c_im)
    cmp_k_w = _compress_weights(cmp_pos_k, cmp_w1_k, cmp_w2_k)
    cmp_v_w = _compress_weights(cmp_pos_v, cmp_w1_v, cmp_w2_v)

    xf = x.reshape(n_batch * seq, d_model)
    for l in range(depth):
        proj = norm_matmul(xf, attn_norm[l].reshape(1, -1), w_in_b, l)
        y_ssm = s5_glu(proj, n_batch, *s5w, ssm_d[l].reshape(1, -1), w_glu_b, b_glu[l].reshape(1, -1), l)
        y_nsa = nsa_mixer(proj, n_batch, seq, cmp_k_w, cmp_v_w, tables, smat, l)
        xf = out_proj(xf, y_ssm, y_nsa, w_out_b, l)
        xf = ffn(xf, ffn_norm[l].reshape(1, -1), w_gate_b, w_up_b, w_down_b, l, final_norm.reshape(1, -1),
                 norm_out=l == depth - 1)
    return xf.reshape(n_batch, seq, d_model)
```
